```python
import jax, jax.numpy as jnp
from jax import lax
import numpy as np

D_MODEL = 2048
BATCH = 2
SEQ = 8192
DEPTH = 1

EPS = 1e-5
POOL_GROUPS = 4
POOL_WINDOWS = (2, 4, 8, 16)
POOL_WIDTH = D_MODEL
POOL_GW = POOL_WIDTH // POOL_GROUPS
POOL_OUT_GW = D_MODEL // POOL_GROUPS
SSD_EXPAND = 2
SSD_INNER = SSD_EXPAND * D_MODEL
SSD_HEADDIM = 64
SSD_HEADS = SSD_INNER // SSD_HEADDIM
SSD_GROUPS = 8
SSD_HPG = SSD_HEADS // SSD_GROUPS
SSD_STATE = 128
SSD_CONV = 4
SSD_CHUNK = 128
SSD_CONV_DIM = SSD_INNER + 2 * SSD_GROUPS * SSD_STATE
N_EXPERTS = 32
TOP_K = 4
D_FF = D_MODEL
SWIGLU_ALPHA = 1.702
SWIGLU_LIMIT = 7.0
MOE_BLOCK = 256
IN_COLS = POOL_WIDTH + SSD_INNER + SSD_CONV_DIM + SSD_HEADS + 2 * D_MODEL
IN_SPLITS = (POOL_WIDTH,
             POOL_WIDTH + SSD_INNER,
             POOL_WIDTH + SSD_INNER + SSD_CONV_DIM,
             POOL_WIDTH + SSD_INNER + SSD_CONV_DIM + SSD_HEADS,
             POOL_WIDTH + SSD_INNER + SSD_CONV_DIM + SSD_HEADS + D_MODEL)

kernel_name = "hybrid_pool_ssd_moe_block"


def rmsnorm(x, w):
    xf = x.astype(jnp.float32)
    y = xf * lax.rsqrt(jnp.mean(xf * xf, axis=-1, keepdims=True) + EPS)
    return (y * w.astype(jnp.float32)).astype(x.dtype)


def causal_pool_mixer(xp, w_pool, pool_scale):
    B, S = xp.shape[:2]
    xg = xp.reshape(B, S, POOL_GROUPS, POOL_GW).astype(jnp.float32)
    csum = jnp.cumsum(xg, axis=1)
    t = jnp.arange(1, S + 1, dtype=jnp.float32)
    means = []
    for g, w in enumerate(POOL_WINDOWS):
        c = csum[:, :, g]
        lagged = jnp.pad(c, ((0, 0), (w, 0), (0, 0)))[:, :S]
        means.append((c - lagged) / jnp.minimum(t, float(w))[None, :, None])
    pooled = jnp.stack(means, axis=2) - xg
    y = jnp.einsum('bsgi,gio->bsgo', pooled.astype(xp.dtype), w_pool)
    return y.reshape(B, S, D_MODEL) * pool_scale


def ssd_mixer(z, xbc, dt_raw, conv_w, conv_b, dt_bias, a_log, d_skip, ssd_norm_w, w_ssd_out):
    B, S = z.shape[:2]
    nc, L, G, J, P, N = S // SSD_CHUNK, SSD_CHUNK, SSD_GROUPS, SSD_HPG, SSD_HEADDIM, SSD_STATE
    xpad = jnp.pad(xbc, ((0, 0), (SSD_CONV - 1, 0), (0, 0)))
    conv = conv_b + sum(xpad[:, k:k + S] * conv_w[k] for k in range(SSD_CONV))
    conv = jax.nn.silu(conv.astype(jnp.float32))
    xs, bm, cm = jnp.split(conv, [SSD_INNER, SSD_INNER + G * N], axis=-1)
    xs = xs.reshape(B, nc, L, G, J, P)
    bm = bm.reshape(B, nc, L, G, N)
    cm = cm.reshape(B, nc, L, G, N)
    dt = jax.nn.softplus(dt_raw.astype(jnp.float32) + dt_bias.astype(jnp.float32))
    a = -jnp.exp(a_log.astype(jnp.float32))
    dt = dt.reshape(B, nc, L, G, J)
    a_cum = jnp.cumsum(dt * a.reshape(G, J), axis=2)
    x_dt = xs * dt[..., None]
    pos = jnp.arange(L)
    causal = (pos[:, None] >= pos[None, :])[None, None, :, :, None, None]
    seg = a_cum[:, :, :, None] - a_cum[:, :, None, :]
    decay = jnp.exp(jnp.where(causal, seg, -jnp.inf))
    cb = jnp.einsum('bclgn,bcsgn->bclsg', cm, bm)
    y_diag = jnp.einsum('bclsgj,bcsgjp->bclgjp', cb[..., None] * decay, x_dt)
    decay_to_end = jnp.exp(a_cum[:, :, -1:] - a_cum)
    states = jnp.einsum('bcsgn,bcsgjp->bcgjpn', bm, x_dt * decay_to_end[..., None])
    chunk_decay = jnp.exp(a_cum[:, :, -1])

    def step(carry, inp):
        st, dec = inp
        return carry * dec[..., None, None] + st, carry

    init = jnp.zeros((B, G, J, P, N), jnp.float32)
    _, prev_states = lax.scan(step, init, (jnp.moveaxis(states, 1, 0), jnp.moveaxis(chunk_decay, 1, 0)))
    prev_states = jnp.moveaxis(prev_states, 0, 1)
    y_off = jnp.einsum('bclgn,bcgjpn->bclgjp', cm, prev_states) * jnp.exp(a_cum)[..., None]
    y = y_diag + y_off + d_skip.astype(jnp.float32).reshape(G, J)[:, :, None] * xs
    y = y.reshape(B, S, SSD_INNER)
    yg = (y * jax.nn.silu(z.astype(jnp.float32))).reshape(B, S, G, SSD_INNER // G)
    yg = yg * lax.rsqrt(jnp.mean(yg * yg, axis=-1, keepdims=True) + EPS)
    yg = yg.reshape(B, S, SSD_INNER) * ssd_norm_w.astype(jnp.float32)
    return yg.astype(z.dtype) @ w_ssd_out


def clamped_swiglu(h):
    glu, lin = jnp.split(h, 2, axis=-1)
    glu = jnp.minimum(glu, SWIGLU_LIMIT)
    lin = jnp.clip(lin, -SWIGLU_LIMIT, SWIGLU_LIMIT)
    return glu * jax.nn.sigmoid(SWIGLU_ALPHA * glu) * (lin + 1.0)


def moe_ffn(u, router_w, router_b, w_up, b_up, w_down, b_down):
    B, S, D = u.shape
    T = B * S
    TK = T * TOP_K
    ut = u.reshape(T, D)
    logits = (ut @ router_w).astype(jnp.float32) + router_b.astype(jnp.float32)
    top_logits, top_idx = lax.top_k(logits, TOP_K)
    gates = jax.nn.softmax(top_logits, axis=-1)
    flat_e = top_idx.reshape(TK).astype(jnp.int32)
    order = jnp.argsort(flat_e, stable=True)
    sorted_e = flat_e[order]
    counts = jnp.bincount(flat_e, length=N_EXPERTS)
    padded = (counts + MOE_BLOCK - 1) // MOE_BLOCK * MOE_BLOCK
    start = jnp.cumsum(counts) - counts
    pend = jnp.cumsum(padded)
    pstart = pend - padded
    dest_sorted = pstart[sorted_e] + jnp.arange(TK, dtype=jnp.int32) - start[sorted_e]
    dest = jnp.zeros((TK,), jnp.int32).at[order].set(dest_sorted.astype(jnp.int32))
    n_blocks = -(-TK // MOE_BLOCK) + N_EXPERTS
    n_slots = n_blocks * MOE_BLOCK
    tok = jnp.arange(TK, dtype=jnp.int32) // TOP_K
    slot_tok = jnp.full((n_slots,), T, jnp.int32).at[dest].set(tok)
    u_pad = jnp.concatenate([ut, jnp.zeros((1, D), ut.dtype)], axis=0)
    xb = u_pad[slot_tok].reshape(n_blocks, MOE_BLOCK, D)
    block_start = jnp.arange(n_blocks, dtype=jnp.int32) * MOE_BLOCK
    block_expert = jnp.minimum(jnp.searchsorted(pend, block_start, side='right'), N_EXPERTS - 1)

    def expert_block(args):
        xblk, e = args
        hgu = xblk @ w_up[e] + b_up[e]
        return clamped_swiglu(hgu) @ w_down[e] + b_down[e]

    yb = lax.map(expert_block, (xb, block_expert)).reshape(n_slots, D)
    y = jnp.einsum('tk,tkd->td', gates.astype(u.dtype), yb[dest].reshape(T, TOP_K, D))
    return y.reshape(B, S, D)


def setup_inputs(seed: int = 0) -> dict:
    key = jax.random.key(seed)
    ks = jax.random.split(key, 24)
    f32 = jnp.float32
    nrm = lambda k, shape, scale: jax.random.normal(k, shape, f32) * scale
    dt_init = jnp.exp(jax.random.uniform(ks[7], (SSD_HEADS,), f32, np.log(1e-3), np.log(1e-1)))
    return {
        "x": nrm(ks[0], (BATCH, SEQ, D_MODEL), 1.0),
        "norm_mix_w": 1.0 + nrm(ks[1], (D_MODEL,), 0.05),
        "w_in": nrm(ks[2], (D_MODEL, IN_COLS), D_MODEL ** -0.5),
        "w_pool": nrm(ks[3], (POOL_GROUPS, POOL_GW, POOL_OUT_GW), POOL_GW ** -0.5),
        "pool_scale": 1.0 + nrm(ks[4], (D_MODEL,), 0.1),
        "conv_w": nrm(ks[5], (SSD_CONV, SSD_CONV_DIM), 0.5),
        "conv_b": nrm(ks[6], (SSD_CONV_DIM,), 0.01),
        "dt_bias": dt_init + jnp.log(-jnp.expm1(-dt_init)),
        "a_log": jnp.log(jax.random.uniform(ks[8], (SSD_HEADS,), f32, 1.0, 16.0)),
        "d_skip": 1.0 + nrm(ks[9], (SSD_HEADS,), 0.1),
        "ssd_norm_w": 1.0 + nrm(ks[10], (SSD_INNER,), 0.05),
        "w_ssd_out": nrm(ks[11], (SSD_INNER, D_MODEL), SSD_INNER ** -0.5),
        "w_out": nrm(ks[12], (D_MODEL, D_MODEL), D_MODEL ** -0.5),
        "norm_ffn_w": 1.0 + nrm(ks[13], (D_MODEL,), 0.05),
        "router_w": nrm(ks[14], (D_MODEL, N_EXPERTS), D_MODEL ** -0.5),
        "router_b": nrm(ks[15], (N_EXPERTS,), 0.01),
        "w_up": nrm(ks[16], (N_EXPERTS, D_MODEL, 2 * D_FF), D_MODEL ** -0.5),
        "b_up": nrm(ks[17], (N_EXPERTS, 2 * D_FF), 0.01),
        "w_down": nrm(ks[18], (N_EXPERTS, D_FF, D_MODEL), D_FF ** -0.5),
        "b_down": nrm(ks[19], (N_EXPERTS, D_MODEL), 0.01),
        "norm_final_w": 1.0 + nrm(ks[20], (D_MODEL,), 0.05),
    }


def reference(x, norm_mix_w, w_in, w_pool, pool_scale, conv_w, conv_b, dt_bias, a_log, d_skip,
              ssd_norm_w, w_ssd_out, w_out, norm_ffn_w, router_w, router_b, w_up, b_up,
              w_down, b_down, norm_final_w):
    h = x
    for _ in range(DEPTH):
        u = rmsnorm(h, norm_mix_w)
        proj = u @ w_in
        xp, z, xbc, dt_raw, g_pool, g_ssd = jnp.split(proj, IN_SPLITS, axis=-1)
        pool_out = causal_pool_mixer(xp, w_pool, pool_scale)
        ssd_out = ssd_mixer(z, xbc, dt_raw, conv_w, conv_b, dt_bias, a_log, d_skip,
                            ssd_norm_w, w_ssd_out)
        merged = jax.nn.sigmoid(g_pool) * pool_out + jax.nn.sigmoid(g_ssd) * ssd_out
        h = h + merged @ w_out
        h = h + moe_ffn(rmsnorm(h, norm_ffn_w), router_w, router_b, w_up, b_up, w_down, b_down)
    return rmsnorm(h, norm_final_w)
```

```python
import functools

import jax
import jax.numpy as jnp
from jax import lax
from jax.experimental import pallas as pl
from jax.experimental.pallas import tpu as pltpu

F32 = jnp.float32
BF16 = jnp.bfloat16

D_MODEL = 2048
EPS = 1e-5
POOL_GROUPS = 4
POOL_WINDOWS = (2, 4, 8, 16)
POOL_GW = D_MODEL // POOL_GROUPS
POOL_HALO = 16
SSD_INNER = 2 * D_MODEL
SSD_HEADDIM = 64
SSD_HEADS = SSD_INNER // SSD_HEADDIM
SSD_GROUPS = 8
SSD_HPG = SSD_HEADS // SSD_GROUPS
SSD_STATE = 128
SSD_CONV = 4
SSD_CHUNK = 128
SSD_GW = SSD_HPG * SSD_HEADDIM
SSD_BC = SSD_GROUPS * SSD_STATE
N_EXPERTS = 32
TOP_K = 4
D_FF = D_MODEL
SWIGLU_ALPHA = 1.702
SWIGLU_LIMIT = 7.0

LANES = 128
CONV_TAIL = 8
VMEM_LIMIT = 56 * 1024 * 1024

COL_Z = 0
COL_XS = SSD_INNER
COL_BC = 2 * SSD_INNER
COL_P = 2 * SSD_INNER + 2 * SSD_BC
COL_GP = COL_P + D_MODEL
COL_GS = COL_GP + D_MODEL
PROJ_COLS = COL_GS + D_MODEL

NEG_BIG = -1e30


def _cparams(sem):
    return pltpu.CompilerParams(dimension_semantics=sem, vmem_limit_bytes=VMEM_LIMIT)


def _inproj_kernel(x_ref, nw_ref, w_ref, wdt_ref, o_ref, dt_ref, u_scr):
    @pl.when(pl.program_id(1) == 0)
    def _():
        x = x_ref[...]
        ms = jnp.mean(x * x, axis=-1, keepdims=True)
        u = (x * lax.rsqrt(ms + EPS) * nw_ref[...]).astype(BF16)
        u_scr[...] = u
        dt_ref[...] = jnp.dot(u, wdt_ref[...], preferred_element_type=F32)

    o_ref[...] = jnp.dot(u_scr[...], w_ref[...], preferred_element_type=F32).astype(o_ref.dtype)


def _inproj(x2, norm_w, w_main, w_dt, tm, tn):
    T = x2.shape[0]
    return pl.pallas_call(
        _inproj_kernel,
        grid=(T // tm, PROJ_COLS // tn),
        in_specs=[
            pl.BlockSpec((tm, D_MODEL), lambda i, j: (i, 0)),
            pl.BlockSpec((1, D_MODEL), lambda i, j: (0, 0)),
            pl.BlockSpec((D_MODEL, tn), lambda i, j: (0, j)),
            pl.BlockSpec((D_MODEL, LANES), lambda i, j: (0, 0)),
        ],
        out_specs=[
            pl.BlockSpec((tm, tn), lambda i, j: (i, j)),
            pl.BlockSpec((tm, LANES), lambda i, j: (i, 0)),
        ],
        out_shape=[
            jax.ShapeDtypeStruct((T, PROJ_COLS), BF16),
            jax.ShapeDtypeStruct((T, LANES), F32),
        ],
        scratch_shapes=[pltpu.VMEM((tm, D_MODEL), BF16)],
        compiler_params=_cparams(("parallel", "arbitrary")),
        name="inproj",
    )(x2, norm_w, w_main, w_dt)


def _pool_kernel(halo_ref, p_ref, g_ref, wp_ref, ps_ref, o_ref, buf, *, tm, seq):
    row0 = (pl.program_id(0) * tm) % seq
    halo = jnp.where(row0 == 0, 0.0, halo_ref[...].astype(F32))
    pos = row0 + lax.broadcasted_iota(jnp.int32, (tm, 1), 0)
    for g, w in enumerate(POOL_WINDOWS):
        cols = slice(g * POOL_GW, (g + 1) * POOL_GW)
        buf[0:POOL_HALO, :] = halo[:, cols]
        buf[POOL_HALO:, :] = p_ref[:, cols].astype(F32)
        cur = buf[POOL_HALO:POOL_HALO + tm, :]
        acc = cur
        for k in range(1, w):
            acc = acc + buf[POOL_HALO - k:POOL_HALO - k + tm, :]
        cnt = jnp.minimum(pos + 1, w).astype(F32)
        pooled = acc / cnt - cur
        y = jnp.dot(pooled.astype(BF16), wp_ref[g], preferred_element_type=F32)
        y = y * ps_ref[:, cols] * jax.nn.sigmoid(g_ref[:, cols].astype(F32))
        o_ref[:, cols] = y.astype(o_ref.dtype)


def _pool(proj, w_pool, pool_scale, tm, seq):
    T = proj.shape[0]
    hb = tm // POOL_HALO
    return pl.pallas_call(
        functools.partial(_pool_kernel, tm=tm, seq=seq),
        grid=(T // tm,),
        in_specs=[
            pl.BlockSpec((POOL_HALO, D_MODEL),
                         lambda i: (jnp.maximum(i * hb - 1, 0), COL_P // D_MODEL)),
            pl.BlockSpec((tm, D_MODEL), lambda i: (i, COL_P // D_MODEL)),
            pl.BlockSpec((tm, D_MODEL), lambda i: (i, COL_GP // D_MODEL)),
            pl.BlockSpec((POOL_GROUPS, POOL_GW, POOL_GW), lambda i: (0, 0, 0)),
            pl.BlockSpec((1, D_MODEL), lambda i: (0, 0)),
        ],
        out_specs=pl.BlockSpec((tm, D_MODEL), lambda i: (i, 0)),
        out_shape=jax.ShapeDtypeStruct((T, D_MODEL), BF16),
        scratch_shapes=[pltpu.VMEM((POOL_HALO + tm, POOL_GW), F32)],
        compiler_params=_cparams(("parallel",)),
        name="pool",
    )(proj, proj, proj, w_pool, pool_scale)


def _split3(v):
    hi = v.astype(BF16)
    r1 = v - hi.astype(F32)
    mid = r1.astype(BF16)
    lo = (r1 - mid.astype(F32)).astype(BF16)
    return hi, mid, lo


def _ssd_kernel(z_ref, xs_ref, bc_ref, dt_ref, cwx_ref, cbx_ref, cwbc_ref, cbbc_ref,
                dtb_ref, alog_ref, dsk_ref, nw_ref, o_ref,
                tail_x, tail_bc, cbuf, xs_scr, b_scr, c_scr, a_scr, at_scr, dtt_scr, state):
    L = SSD_CHUNK
    c = pl.program_id(1)

    @pl.when(c == 0)
    def _():
        tail_x[...] = jnp.zeros_like(tail_x)
        tail_bc[...] = jnp.zeros_like(tail_bc)
        state[...] = jnp.zeros_like(state)

    def conv_slab(src_ref, tail_ref, cw_ref, cb_ref, col):
        cols = slice(col, col + SSD_GW)
        raw = src_ref[:, cols].astype(F32)
        cbuf[0:CONV_TAIL, :] = tail_ref[:, cols]
        cbuf[CONV_TAIL:, :] = raw
        acc = cb_ref[:, cols] + cw_ref[SSD_CONV - 1:SSD_CONV, cols] * raw
        for k in range(SSD_CONV - 1):
            off = CONV_TAIL - (SSD_CONV - 1) + k
            acc = acc + cw_ref[k:k + 1, cols] * cbuf[off:off + L, :]
        tail_ref[:, cols] = raw[L - CONV_TAIL:, :]
        return acc * jax.nn.sigmoid(acc)

    for s in range(SSD_INNER // SSD_GW):
        xs_scr[:, s * SSD_GW:(s + 1) * SSD_GW] = conv_slab(xs_ref, tail_x, cwx_ref, cbx_ref, s * SSD_GW)
    for s in range(SSD_BC // SSD_GW):
        b_scr[:, s * SSD_GW:(s + 1) * SSD_GW] = conv_slab(bc_ref, tail_bc, cwbc_ref, cbbc_ref, s * SSD_GW)
    for s in range(SSD_BC // SSD_GW):
        c_scr[:, s * SSD_GW:(s + 1) * SSD_GW] = conv_slab(
            bc_ref, tail_bc, cwbc_ref, cbbc_ref, SSD_BC + s * SSD_GW).astype(BF16)

    dtv = jax.nn.softplus(dt_ref[...] + dtb_ref[...])
    d_a = dtv * (-jnp.exp(alog_ref[...]))
    rr = lax.broadcasted_iota(jnp.int32, (L, L), 0)
    cc = lax.broadcasted_iota(jnp.int32, (L, L), 1)
    causal = rr >= cc
    tri = jnp.where(causal, 1.0, 0.0).astype(BF16)
    a_cum = sum(jnp.dot(tri, part, preferred_element_type=F32) for part in _split3(d_a))
    a_scr[...] = a_cum
    at_scr[...] = a_cum.T
    dtt_scr[...] = dtv.T

    def group_body(g, carry):
        off = pl.multiple_of(g * SSD_GW, SSD_GW)
        offn = pl.multiple_of(g * SSD_STATE, SSD_STATE)
        xg = xs_scr[:, pl.ds(off, SSD_GW)]
        xg_b = xg.astype(BF16)
        b_f = b_scr[:, pl.ds(offn, SSD_STATE)]
        c_b = c_scr[:, pl.ds(offn, SSD_STATE)]
        b_t = b_f.T
        cbm = lax.dot_general(c_b, b_f.astype(BF16), (((1,), (1,)), ((), ())),
                              preferred_element_type=F32)
        s_g = state[g]
        y_off = jnp.dot(c_b, s_g.astype(BF16), preferred_element_type=F32)
        a_g = pltpu.roll(a_scr[...], (LANES - SSD_HPG * g) % LANES, 1)
        ea_g = jnp.exp(a_g)
        ys, s_new = [], []
        for j in range(SSD_HPG):
            h = g * SSD_HPG + j
            hs = slice(j * SSD_HEADDIM, (j + 1) * SSD_HEADDIM)
            a_row = at_scr[pl.ds(h, 1), :]
            dt_row = dtt_scr[pl.ds(h, 1), :]
            seg = a_g[:, j:j + 1] - a_row
            decay = jnp.exp(jnp.where(causal, seg, NEG_BIG))
            m_h = (cbm * decay * dt_row).astype(BF16)
            x_h = xg_b[:, hs]
            y_h = jnp.dot(m_h, x_h, preferred_element_type=F32)
            ys.append(y_h + y_off[:, hs] * ea_g[:, j:j + 1])
            a_end = a_row[:, L - 1:L]
            w_row = dt_row * jnp.exp(a_end - a_row)
            bh_t = (b_t * w_row).astype(BF16)
            s_h = jnp.dot(bh_t, x_h, preferred_element_type=F32)
            s_new.append(s_g[:, hs] * jnp.exp(a_end) + s_h)
        state[g] = jnp.concatenate(s_new, axis=1)
        y = jnp.concatenate(ys, axis=1) + dsk_ref[:, pl.ds(off, SSD_GW)] * xg
        zg = z_ref[:, pl.ds(off, SSD_GW)].astype(F32)
        yg = y * (zg * jax.nn.sigmoid(zg))
        ms = jnp.mean(yg * yg, axis=-1, keepdims=True)
        yn = yg * lax.rsqrt(ms + EPS) * nw_ref[:, pl.ds(off, SSD_GW)]
        o_ref[:, pl.ds(off, SSD_GW)] = yn.astype(o_ref.dtype)
        return carry

    lax.fori_loop(0, SSD_GROUPS, group_body, 0)


def _ssd(proj, dt_raw, cwx, cbx, cwbc, cbbc, dtb, alog, dsk, nw, batch, seq):
    L = SSD_CHUNK
    nc = seq // L
    T = proj.shape[0]
    row = lambda b, c: b * nc + c
    const = lambda b, c: (0, 0)
    return pl.pallas_call(
        _ssd_kernel,
        grid=(batch, nc),
        in_specs=[
            pl.BlockSpec((L, SSD_INNER), lambda b, c: (row(b, c), COL_Z // SSD_INNER)),
            pl.BlockSpec((L, SSD_INNER), lambda b, c: (row(b, c), COL_XS // SSD_INNER)),
            pl.BlockSpec((L, 2 * SSD_BC), lambda b, c: (row(b, c), COL_BC // (2 * SSD_BC))),
            pl.BlockSpec((L, LANES), lambda b, c: (row(b, c), 0)),
            pl.BlockSpec((SSD_CONV, SSD_INNER), const),
            pl.BlockSpec((1, SSD_INNER), const),
            pl.BlockSpec((SSD_CONV, 2 * SSD_BC), const),
            pl.BlockSpec((1, 2 * SSD_BC), const),
            pl.BlockSpec((1, LANES), const),
            pl.BlockSpec((1, LANES), const),
            pl.BlockSpec((1, SSD_INNER), const),
            pl.BlockSpec((1, SSD_INNER), const),
        ],
        out_specs=pl.BlockSpec((L, SSD_INNER), lambda b, c: (row(b, c), 0)),
        out_shape=jax.ShapeDtypeStruct((T, SSD_INNER), BF16),
        scratch_shapes=[
            pltpu.VMEM((CONV_TAIL, SSD_INNER), F32),
            pltpu.VMEM((CONV_TAIL, 2 * SSD_BC), F32),
            pltpu.VMEM((CONV_TAIL + L, SSD_GW), F32),
            pltpu.VMEM((L, SSD_INNER), F32),
            pltpu.VMEM((L, SSD_BC), F32),
            pltpu.VMEM((L, SSD_BC), BF16),
            pltpu.VMEM((L, LANES), F32),
            pltpu.VMEM((LANES, L), F32),
            pltpu.VMEM((LANES, L), F32),
            pltpu.VMEM((SSD_GROUPS, SSD_STATE, SSD_GW), F32),
        ],
        compiler_params=_cparams(("arbitrary", "arbitrary")),
        name="ssd",
    )(proj, proj, proj, dt_raw, cwx, cbx, cwbc, cbbc, dtb, alog, dsk, nw)


def _mixout_kernel(pg_ref, yn_ref, gs_ref, x_ref, wso_ref, wo_ref, nw_ref, rwh_ref, rwl_ref, rb_ref,
                   h_ref, u_ref, lg_ref):
    ssd = jnp.dot(yn_ref[...], wso_ref[...], preferred_element_type=F32)
    merged = pg_ref[...].astype(F32) + jax.nn.sigmoid(gs_ref[...].astype(F32)) * ssd
    h = x_ref[...] + jnp.dot(merged.astype(BF16), wo_ref[...], preferred_element_type=F32)
    h_ref[...] = h
    ms = jnp.mean(h * h, axis=-1, keepdims=True)
    u = h * lax.rsqrt(ms + EPS) * nw_ref[...]
    u_hi = u.astype(BF16)
    u_ref[...] = u_hi
    u_lo = (u - u_hi.astype(F32)).astype(BF16)
    rwh = rwh_ref[...]
    lg = (jnp.dot(u_hi, rwh, preferred_element_type=F32)
          + jnp.dot(u_lo, rwh, preferred_element_type=F32)
          + jnp.dot(u_hi, rwl_ref[...], preferred_element_type=F32))
    lg_ref[...] = lg + rb_ref[...]


def _mixout(pool_g, yn, proj, x2, wso, wo, nw, rwh, rwl, rb, tm):
    T = x2.shape[0]
    const = lambda i: (0, 0)
    return pl.pallas_call(
        _mixout_kernel,
        grid=(T // tm,),
        in_specs=[
            pl.BlockSpec((tm, D_MODEL), lambda i: (i, 0)),
            pl.BlockSpec((tm, SSD_INNER), lambda i: (i, 0)),
            pl.BlockSpec((tm, D_MODEL), lambda i: (i, COL_GS // D_MODEL)),
            pl.BlockSpec((tm, D_MODEL), lambda i: (i, 0)),
            pl.BlockSpec((SSD_INNER, D_MODEL), const),
            pl.BlockSpec((D_MODEL, D_MODEL), const),
            pl.BlockSpec((1, D_MODEL), const),
            pl.BlockSpec((D_MODEL, LANES), const),
            pl.BlockSpec((D_MODEL, LANES), const),
            pl.BlockSpec((1, LANES), const),
        ],
        out_specs=[
            pl.BlockSpec((tm, D_MODEL), lambda i: (i, 0)),
            pl.BlockSpec((tm, D_MODEL), lambda i: (i, 0)),
            pl.BlockSpec((tm, LANES), lambda i: (i, 0)),
        ],
        out_shape=[
            jax.ShapeDtypeStruct((T, D_MODEL), F32),
            jax.ShapeDtypeStruct((T, D_MODEL), BF16),
            jax.ShapeDtypeStruct((T, LANES), F32),
        ],
        compiler_params=_cparams(("parallel",)),
        name="mixout",
    )(pool_g, yn, proj, x2, wso, wo, nw, rwh, rwl, rb)


def _ffn_kernel(te_ref, nu_ref, x_ref, wg_ref, wl_ref, bg_ref, bl_ref, wd_ref, bd_ref, gate_ref,
                o_ref, acc, *, nf):
    i = pl.program_id(0)
    f = pl.program_id(1)

    @pl.when(i < nu_ref[0])
    def _():
        x = x_ref[...]
        hg = jnp.dot(x, wg_ref[0].astype(BF16), preferred_element_type=F32) + bg_ref[0]
        hl = jnp.dot(x, wl_ref[0].astype(BF16), preferred_element_type=F32) + bl_ref[0]
        glu = jnp.minimum(hg, SWIGLU_LIMIT)
        lin = jnp.clip(hl, -SWIGLU_LIMIT, SWIGLU_LIMIT)
        act = glu * jax.nn.sigmoid(SWIGLU_ALPHA * glu) * (lin + 1.0)
        contrib = jnp.dot(act.astype(BF16), wd_ref[0].astype(BF16), preferred_element_type=F32)

        @pl.when(f == 0)
        def _():
            acc[...] = contrib + bd_ref[0]

        @pl.when(f > 0)
        def _():
            acc[...] += contrib

        @pl.when(f == nf - 1)
        def _():
            o_ref[...] = (acc[...] * gate_ref[...]).astype(o_ref.dtype)


def _ffn(tile_expert, n_used, xb, w_up, b_up3, w_down, b_down3, slot_gate, tm, fc):
    n_slots = xb.shape[0]
    n_tiles = n_slots // tm
    nf = D_FF // fc

    def live(i, nu):
        return jnp.minimum(i, nu[0] - 1)

    def fidx(i, f, nu):
        return jnp.where(i < nu[0], f, nf - 1)

    grid_spec = pltpu.PrefetchScalarGridSpec(
        num_scalar_prefetch=2,
        grid=(n_tiles, nf),
        in_specs=[
            pl.BlockSpec((tm, D_MODEL), lambda i, f, te, nu: (live(i, nu), 0)),
            pl.BlockSpec((1, D_MODEL, fc), lambda i, f, te, nu: (te[i], 0, fidx(i, f, nu))),
            pl.BlockSpec((1, D_MODEL, fc), lambda i, f, te, nu: (te[i], 0, nf + fidx(i, f, nu))),
            pl.BlockSpec((1, 1, fc), lambda i, f, te, nu: (te[i], 0, fidx(i, f, nu))),
            pl.BlockSpec((1, 1, fc), lambda i, f, te, nu: (te[i], 0, nf + fidx(i, f, nu))),
            pl.BlockSpec((1, fc, D_MODEL), lambda i, f, te, nu: (te[i], fidx(i, f, nu), 0)),
            pl.BlockSpec((1, 1, D_MODEL), lambda i, f, te, nu: (te[i], 0, 0)),
            pl.BlockSpec((tm, 1), lambda i, f, te, nu: (live(i, nu), 0)),
        ],
        out_specs=pl.BlockSpec((tm, D_MODEL), lambda i, f, te, nu: (live(i, nu), 0)),
        scratch_shapes=[pltpu.VMEM((tm, D_MODEL), F32)],
    )
    return pl.pallas_call(
        functools.partial(_ffn_kernel, nf=nf),
        grid_spec=grid_spec,
        out_shape=jax.ShapeDtypeStruct((n_slots, D_MODEL), BF16),
        compiler_params=_cparams(("arbitrary", "arbitrary")),
        name="ffn",
    )(tile_expert, n_used, xb, w_up, w_up, b_up3, b_up3, w_down, b_down3, slot_gate)


def _final_kernel(h_ref, y_ref, nw_ref, o_ref):
    h = h_ref[...]
    for k in range(TOP_K):
        h = h + y_ref[:, k, :].astype(F32)
    ms = jnp.mean(h * h, axis=-1, keepdims=True)
    o_ref[...] = h * lax.rsqrt(ms + EPS) * nw_ref[...]


def _final(h, yk, nw, tm):
    T = h.shape[0]
    return pl.pallas_call(
        _final_kernel,
        grid=(T // tm,),
        in_specs=[
            pl.BlockSpec((tm, D_MODEL), lambda i: (i, 0)),
            pl.BlockSpec((tm, TOP_K, D_MODEL), lambda i: (i, 0, 0)),
            pl.BlockSpec((1, D_MODEL), lambda i: (0, 0)),
        ],
        out_specs=pl.BlockSpec((tm, D_MODEL), lambda i: (i, 0)),
        out_shape=jax.ShapeDtypeStruct((T, D_MODEL), F32),
        compiler_params=_cparams(("parallel",)),
        name="final",
    )(h, yk, nw)


def _row(v, width=None):
    v = v.astype(F32).reshape(1, -1)
    if width is not None and v.shape[1] < width:
        v = jnp.pad(v, ((0, 0), (0, width - v.shape[1])))
    return v


def _tile(n, pref):
    t = pref
    while n % t:
        t //= 2
    return t


def kernel(x, norm_mix_w, w_in, w_pool, pool_scale, conv_w, conv_b, dt_bias, a_log, d_skip,
           ssd_norm_w, w_ssd_out, w_out, norm_ffn_w, router_w, router_b, w_up, b_up,
           w_down, b_down, norm_final_w):
    B, S, D = x.shape
    T = B * S
    x2 = x.reshape(T, D)

    o_p, o_z, o_xbc = 0, D_MODEL, D_MODEL + SSD_INNER
    o_dt = o_xbc + SSD_INNER + 2 * SSD_BC
    o_gp = o_dt + SSD_HEADS
    o_gs = o_gp + D_MODEL
    w_main = jnp.concatenate(
        [w_in[:, o_z:o_z + SSD_INNER], w_in[:, o_xbc:o_dt], w_in[:, o_p:o_p + D_MODEL],
         w_in[:, o_gp:o_gp + D_MODEL], w_in[:, o_gs:o_gs + D_MODEL]], axis=1).astype(BF16)
    w_dt = jnp.pad(w_in[:, o_dt:o_dt + SSD_HEADS], ((0, 0), (0, LANES - SSD_HEADS))).astype(BF16)

    proj, dt_raw = _inproj(x2, _row(norm_mix_w), w_main, w_dt, _tile(T, 1024), 1024)

    pool_g = _pool(proj, w_pool.astype(BF16), _row(pool_scale), _tile(S, 256), S)

    yn = _ssd(proj, dt_raw,
              conv_w[:, :SSD_INNER].astype(F32), _row(conv_b[:SSD_INNER]),
              conv_w[:, SSD_INNER:].astype(F32), _row(conv_b[SSD_INNER:]),
              _row(dt_bias, LANES), _row(a_log, LANES),
              _row(jnp.repeat(d_skip, SSD_HEADDIM)), _row(ssd_norm_w), B, S)

    rw = jnp.pad(router_w.astype(F32), ((0, 0), (0, LANES - N_EXPERTS)))
    rw_hi = rw.astype(BF16)
    rw_lo = (rw - rw_hi.astype(F32)).astype(BF16)
    h, u, logits = _mixout(pool_g, yn, proj, x2, w_ssd_out.astype(BF16), w_out.astype(BF16),
                           _row(norm_ffn_w), rw_hi, rw_lo, _row(router_b, LANES), _tile(T, 256))

    tm_ffn = 1024 if T * TOP_K >= 8192 else 128
    TK = T * TOP_K
    top_logits, top_idx = lax.top_k(logits[:, :N_EXPERTS], TOP_K)
    gates = jax.nn.softmax(top_logits, axis=-1)
    flat_e = top_idx.reshape(TK).astype(jnp.int32)
    order = jnp.argsort(flat_e, stable=True)
    sorted_e = flat_e[order]
    counts = jnp.bincount(flat_e, length=N_EXPERTS)
    padded = (counts + tm_ffn - 1) // tm_ffn * tm_ffn
    start = jnp.cumsum(counts) - counts
    pend = jnp.cumsum(padded)
    pstart = pend - padded
    dest_sorted = pstart[sorted_e] + jnp.arange(TK, dtype=jnp.int32) - start[sorted_e]
    dest = jnp.zeros((TK,), jnp.int32).at[order].set(dest_sorted.astype(jnp.int32))
    n_tiles = -(-TK // tm_ffn) + N_EXPERTS
    n_slots = n_tiles * tm_ffn
    tok = jnp.arange(TK, dtype=jnp.int32) // TOP_K
    slot_tok = jnp.full((n_slots,), T, jnp.int32).at[dest].set(tok)
    slot_gate = jnp.zeros((n_slots,), F32).at[dest].set(gates.reshape(TK)).reshape(n_slots, 1)
    u_pad = jnp.concatenate([u, jnp.zeros((1, D), u.dtype)], axis=0)
    xb = u_pad[slot_tok]
    tile_start = jnp.arange(n_tiles, dtype=jnp.int32) * tm_ffn
    tile_expert = jnp.minimum(jnp.searchsorted(pend, tile_start, side='right'),
                              N_EXPERTS - 1).astype(jnp.int32)
    n_used = (pend[-1] // tm_ffn).astype(jnp.int32).reshape(1)

    yb = _ffn(tile_expert, n_used, xb, w_up, b_up.reshape(N_EXPERTS, 1, 2 * D_FF),
              w_down, b_down.reshape(N_EXPERTS, 1, D_MODEL), slot_gate, tm_ffn, 256)
    yk = yb[dest].reshape(T, TOP_K, D)

    out = _final(h, yk, _row(norm_final_w), _tile(T, 512))
    return out.reshape(B, S, D)
```

```python
import functools

import jax
import jax.numpy as jnp
from jax import lax
from jax.experimental import pallas as pl
from jax.experimental.pallas import tpu as pltpu

F32 = jnp.float32
BF16 = jnp.bfloat16

D_MODEL = 2048
EPS = 1e-5
POOL_GROUPS = 4
POOL_WINDOWS = (2, 4, 8, 16)
POOL_GW = D_MODEL // POOL_GROUPS
POOL_HALO = 16
SSD_INNER = 2 * D_MODEL
SSD_HEADDIM = 64
SSD_HEADS = SSD_INNER // SSD_HEADDIM
SSD_GROUPS = 8
SSD_HPG = SSD_HEADS // SSD_GROUPS
SSD_STATE = 128
SSD_CONV = 4
SSD_CHUNK = 128
SSD_GW = SSD_HPG * SSD_HEADDIM
SSD_BC = SSD_GROUPS * SSD_STATE
N_EXPERTS = 32
TOP_K = 4
D_FF = D_MODEL
SWIGLU_ALPHA = 1.702
SWIGLU_LIMIT = 7.0

LANES = 128
HALF = D_MODEL // 2
CONV_TAIL = 8
VMEM_LIMIT = 56 * 1024 * 1024

COL_Z = 0
COL_XS = SSD_INNER
COL_BC = 2 * SSD_INNER
COL_P = 2 * SSD_INNER + 2 * SSD_BC
COL_GP = COL_P + D_MODEL
COL_GS = COL_GP + D_MODEL
PROJ_COLS = COL_GS + D_MODEL

NEG_BIG = -1e30


def _cparams(sem):
    return pltpu.CompilerParams(dimension_semantics=sem, vmem_limit_bytes=VMEM_LIMIT)


def _inproj_kernel(x_ref, nw_ref, w_ref, wdt_ref, o_ref, dt_ref, u_scr):
    @pl.when(pl.program_id(1) == 0)
    def _():
        x = x_ref[...]
        ms = jnp.mean(x * x, axis=-1, keepdims=True)
        u = (x * lax.rsqrt(ms + EPS) * nw_ref[...]).astype(BF16)
        u_scr[...] = u
        dt_ref[...] = jnp.dot(u, wdt_ref[...], preferred_element_type=F32)

    o_ref[...] = jnp.dot(u_scr[...], w_ref[...], preferred_element_type=F32).astype(o_ref.dtype)


def _inproj(x2, norm_w, w_main, w_dt, tm, tn):
    T = x2.shape[0]
    return pl.pallas_call(
        _inproj_kernel,
        grid=(T // tm, PROJ_COLS // tn),
        in_specs=[
            pl.BlockSpec((tm, D_MODEL), lambda i, j: (i, 0)),
            pl.BlockSpec((1, D_MODEL), lambda i, j: (0, 0)),
            pl.BlockSpec((D_MODEL, tn), lambda i, j: (0, j)),
            pl.BlockSpec((D_MODEL, LANES), lambda i, j: (0, 0)),
        ],
        out_specs=[
            pl.BlockSpec((tm, tn), lambda i, j: (i, j)),
            pl.BlockSpec((tm, LANES), lambda i, j: (i, 0)),
        ],
        out_shape=[
            jax.ShapeDtypeStruct((T, PROJ_COLS), BF16),
            jax.ShapeDtypeStruct((T, LANES), F32),
        ],
        scratch_shapes=[pltpu.VMEM((tm, D_MODEL), BF16)],
        compiler_params=_cparams(("parallel", "arbitrary")),
        name="inproj",
    )(x2, norm_w, w_main, w_dt)


def _pool_kernel(halo_ref, p_ref, g_ref, wp_ref, ps_ref, o_ref, buf, *, tm, seq):
    row0 = (pl.program_id(0) * tm) % seq
    halo = jnp.where(row0 == 0, 0.0, halo_ref[...].astype(F32))
    pos = row0 + lax.broadcasted_iota(jnp.int32, (tm, 1), 0)
    for g, w in enumerate(POOL_WINDOWS):
        cols = slice(g * POOL_GW, (g + 1) * POOL_GW)
        buf[0:POOL_HALO, :] = halo[:, cols]
        buf[POOL_HALO:, :] = p_ref[:, cols].astype(F32)
        cur = buf[POOL_HALO:POOL_HALO + tm, :]
        acc = cur
        for k in range(1, w):
            acc = acc + buf[POOL_HALO - k:POOL_HALO - k + tm, :]
        cnt = jnp.minimum(pos + 1, w).astype(F32)
        pooled = acc / cnt - cur
        y = jnp.dot(pooled.astype(BF16), wp_ref[g], preferred_element_type=F32)
        y = y * ps_ref[:, cols] * jax.nn.sigmoid(g_ref[:, cols].astype(F32))
        o_ref[:, cols] = y.astype(o_ref.dtype)


def _pool(proj, w_pool, pool_scale, tm, seq):
    T = proj.shape[0]
    hb = tm // POOL_HALO
    return pl.pallas_call(
        functools.partial(_pool_kernel, tm=tm, seq=seq),
        grid=(T // tm,),
        in_specs=[
            pl.BlockSpec((POOL_HALO, D_MODEL),
                         lambda i: (jnp.maximum(i * hb - 1, 0), COL_P // D_MODEL)),
            pl.BlockSpec((tm, D_MODEL), lambda i: (i, COL_P // D_MODEL)),
            pl.BlockSpec((tm, D_MODEL), lambda i: (i, COL_GP // D_MODEL)),
            pl.BlockSpec((POOL_GROUPS, POOL_GW, POOL_GW), lambda i: (0, 0, 0)),
            pl.BlockSpec((1, D_MODEL), lambda i: (0, 0)),
        ],
        out_specs=pl.BlockSpec((tm, D_MODEL), lambda i: (i, 0)),
        out_shape=jax.ShapeDtypeStruct((T, D_MODEL), BF16),
        scratch_shapes=[pltpu.VMEM((POOL_HALO + tm, POOL_GW), F32)],
        compiler_params=_cparams(("parallel",)),
        name="pool",
    )(proj, proj, proj, w_pool, pool_scale)


def _split3(v):
    hi = v.astype(BF16)
    r1 = v - hi.astype(F32)
    mid = r1.astype(BF16)
    lo = (r1 - mid.astype(F32)).astype(BF16)
    return hi, mid, lo


def _ssd_kernel(z_ref, xs_ref, bc_ref, dt_ref, cwx_ref, cbx_ref, cwbc_ref, cbbc_ref,
                dtb_ref, alog_ref, dsk_ref, nw_ref, o_ref,
                tail_x, tail_bc, cbuf, xs_scr, b_scr, c_scr, a_scr, at_scr, dtt_scr, state):
    L = SSD_CHUNK
    c = pl.program_id(1)

    @pl.when(c == 0)
    def _():
        tail_x[...] = jnp.zeros_like(tail_x)
        tail_bc[...] = jnp.zeros_like(tail_bc)
        state[...] = jnp.zeros_like(state)

    def conv_slab(src_ref, tail_ref, cw_ref, cb_ref, col):
        cols = slice(col, col + SSD_GW)
        raw = src_ref[:, cols].astype(F32)
        cbuf[0:CONV_TAIL, :] = tail_ref[:, cols]
        cbuf[CONV_TAIL:, :] = raw
        acc = cb_ref[:, cols] + cw_ref[SSD_CONV - 1:SSD_CONV, cols] * raw
        for k in range(SSD_CONV - 1):
            off = CONV_TAIL - (SSD_CONV - 1) + k
            acc = acc + cw_ref[k:k + 1, cols] * cbuf[off:off + L, :]
        tail_ref[:, cols] = raw[L - CONV_TAIL:, :]
        return acc * jax.nn.sigmoid(acc)

    for s in range(SSD_INNER // SSD_GW):
        xs_scr[:, s * SSD_GW:(s + 1) * SSD_GW] = conv_slab(xs_ref, tail_x, cwx_ref, cbx_ref, s * SSD_GW)
    for s in range(SSD_BC // SSD_GW):
        b_scr[:, s * SSD_GW:(s + 1) * SSD_GW] = conv_slab(bc_ref, tail_bc, cwbc_ref, cbbc_ref, s * SSD_GW)
    for s in range(SSD_BC // SSD_GW):
        c_scr[:, s * SSD_GW:(s + 1) * SSD_GW] = conv_slab(
            bc_ref, tail_bc, cwbc_ref, cbbc_ref, SSD_BC + s * SSD_GW).astype(BF16)

    dtv = jax.nn.softplus(dt_ref[...] + dtb_ref[...])
    d_a = dtv * (-jnp.exp(alog_ref[...]))
    rr = lax.broadcasted_iota(jnp.int32, (L, L), 0)
    cc = lax.broadcasted_iota(jnp.int32, (L, L), 1)
    causal = rr >= cc
    tri = jnp.where(causal, 1.0, 0.0).astype(BF16)
    a_cum = sum(jnp.dot(tri, part, preferred_element_type=F32) for part in _split3(d_a))
    a_scr[...] = a_cum
    at_scr[...] = a_cum.T
    dtt_scr[...] = dtv.T

    def group_body(g, carry):
        off = pl.multiple_of(g * SSD_GW, SSD_GW)
        offn = pl.multiple_of(g * SSD_STATE, SSD_STATE)
        xg = xs_scr[:, pl.ds(off, SSD_GW)]
        xg_b = xg.astype(BF16)
        b_f = b_scr[:, pl.ds(offn, SSD_STATE)]
        c_b = c_scr[:, pl.ds(offn, SSD_STATE)]
        b_t = b_f.T
        cbm = lax.dot_general(c_b, b_f.astype(BF16), (((1,), (1,)), ((), ())),
                              preferred_element_type=F32)
        s_g = state[g]
        y_off = jnp.dot(c_b, s_g.astype(BF16), preferred_element_type=F32)
        a_g = pltpu.roll(a_scr[...], (LANES - SSD_HPG * g) % LANES, 1)
        ea_g = jnp.exp(a_g)
        ys, s_new = [], []
        for j in range(SSD_HPG):
            h = g * SSD_HPG + j
            hs = slice(j * SSD_HEADDIM, (j + 1) * SSD_HEADDIM)
            a_row = at_scr[pl.ds(h, 1), :]
            dt_row = dtt_scr[pl.ds(h, 1), :]
            seg = a_g[:, j:j + 1] - a_row
            decay = jnp.exp(jnp.where(causal, seg, NEG_BIG))
            m_h = (cbm * decay * dt_row).astype(BF16)
            x_h = xg_b[:, hs]
            y_h = jnp.dot(m_h, x_h, preferred_element_type=F32)
            ys.append(y_h + y_off[:, hs] * ea_g[:, j:j + 1])
            a_end = a_row[:, L - 1:L]
            w_row = dt_row * jnp.exp(a_end - a_row)
            bh_t = (b_t * w_row).astype(BF16)
            s_h = jnp.dot(bh_t, x_h, preferred_element_type=F32)
            s_new.append(s_g[:, hs] * jnp.exp(a_end) + s_h)
        state[g] = jnp.concatenate(s_new, axis=1)
        y = jnp.concatenate(ys, axis=1) + dsk_ref[:, pl.ds(off, SSD_GW)] * xg
        zg = z_ref[:, pl.ds(off, SSD_GW)].astype(F32)
        yg = y * (zg * jax.nn.sigmoid(zg))
        ms = jnp.mean(yg * yg, axis=-1, keepdims=True)
        yn = yg * lax.rsqrt(ms + EPS) * nw_ref[:, pl.ds(off, SSD_GW)]
        o_ref[:, pl.ds(off, SSD_GW)] = yn.astype(o_ref.dtype)
        return carry

    lax.fori_loop(0, SSD_GROUPS, group_body, 0)


def _ssd(proj, dt_raw, cwx, cbx, cwbc, cbbc, dtb, alog, dsk, nw, batch, seq):
    L = SSD_CHUNK
    nc = seq // L
    T = proj.shape[0]
    row = lambda b, c: b * nc + c
    const = lambda b, c: (0, 0)
    return pl.pallas_call(
        _ssd_kernel,
        grid=(batch, nc),
        in_specs=[
            pl.BlockSpec((L, SSD_INNER), lambda b, c: (row(b, c), COL_Z // SSD_INNER)),
            pl.BlockSpec((L, SSD_INNER), lambda b, c: (row(b, c), COL_XS // SSD_INNER)),
            pl.BlockSpec((L, 2 * SSD_BC), lambda b, c: (row(b, c), COL_BC // (2 * SSD_BC))),
            pl.BlockSpec((L, LANES), lambda b, c: (row(b, c), 0)),
            pl.BlockSpec((SSD_CONV, SSD_INNER), const),
            pl.BlockSpec((1, SSD_INNER), const),
            pl.BlockSpec((SSD_CONV, 2 * SSD_BC), const),
            pl.BlockSpec((1, 2 * SSD_BC), const),
            pl.BlockSpec((1, LANES), const),
            pl.BlockSpec((1, LANES), const),
            pl.BlockSpec((1, SSD_INNER), const),
            pl.BlockSpec((1, SSD_INNER), const),
        ],
        out_specs=pl.BlockSpec((L, SSD_INNER), lambda b, c: (row(b, c), 0)),
        out_shape=jax.ShapeDtypeStruct((T, SSD_INNER), BF16),
        scratch_shapes=[
            pltpu.VMEM((CONV_TAIL, SSD_INNER), F32),
            pltpu.VMEM((CONV_TAIL, 2 * SSD_BC), F32),
            pltpu.VMEM((CONV_TAIL + L, SSD_GW), F32),
            pltpu.VMEM((L, SSD_INNER), F32),
            pltpu.VMEM((L, SSD_BC), F32),
            pltpu.VMEM((L, SSD_BC), BF16),
            pltpu.VMEM((L, LANES), F32),
            pltpu.VMEM((LANES, L), F32),
            pltpu.VMEM((LANES, L), F32),
            pltpu.VMEM((SSD_GROUPS, SSD_STATE, SSD_GW), F32),
        ],
        compiler_params=_cparams(("arbitrary", "arbitrary")),
        name="ssd",
    )(proj, proj, proj, dt_raw, cwx, cbx, cwbc, cbbc, dtb, alog, dsk, nw)


def _pack_bf16_pairs(v):
    bits = pltpu.bitcast(v.astype(BF16).astype(F32), jnp.uint32)
    return (bits[:, :HALF] >> 16) | (bits[:, HALF:] & jnp.uint32(0xFFFF0000))


def _unpack_lo(w):
    return pltpu.bitcast(w << 16, F32)


def _unpack_hi(w):
    return pltpu.bitcast(w & jnp.uint32(0xFFFF0000), F32)


def _mixout_kernel(pg_ref, yn_ref, gs_ref, x_ref, wso_ref, wo_ref, nw_ref, rwh_ref, rwl_ref, rb_ref,
                   h_ref, u_ref, lg_ref):
    ssd = jnp.dot(yn_ref[...], wso_ref[...], preferred_element_type=F32)
    merged = pg_ref[...].astype(F32) + jax.nn.sigmoid(gs_ref[...].astype(F32)) * ssd
    h = x_ref[...] + jnp.dot(merged.astype(BF16), wo_ref[...], preferred_element_type=F32)
    h_ref[...] = h
    ms = jnp.mean(h * h, axis=-1, keepdims=True)
    u = h * lax.rsqrt(ms + EPS) * nw_ref[...]
    u_hi = u.astype(BF16)
    u_ref[...] = _pack_bf16_pairs(u)
    u_lo = (u - u_hi.astype(F32)).astype(BF16)
    rwh = rwh_ref[...]
    lg = (jnp.dot(u_hi, rwh, preferred_element_type=F32)
          + jnp.dot(u_lo, rwh, preferred_element_type=F32)
          + jnp.dot(u_hi, rwl_ref[...], preferred_element_type=F32))
    lg_ref[...] = lg + rb_ref[...]


def _mixout(pool_g, yn, proj, x2, wso, wo, nw, rwh, rwl, rb, tm):
    T = x2.shape[0]
    const = lambda i: (0, 0)
    return pl.pallas_call(
        _mixout_kernel,
        grid=(T // tm,),
        in_specs=[
            pl.BlockSpec((tm, D_MODEL), lambda i: (i, 0)),
            pl.BlockSpec((tm, SSD_INNER), lambda i: (i, 0)),
            pl.BlockSpec((tm, D_MODEL), lambda i: (i, COL_GS // D_MODEL)),
            pl.BlockSpec((tm, D_MODEL), lambda i: (i, 0)),
            pl.BlockSpec((SSD_INNER, D_MODEL), const),
            pl.BlockSpec((D_MODEL, D_MODEL), const),
            pl.BlockSpec((1, D_MODEL), const),
            pl.BlockSpec((D_MODEL, LANES), const),
            pl.BlockSpec((D_MODEL, LANES), const),
            pl.BlockSpec((1, LANES), const),
        ],
        out_specs=[
            pl.BlockSpec((tm, D_MODEL), lambda i: (i, 0)),
            pl.BlockSpec((tm, HALF), lambda i: (i, 0)),
            pl.BlockSpec((tm, LANES), lambda i: (i, 0)),
        ],
        out_shape=[
            jax.ShapeDtypeStruct((T, D_MODEL), F32),
            jax.ShapeDtypeStruct((T, HALF), jnp.uint32),
            jax.ShapeDtypeStruct((T, LANES), F32),
        ],
        compiler_params=_cparams(("parallel",)),
        name="mixout",
    )(pool_g, yn, proj, x2, wso, wo, nw, rwh, rwl, rb)


def _router_kernel(lg_ref, er_ref, gt_ref, cnt_ref, run_scr, *, tm):
    @pl.when(pl.program_id(0) == 0)
    def _():
        run_scr[...] = jnp.zeros_like(run_scr)

    lane = lax.broadcasted_iota(jnp.int32, (tm, LANES), 1)
    lg = jnp.where(lane < N_EXPERTS, lg_ref[...], -jnp.inf)
    idxs, vals, hots = [], [], []
    for _ in range(TOP_K):
        m = jnp.max(lg, axis=-1, keepdims=True)
        idx = jnp.min(jnp.where(lg == m, lane, LANES), axis=-1, keepdims=True)
        hot = lane == idx
        lg = jnp.where(hot, -jnp.inf, lg)
        idxs.append(idx)
        vals.append(m)
        hots.append(hot)
    exps = [jnp.exp(v - vals[0]) for v in vals]
    den = exps[0] + exps[1] + exps[2] + exps[3]
    cnt = sum(jnp.where(hot, 1.0, 0.0) for hot in hots)
    rr = lax.broadcasted_iota(jnp.int32, (tm, tm), 0)
    cc = lax.broadcasted_iota(jnp.int32, (tm, tm), 1)
    before = jnp.where(rr > cc, 1.0, 0.0).astype(BF16)
    base = jnp.dot(before, cnt.astype(BF16), preferred_element_type=F32) + run_scr[...]
    er = jnp.zeros((tm, LANES), jnp.int32)
    gt = jnp.zeros((tm, LANES), F32)
    for k in range(TOP_K):
        rank = jnp.sum(jnp.where(hots[k], base, 0.0), axis=-1, keepdims=True)
        er = jnp.where(lane == k, idxs[k], er)
        er = jnp.where(lane == TOP_K + k, rank.astype(jnp.int32), er)
        gt = jnp.where(lane == k, exps[k] / den, gt)
    er_ref[...] = er
    gt_ref[...] = gt
    run_scr[...] += jnp.sum(cnt, axis=0, keepdims=True)
    cnt_ref[...] = run_scr[...]


def _router(logits, tm):
    T = logits.shape[0]
    return pl.pallas_call(
        functools.partial(_router_kernel, tm=tm),
        grid=(T // tm,),
        in_specs=[pl.BlockSpec((tm, LANES), lambda i: (i, 0))],
        out_specs=[
            pl.BlockSpec((tm, LANES), lambda i: (i, 0)),
            pl.BlockSpec((tm, LANES), lambda i: (i, 0)),
            pl.BlockSpec((1, LANES), lambda i: (0, 0)),
        ],
        out_shape=[
            jax.ShapeDtypeStruct((T, LANES), jnp.int32),
            jax.ShapeDtypeStruct((T, LANES), F32),
            jax.ShapeDtypeStruct((1, LANES), F32),
        ],
        scratch_shapes=[pltpu.VMEM((1, LANES), F32)],
        compiler_params=_cparams(("arbitrary",)),
        name="router",
    )(logits)


def _dispatch_kernel(ps_ref, e_ref, r_ref, u_ref, xb_ref, dest_ref, sem, *, tm):
    def body(r, carry):
        for k in range(TOP_K):
            j = r * TOP_K + k
            d = ps_ref[e_ref[j]] + r_ref[j]
            dest_ref[j] = d
            pltpu.make_async_copy(u_ref.at[pl.ds(r, 1)], xb_ref.at[pl.ds(d, 1)], sem).start()
        return carry

    lax.fori_loop(0, tm, body, 0)
    for _ in range(TOP_K):
        pltpu.make_async_copy(u_ref, xb_ref.at[pl.ds(0, tm)], sem).wait()


def _dispatch(pstart, e_flat, r_flat, u_pk, n_slots, tm):
    T = u_pk.shape[0]
    grid_spec = pltpu.PrefetchScalarGridSpec(
        num_scalar_prefetch=1,
        grid=(T // tm,),
        in_specs=[
            pl.BlockSpec((tm * TOP_K,), lambda i, ps: (i,), memory_space=pltpu.SMEM),
            pl.BlockSpec((tm * TOP_K,), lambda i, ps: (i,), memory_space=pltpu.SMEM),
            pl.BlockSpec((tm, HALF), lambda i, ps: (i, 0)),
        ],
        out_specs=[
            pl.BlockSpec(memory_space=pl.ANY),
            pl.BlockSpec((tm * TOP_K,), lambda i, ps: (i,), memory_space=pltpu.SMEM),
        ],
        scratch_shapes=[pltpu.SemaphoreType.DMA(())],
    )
    return pl.pallas_call(
        functools.partial(_dispatch_kernel, tm=tm),
        grid_spec=grid_spec,
        out_shape=[
            jax.ShapeDtypeStruct((n_slots, HALF), jnp.uint32),
            jax.ShapeDtypeStruct((T * TOP_K,), jnp.int32),
        ],
        compiler_params=_cparams(("arbitrary",)),
        name="dispatch",
    )(pstart, e_flat, r_flat, u_pk)


def _row_blocks(nrows, tm, sub, block):
    @pl.when(nrows == tm)
    def _():
        block(0, tm)

    @pl.when(nrows < tm)
    def _():
        for sb in range(tm // sub):
            @pl.when(sb * sub < nrows)
            def _():
                block(sb * sub, sub)


def _ffn_up_kernel(te_ref, nu_ref, tr_ref, x_ref, wg_ref, wl_ref, bg_ref, bl_ref, o_ref, xs, *, tm, sub):
    i = pl.program_id(0)
    nrows = tr_ref[i]

    @pl.when(jnp.logical_and(nrows > 0, pl.program_id(1) == 0))
    def _():
        w = x_ref[...]
        xs[:, :HALF] = _unpack_lo(w).astype(BF16)
        xs[:, HALF:] = _unpack_hi(w).astype(BF16)

    wg = wg_ref[0].astype(BF16)
    wl = wl_ref[0].astype(BF16)
    bg = bg_ref[0]
    bl = bl_ref[0]

    def block(r0, nr):
        x = xs[r0:r0 + nr, :]
        glu = jnp.minimum(jnp.dot(x, wg, preferred_element_type=F32) + bg, SWIGLU_LIMIT)
        lin = jnp.clip(jnp.dot(x, wl, preferred_element_type=F32) + bl, -SWIGLU_LIMIT, SWIGLU_LIMIT)
        act = glu * jax.nn.sigmoid(SWIGLU_ALPHA * glu) * (lin + 1.0)
        o_ref[r0:r0 + nr, :] = act.astype(o_ref.dtype)

    _row_blocks(nrows, tm, sub, block)


def _ffn_down_kernel(te_ref, nu_ref, tr_ref, a_ref, wa_ref, wb_ref, ba_ref, bb_ref, o_ref, *, tm, sub):
    nrows = tr_ref[pl.program_id(0)]
    wa = wa_ref[0].astype(BF16)
    wb = wb_ref[0].astype(BF16)
    ba = ba_ref[0]
    bb = bb_ref[0]

    def block(r0, nr):
        a = a_ref[r0:r0 + nr, :]
        ya = jnp.dot(a, wa, preferred_element_type=F32) + ba
        yb = jnp.dot(a, wb, preferred_element_type=F32) + bb
        lo = pltpu.bitcast(ya.astype(BF16).astype(F32), jnp.uint32) >> 16
        hi = pltpu.bitcast(yb.astype(BF16).astype(F32), jnp.uint32) & jnp.uint32(0xFFFF0000)
        o_ref[r0:r0 + nr, :] = lo | hi

    _row_blocks(nrows, tm, sub, block)


def _ffn(tile_expert, n_used, tile_rows, xb, w_up, b_up3, w_down, b_down3, tm, sub, fc, nc):
    n_slots = xb.shape[0]
    n_tiles = n_slots // tm
    nf = D_FF // fc
    nj = HALF // nc

    def live(i, nu):
        return jnp.minimum(i, nu[0] - 1)

    def frozen(i, j, nu, last):
        return jnp.where(i < nu[0], j, last)

    up_spec = pltpu.PrefetchScalarGridSpec(
        num_scalar_prefetch=3,
        grid=(n_tiles, nf),
        in_specs=[
            pl.BlockSpec((tm, HALF), lambda i, f, te, nu, tr: (live(i, nu), 0)),
            pl.BlockSpec((1, D_MODEL, fc), lambda i, f, te, nu, tr: (te[i], 0, frozen(i, f, nu, nf - 1))),
            pl.BlockSpec((1, D_MODEL, fc), lambda i, f, te, nu, tr: (te[i], 0, nf + frozen(i, f, nu, nf - 1))),
            pl.BlockSpec((1, 1, fc), lambda i, f, te, nu, tr: (te[i], 0, frozen(i, f, nu, nf - 1))),
            pl.BlockSpec((1, 1, fc), lambda i, f, te, nu, tr: (te[i], 0, nf + frozen(i, f, nu, nf - 1))),
        ],
        out_specs=pl.BlockSpec((tm, fc), lambda i, f, te, nu, tr: (live(i, nu), frozen(i, f, nu, nf - 1))),
        scratch_shapes=[pltpu.VMEM((tm, D_MODEL), BF16)],
    )
    act = pl.pallas_call(
        functools.partial(_ffn_up_kernel, tm=tm, sub=sub),
        grid_spec=up_spec,
        out_shape=jax.ShapeDtypeStruct((n_slots, D_FF), BF16),
        compiler_params=_cparams(("arbitrary", "arbitrary")),
        name="ffn_up",
    )(tile_expert, n_used, tile_rows, xb, w_up, w_up, b_up3, b_up3)

    down_spec = pltpu.PrefetchScalarGridSpec(
        num_scalar_prefetch=3,
        grid=(n_tiles, nj),
        in_specs=[
            pl.BlockSpec((tm, D_FF), lambda i, j, te, nu, tr: (live(i, nu), 0)),
            pl.BlockSpec((1, D_FF, nc), lambda i, j, te, nu, tr: (te[i], 0, frozen(i, j, nu, nj - 1))),
            pl.BlockSpec((1, D_FF, nc), lambda i, j, te, nu, tr: (te[i], 0, nj + frozen(i, j, nu, nj - 1))),
            pl.BlockSpec((1, 1, nc), lambda i, j, te, nu, tr: (te[i], 0, frozen(i, j, nu, nj - 1))),
            pl.BlockSpec((1, 1, nc), lambda i, j, te, nu, tr: (te[i], 0, nj + frozen(i, j, nu, nj - 1))),
        ],
        out_specs=pl.BlockSpec((tm, nc), lambda i, j, te, nu, tr: (live(i, nu), frozen(i, j, nu, nj - 1))),
    )
    return pl.pallas_call(
        functools.partial(_ffn_down_kernel, tm=tm, sub=sub),
        grid_spec=down_spec,
        out_shape=jax.ShapeDtypeStruct((n_slots, HALF), jnp.uint32),
        compiler_params=_cparams(("arbitrary", "arbitrary")),
        name="ffn_down",
    )(tile_expert, n_used, tile_rows, act, w_down, w_down, b_down3, b_down3)


def _combine_kernel(d_ref, h_ref, g_ref, nw_ref, yb_ref, o_ref, ybuf, sem, *, tm):
    def body(r, carry):
        for k in range(TOP_K):
            d = d_ref[r * TOP_K + k]
            pltpu.make_async_copy(yb_ref.at[pl.ds(d, 1)], ybuf.at[k, pl.ds(r, 1)], sem).start()
        return carry

    lax.fori_loop(0, tm, body, 0)
    for k in range(TOP_K):
        pltpu.make_async_copy(yb_ref.at[pl.ds(0, tm)], ybuf.at[k], sem).wait()
    lo = h_ref[:, :HALF]
    hi = h_ref[:, HALF:]
    for k in range(TOP_K):
        w = ybuf[k]
        g = g_ref[:, k:k + 1]
        lo = lo + g * _unpack_lo(w)
        hi = hi + g * _unpack_hi(w)
    ms = (jnp.sum(lo * lo, axis=-1, keepdims=True) + jnp.sum(hi * hi, axis=-1, keepdims=True)) / D_MODEL
    scale = lax.rsqrt(ms + EPS)
    o_ref[:, :HALF] = lo * scale * nw_ref[:, :HALF]
    o_ref[:, HALF:] = hi * scale * nw_ref[:, HALF:]


def _combine(dest, h, gates, nw, yb, tm):
    T = h.shape[0]
    return pl.pallas_call(
        functools.partial(_combine_kernel, tm=tm),
        grid=(T // tm,),
        in_specs=[
            pl.BlockSpec((tm * TOP_K,), lambda i: (i,), memory_space=pltpu.SMEM),
            pl.BlockSpec((tm, D_MODEL), lambda i: (i, 0)),
            pl.BlockSpec((tm, LANES), lambda i: (i, 0)),
            pl.BlockSpec((1, D_MODEL), lambda i: (0, 0)),
            pl.BlockSpec(memory_space=pl.ANY),
        ],
        out_specs=pl.BlockSpec((tm, D_MODEL), lambda i: (i, 0)),
        out_shape=jax.ShapeDtypeStruct((T, D_MODEL), F32),
        scratch_shapes=[pltpu.VMEM((TOP_K, tm, HALF), jnp.uint32), pltpu.SemaphoreType.DMA(())],
        compiler_params=_cparams(("arbitrary",)),
        name="combine",
    )(dest, h, gates, nw, yb)


def _row(v, width=None):
    v = v.astype(F32).reshape(1, -1)
    if width is not None and v.shape[1] < width:
        v = jnp.pad(v, ((0, 0), (0, width - v.shape[1])))
    return v


def _tile(n, pref):
    t = pref
    while n % t:
        t //= 2
    return t


def kernel(x, norm_mix_w, w_in, w_pool, pool_scale, conv_w, conv_b, dt_bias, a_log, d_skip,
           ssd_norm_w, w_ssd_out, w_out, norm_ffn_w, router_w, router_b, w_up, b_up,
           w_down, b_down, norm_final_w):
    B, S, D = x.shape
    T = B * S
    x2 = x.reshape(T, D)

    o_p, o_z, o_xbc = 0, D_MODEL, D_MODEL + SSD_INNER
    o_dt = o_xbc + SSD_INNER + 2 * SSD_BC
    o_gp = o_dt + SSD_HEADS
    o_gs = o_gp + D_MODEL
    w_main = jnp.concatenate(
        [w_in[:, o_z:o_z + SSD_INNER], w_in[:, o_xbc:o_dt], w_in[:, o_p:o_p + D_MODEL],
         w_in[:, o_gp:o_gp + D_MODEL], w_in[:, o_gs:o_gs + D_MODEL]], axis=1).astype(BF16)
    w_dt = jnp.pad(w_in[:, o_dt:o_dt + SSD_HEADS], ((0, 0), (0, LANES - SSD_HEADS))).astype(BF16)

    proj, dt_raw = _inproj(x2, _row(norm_mix_w), w_main, w_dt, _tile(T, 1024), 1024)

    pool_g = _pool(proj, w_pool.astype(BF16), _row(pool_scale), _tile(S, 256), S)

    yn = _ssd(proj, dt_raw,
              conv_w[:, :SSD_INNER].astype(F32), _row(conv_b[:SSD_INNER]),
              conv_w[:, SSD_INNER:].astype(F32), _row(conv_b[SSD_INNER:]),
              _row(dt_bias, LANES), _row(a_log, LANES),
              _row(jnp.repeat(d_skip, SSD_HEADDIM)), _row(ssd_norm_w), B, S)

    rw = jnp.pad(router_w.astype(F32), ((0, 0), (0, LANES - N_EXPERTS)))
    rw_hi = rw.astype(BF16)
    rw_lo = (rw - rw_hi.astype(F32)).astype(BF16)
    h, u_pk, logits = _mixout(pool_g, yn, proj, x2, w_ssd_out.astype(BF16), w_out.astype(BF16),
                              _row(norm_ffn_w), rw_hi, rw_lo, _row(router_b, LANES), _tile(T, 256))

    er, gates, cnt = _router(logits, _tile(T, 256))

    TK = T * TOP_K
    tm_ffn = 1024 if TK >= 32768 else 256
    sub = 256
    n_tiles = TK // tm_ffn + N_EXPERTS
    counts = cnt[0, :N_EXPERTS].astype(jnp.int32)
    tiles_e = (counts + tm_ffn - 1) // tm_ffn
    tend = jnp.cumsum(tiles_e)
    tstart = tend - tiles_e
    pstart = (tstart * tm_ffn).astype(jnp.int32)
    tile_id = jnp.arange(n_tiles, dtype=jnp.int32)
    tile_expert = jnp.minimum(jnp.searchsorted(tend, tile_id, side='right'), N_EXPERTS - 1).astype(jnp.int32)
    tile_rows = jnp.clip(counts[tile_expert] - (tile_id - tstart[tile_expert]) * tm_ffn, 0, tm_ffn).astype(jnp.int32)
    n_used = tend[-1].astype(jnp.int32).reshape(1)
    e_flat = er[:, :TOP_K].reshape(TK)
    r_flat = er[:, TOP_K:2 * TOP_K].reshape(TK)

    xb, dest = _dispatch(pstart, e_flat, r_flat, u_pk, n_tiles * tm_ffn, _tile(T, 512))
    yb = _ffn(tile_expert, n_used, tile_rows, xb, w_up, b_up.reshape(N_EXPERTS, 1, 2 * D_FF),
              w_down, b_down.reshape(N_EXPERTS, 1, D_MODEL), tm_ffn, sub, 512, 512)
    out = _combine(dest, h, gates, _row(norm_final_w), yb, _tile(T, 256))
    return out.reshape(B, S, D)
```

```python
import functools

import jax
import jax.numpy as jnp
from jax import lax
from jax.experimental import pallas as pl
from jax.experimental.pallas import tpu as pltpu

F32 = jnp.float32
BF16 = jnp.bfloat16

D_MODEL = 2048
EPS = 1e-5
POOL_GROUPS = 4
POOL_WINDOWS = (2, 4, 8, 16)
POOL_GW = D_MODEL // POOL_GROUPS
POOL_HALO = 16
SSD_INNER = 2 * D_MODEL
SSD_HEADDIM = 64
SSD_HEADS = SSD_INNER // SSD_HEADDIM
SSD_GROUPS = 8
SSD_HPG = SSD_HEADS // SSD_GROUPS
SSD_STATE = 128
SSD_CONV = 4
SSD_CHUNK = 128
SSD_GW = SSD_HPG * SSD_HEADDIM
SSD_BC = SSD_GROUPS * SSD_STATE
N_EXPERTS = 32
TOP_K = 4
D_FF = D_MODEL
SWIGLU_ALPHA = 1.702
SWIGLU_LIMIT = 7.0

LANES = 128
HALF = D_MODEL // 2
CONV_TAIL = 8
CONV_SLAB = 256
QUAD = 4
QUAD_W = QUAD * SSD_HEADDIM
VMEM_LIMIT = 56 * 1024 * 1024

COL_Z = 0
COL_XS = SSD_INNER
COL_BC = 2 * SSD_INNER
COL_P = 2 * SSD_INNER + 2 * SSD_BC
COL_GP = COL_P + D_MODEL
COL_GS = COL_GP + D_MODEL
PROJ_COLS = COL_GS + D_MODEL

NEG_BIG = -1e30


def _cparams(sem):
    return pltpu.CompilerParams(dimension_semantics=sem, vmem_limit_bytes=VMEM_LIMIT)


def _inproj_kernel(x_ref, nw_ref, w_ref, wdt_ref, o_ref, dt_ref, u_scr):
    @pl.when(pl.program_id(1) == 0)
    def _():
        x = x_ref[...]
        ms = jnp.mean(x * x, axis=-1, keepdims=True)
        u = (x * lax.rsqrt(ms + EPS) * nw_ref[...]).astype(BF16)
        u_scr[...] = u
        dt_ref[...] = jnp.dot(u, wdt_ref[...], preferred_element_type=F32)

    o_ref[...] = jnp.dot(u_scr[...], w_ref[...], preferred_element_type=F32).astype(o_ref.dtype)


def _inproj(x2, norm_w, w_main, w_dt, tm, tn):
    T = x2.shape[0]
    return pl.pallas_call(
        _inproj_kernel,
        grid=(T // tm, PROJ_COLS // tn),
        in_specs=[
            pl.BlockSpec((tm, D_MODEL), lambda i, j: (i, 0)),
            pl.BlockSpec((1, D_MODEL), lambda i, j: (0, 0)),
            pl.BlockSpec((D_MODEL, tn), lambda i, j: (0, j)),
            pl.BlockSpec((D_MODEL, LANES), lambda i, j: (0, 0)),
        ],
        out_specs=[
            pl.BlockSpec((tm, tn), lambda i, j: (i, j)),
            pl.BlockSpec((tm, LANES), lambda i, j: (i, 0)),
        ],
        out_shape=[
            jax.ShapeDtypeStruct((T, PROJ_COLS), BF16),
            jax.ShapeDtypeStruct((T, LANES), F32),
        ],
        scratch_shapes=[pltpu.VMEM((tm, D_MODEL), BF16)],
        compiler_params=_cparams(("parallel", "arbitrary")),
        name="inproj",
    )(x2, norm_w, w_main, w_dt)


def _pool_kernel(halo_ref, p_ref, g_ref, wp_ref, ps_ref, o_ref, buf, *, tm, seq):
    row0 = (pl.program_id(0) * tm) % seq
    halo = jnp.where(row0 == 0, 0.0, halo_ref[...].astype(F32))
    pos = row0 + lax.broadcasted_iota(jnp.int32, (tm, 1), 0)
    for g, w in enumerate(POOL_WINDOWS):
        cols = slice(g * POOL_GW, (g + 1) * POOL_GW)
        buf[0:POOL_HALO, :] = halo[:, cols]
        buf[POOL_HALO:, :] = p_ref[:, cols].astype(F32)
        cur = buf[POOL_HALO:POOL_HALO + tm, :]
        acc = cur
        for k in range(1, w):
            acc = acc + buf[POOL_HALO - k:POOL_HALO - k + tm, :]
        cnt = jnp.minimum(pos + 1, w).astype(F32)
        pooled = acc / cnt - cur
        y = jnp.dot(pooled.astype(BF16), wp_ref[g], preferred_element_type=F32)
        y = y * ps_ref[:, cols] * jax.nn.sigmoid(g_ref[:, cols].astype(F32))
        o_ref[:, cols] = y.astype(o_ref.dtype)


def _pool(proj, w_pool, pool_scale, tm, seq):
    T = proj.shape[0]
    hb = tm // POOL_HALO
    return pl.pallas_call(
        functools.partial(_pool_kernel, tm=tm, seq=seq),
        grid=(T // tm,),
        in_specs=[
            pl.BlockSpec((POOL_HALO, D_MODEL),
                         lambda i: (jnp.maximum(i * hb - 1, 0), COL_P // D_MODEL)),
            pl.BlockSpec((tm, D_MODEL), lambda i: (i, COL_P // D_MODEL)),
            pl.BlockSpec((tm, D_MODEL), lambda i: (i, COL_GP // D_MODEL)),
            pl.BlockSpec((POOL_GROUPS, POOL_GW, POOL_GW), lambda i: (0, 0, 0)),
            pl.BlockSpec((1, D_MODEL), lambda i: (0, 0)),
        ],
        out_specs=pl.BlockSpec((tm, D_MODEL), lambda i: (i, 0)),
        out_shape=jax.ShapeDtypeStruct((T, D_MODEL), BF16),
        scratch_shapes=[pltpu.VMEM((POOL_HALO + tm, POOL_GW), F32)],
        compiler_params=_cparams(("parallel",)),
        name="pool",
    )(proj, proj, proj, w_pool, pool_scale)


def _split3(v):
    hi = v.astype(BF16)
    r1 = v - hi.astype(F32)
    mid = r1.astype(BF16)
    lo = (r1 - mid.astype(F32)).astype(BF16)
    return hi, mid, lo


def _ssd_kernel(z_ref, xs_ref, bc_ref, dt_ref, cwx_ref, cbx_ref, cwbc_ref, cbbc_ref,
                dtb_ref, alog_ref, dsk_ref, nw_ref, qm_ref, o_ref,
                tail_x, tail_bc, cbuf, xs_scr, b_scr, c_scr, a_scr, qt_scr, wt_scr, state):
    L = SSD_CHUNK
    c = pl.program_id(1)

    @pl.when(c == 0)
    def _():
        tail_x[...] = jnp.zeros_like(tail_x)
        tail_bc[...] = jnp.zeros_like(tail_bc)
        state[...] = jnp.zeros_like(state)

    def conv_slab(src_ref, tail_ref, cw_ref, cb_ref, col):
        cols = slice(col, col + CONV_SLAB)
        raw = src_ref[:, cols].astype(F32)
        cbuf[0:CONV_TAIL, :] = tail_ref[:, cols]
        cbuf[CONV_TAIL:, :] = raw
        acc = cb_ref[:, cols] + cw_ref[SSD_CONV - 1:SSD_CONV, cols] * raw
        for k in range(SSD_CONV - 1):
            off = CONV_TAIL - (SSD_CONV - 1) + k
            acc = acc + cw_ref[k:k + 1, cols] * cbuf[off:off + L, :]
        tail_ref[:, cols] = raw[L - CONV_TAIL:, :]
        return acc * jax.nn.sigmoid(acc)

    for s in range(SSD_INNER // CONV_SLAB):
        xs_scr[:, s * CONV_SLAB:(s + 1) * CONV_SLAB] = conv_slab(xs_ref, tail_x, cwx_ref, cbx_ref, s * CONV_SLAB)
    for s in range(SSD_BC // CONV_SLAB):
        b_scr[:, s * CONV_SLAB:(s + 1) * CONV_SLAB] = conv_slab(bc_ref, tail_bc, cwbc_ref, cbbc_ref, s * CONV_SLAB)
    for s in range(SSD_BC // CONV_SLAB):
        c_scr[:, s * CONV_SLAB:(s + 1) * CONV_SLAB] = conv_slab(
            bc_ref, tail_bc, cwbc_ref, cbbc_ref, SSD_BC + s * CONV_SLAB).astype(BF16)

    dtv = jax.nn.softplus(dt_ref[...] + dtb_ref[...])
    d_a = dtv * (-jnp.exp(alog_ref[...]))
    rr = lax.broadcasted_iota(jnp.int32, (L, L), 0)
    cc = lax.broadcasted_iota(jnp.int32, (L, L), 1)
    causal = rr >= cc
    low_half = cc < SSD_HEADDIM
    tri = jnp.where(causal, 1.0, 0.0).astype(BF16)
    a_cum = sum(jnp.dot(tri, part, preferred_element_type=F32) for part in _split3(d_a))
    a_t = a_cum.T
    dt_t = dtv.T
    a_scr[...] = a_cum
    qt_scr[...] = a_t - jnp.log(dt_t)
    wt_scr[...] = dt_t * jnp.exp(a_t[:, L - 1:L] - a_t)

    def group_body(g, carry):
        off = pl.multiple_of(g * SSD_GW, SSD_GW)
        offn = pl.multiple_of(g * SSD_STATE, SSD_STATE)
        xg = xs_scr[:, pl.ds(off, SSD_GW)]
        xg_b = xg.astype(BF16)
        b_f = b_scr[:, pl.ds(offn, SSD_STATE)]
        c_b = c_scr[:, pl.ds(offn, SSD_STATE)]
        b_t = b_f.T
        cbm = lax.dot_general(c_b, b_f.astype(BF16), (((1,), (1,)), ((), ())),
                              preferred_element_type=F32)
        s_g = state[g]
        y_off = jnp.dot(c_b, s_g.astype(BF16), preferred_element_type=F32)
        a_g = pltpu.roll(a_scr[...], (LANES - SSD_HPG * g) % LANES, 1)
        ys = []
        for quad in range(SSD_HPG // QUAD):
            qs = slice(quad * QUAD_W, (quad + 1) * QUAD_W)
            m_parts, b_parts, a_cols = [], [], []
            for jj in range(QUAD):
                j = quad * QUAD + jj
                h = g * SSD_HPG + j
                a_col = jnp.broadcast_to(a_g[:, j:j + 1], (L, L))
                seg = a_col - qt_scr[pl.ds(h, 1), :]
                m_parts.append((cbm * jnp.exp(jnp.where(causal, seg, NEG_BIG))).astype(BF16))
                b_parts.append((b_t * wt_scr[pl.ds(h, 1), :]).astype(BF16))
                a_cols.append(a_col)
            lhs = jnp.concatenate([jnp.concatenate(m_parts, axis=1), jnp.concatenate(b_parts, axis=1)], axis=0)
            xq = xg_b[:, qs]
            rhs = jnp.concatenate([xq * qm_ref[jj] for jj in range(QUAD)], axis=0)
            res = jnp.dot(lhs, rhs, preferred_element_type=F32)
            ea_q = jnp.concatenate(
                [jnp.exp(jnp.where(low_half, a_cols[2 * p], a_cols[2 * p + 1])) for p in range(QUAD // 2)], axis=1)
            ys.append(res[:L] + y_off[:, qs] * ea_q)
            state[g, :, qs] = s_g[:, qs] * ea_q[L - 1:L, :] + res[L:]
        y = jnp.concatenate(ys, axis=1) + dsk_ref[:, pl.ds(off, SSD_GW)] * xg
        zg = z_ref[:, pl.ds(off, SSD_GW)].astype(F32)
        yg = y * (zg * jax.nn.sigmoid(zg))
        ms = jnp.mean(yg * yg, axis=-1, keepdims=True)
        yn = yg * lax.rsqrt(ms + EPS) * nw_ref[:, pl.ds(off, SSD_GW)]
        o_ref[:, pl.ds(off, SSD_GW)] = yn.astype(o_ref.dtype)
        return carry

    lax.fori_loop(0, SSD_GROUPS, group_body, 0)


def _ssd(proj, dt_raw, cwx, cbx, cwbc, cbbc, dtb, alog, dsk, nw, batch, seq):
    L = SSD_CHUNK
    nc = seq // L
    T = proj.shape[0]
    row = lambda b, c: b * nc + c
    const = lambda b, c: (0, 0)
    head_of_lane = jnp.arange(QUAD_W, dtype=jnp.int32) // SSD_HEADDIM
    qmask = (head_of_lane[None, None, :] == jnp.arange(QUAD, dtype=jnp.int32)[:, None, None])
    qmask = jnp.broadcast_to(qmask, (QUAD, L, QUAD_W)).astype(BF16)
    return pl.pallas_call(
        _ssd_kernel,
        grid=(batch, nc),
        in_specs=[
            pl.BlockSpec((L, SSD_INNER), lambda b, c: (row(b, c), COL_Z // SSD_INNER)),
            pl.BlockSpec((L, SSD_INNER), lambda b, c: (row(b, c), COL_XS // SSD_INNER)),
            pl.BlockSpec((L, 2 * SSD_BC), lambda b, c: (row(b, c), COL_BC // (2 * SSD_BC))),
            pl.BlockSpec((L, LANES), lambda b, c: (row(b, c), 0)),
            pl.BlockSpec((SSD_CONV, SSD_INNER), const),
            pl.BlockSpec((1, SSD_INNER), const),
            pl.BlockSpec((SSD_CONV, 2 * SSD_BC), const),
            pl.BlockSpec((1, 2 * SSD_BC), const),
            pl.BlockSpec((1, LANES), const),
            pl.BlockSpec((1, LANES), const),
            pl.BlockSpec((1, SSD_INNER), const),
            pl.BlockSpec((1, SSD_INNER), const),
            pl.BlockSpec((QUAD, L, QUAD_W), lambda b, c: (0, 0, 0)),
        ],
        out_specs=pl.BlockSpec((L, SSD_INNER), lambda b, c: (row(b, c), 0)),
        out_shape=jax.ShapeDtypeStruct((T, SSD_INNER), BF16),
        scratch_shapes=[
            pltpu.VMEM((CONV_TAIL, SSD_INNER), F32),
            pltpu.VMEM((CONV_TAIL, 2 * SSD_BC), F32),
            pltpu.VMEM((CONV_TAIL + L, CONV_SLAB), F32),
            pltpu.VMEM((L, SSD_INNER), F32),
            pltpu.VMEM((L, SSD_BC), F32),
            pltpu.VMEM((L, SSD_BC), BF16),
            pltpu.VMEM((L, LANES), F32),
            pltpu.VMEM((LANES, L), F32),
            pltpu.VMEM((LANES, L), F32),
            pltpu.VMEM((SSD_GROUPS, SSD_STATE, SSD_GW), F32),
        ],
        compiler_params=_cparams(("arbitrary", "arbitrary")),
        name="ssd",
    )(proj, proj, proj, dt_raw, cwx, cbx, cwbc, cbbc, dtb, alog, dsk, nw, qmask)


def _pack_bf16_pairs(v):
    bits = pltpu.bitcast(v.astype(BF16).astype(F32), jnp.uint32)
    return (bits[:, :HALF] >> 16) | (bits[:, HALF:] & jnp.uint32(0xFFFF0000))


def _unpack_lo(w):
    return pltpu.bitcast(w << 16, F32)


def _unpack_hi(w):
    return pltpu.bitcast(w & jnp.uint32(0xFFFF0000), F32)


def _mixout_kernel(pg_ref, yn_ref, gs_ref, x_ref, wso_ref, wo_ref, nw_ref, rwh_ref, rwl_ref, rb_ref,
                   h_ref, u_ref, lg_ref):
    ssd = jnp.dot(yn_ref[...], wso_ref[...], preferred_element_type=F32)
    merged = pg_ref[...].astype(F32) + jax.nn.sigmoid(gs_ref[...].astype(F32)) * ssd
    h = x_ref[...] + jnp.dot(merged.astype(BF16), wo_ref[...], preferred_element_type=F32)
    h_ref[...] = h
    ms = jnp.mean(h * h, axis=-1, keepdims=True)
    u = h * lax.rsqrt(ms + EPS) * nw_ref[...]
    u_hi = u.astype(BF16)
    u_ref[...] = _pack_bf16_pairs(u)
    u_lo = (u - u_hi.astype(F32)).astype(BF16)
    rwh = rwh_ref[...]
    lg = (jnp.dot(u_hi, rwh, preferred_element_type=F32)
          + jnp.dot(u_lo, rwh, preferred_element_type=F32)
          + jnp.dot(u_hi, rwl_ref[...], preferred_element_type=F32))
    lg_ref[...] = lg + rb_ref[...]


def _mixout(pool_g, yn, proj, x2, wso, wo, nw, rwh, rwl, rb, tm):
    T = x2.shape[0]
    const = lambda i: (0, 0)
    return pl.pallas_call(
        _mixout_kernel,
        grid=(T // tm,),
        in_specs=[
            pl.BlockSpec((tm, D_MODEL), lambda i: (i, 0)),
            pl.BlockSpec((tm, SSD_INNER), lambda i: (i, 0)),
            pl.BlockSpec((tm, D_MODEL), lambda i: (i, COL_GS // D_MODEL)),
            pl.BlockSpec((tm, D_MODEL), lambda i: (i, 0)),
            pl.BlockSpec((SSD_INNER, D_MODEL), const),
            pl.BlockSpec((D_MODEL, D_MODEL), const),
            pl.BlockSpec((1, D_MODEL), const),
            pl.BlockSpec((D_MODEL, LANES), const),
            pl.BlockSpec((D_MODEL, LANES), const),
            pl.BlockSpec((1, LANES), const),
        ],
        out_specs=[
            pl.BlockSpec((tm, D_MODEL), lambda i: (i, 0)),
            pl.BlockSpec((tm, HALF), lambda i: (i, 0)),
            pl.BlockSpec((tm, LANES), lambda i: (i, 0)),
        ],
        out_shape=[
            jax.ShapeDtypeStruct((T, D_MODEL), F32),
            jax.ShapeDtypeStruct((T, HALF), jnp.uint32),
            jax.ShapeDtypeStruct((T, LANES), F32),
        ],
        compiler_params=_cparams(("parallel",)),
        name="mixout",
    )(pool_g, yn, proj, x2, wso, wo, nw, rwh, rwl, rb)


def _router_kernel(lg_ref, er_ref, gt_ref, cnt_ref, run_scr, *, tm):
    @pl.when(pl.program_id(0) == 0)
    def _():
        run_scr[...] = jnp.zeros_like(run_scr)

    lane = lax.broadcasted_iota(jnp.int32, (tm, LANES), 1)
    lg = jnp.where(lane < N_EXPERTS, lg_ref[...], -jnp.inf)
    idxs, vals, hots = [], [], []
    for _ in range(TOP_K):
        m = jnp.max(lg, axis=-1, keepdims=True)
        idx = jnp.min(jnp.where(lg == m, lane, LANES), axis=-1, keepdims=True)
        hot = lane == idx
        lg = jnp.where(hot, -jnp.inf, lg)
        idxs.append(idx)
        vals.append(m)
        hots.append(hot)
    exps = [jnp.exp(v - vals[0]) for v in vals]
    den = exps[0] + exps[1] + exps[2] + exps[3]
    cnt = sum(jnp.where(hot, 1.0, 0.0) for hot in hots)
    rr = lax.broadcasted_iota(jnp.int32, (tm, tm), 0)
    cc = lax.broadcasted_iota(jnp.int32, (tm, tm), 1)
    before = jnp.where(rr > cc, 1.0, 0.0).astype(BF16)
    base = jnp.dot(before, cnt.astype(BF16), preferred_element_type=F32) + run_scr[...]
    er = jnp.zeros((tm, LANES), jnp.int32)
    gt = jnp.zeros((tm, LANES), F32)
    for k in range(TOP_K):
        rank = jnp.sum(jnp.where(hots[k], base, 0.0), axis=-1, keepdims=True)
        er = jnp.where(lane == k, idxs[k], er)
        er = jnp.where(lane == TOP_K + k, rank.astype(jnp.int32), er)
        gt = jnp.where(lane == k, exps[k] / den, gt)
    er_ref[...] = er
    gt_ref[...] = gt
    run_scr[...] += jnp.sum(cnt, axis=0, keepdims=True)
    cnt_ref[...] = run_scr[...]


def _router(logits, tm):
    T = logits.shape[0]
    return pl.pallas_call(
        functools.partial(_router_kernel, tm=tm),
        grid=(T // tm,),
        in_specs=[pl.BlockSpec((tm, LANES), lambda i: (i, 0))],
        out_specs=[
            pl.BlockSpec((tm, LANES), lambda i: (i, 0)),
            pl.BlockSpec((tm, LANES), lambda i: (i, 0)),
            pl.BlockSpec((1, LANES), lambda i: (0, 0)),
        ],
        out_shape=[
            jax.ShapeDtypeStruct((T, LANES), jnp.int32),
            jax.ShapeDtypeStruct((T, LANES), F32),
            jax.ShapeDtypeStruct((1, LANES), F32),
        ],
        scratch_shapes=[pltpu.VMEM((1, LANES), F32)],
        compiler_params=_cparams(("arbitrary",)),
        name="router",
    )(logits)


def _dispatch_kernel(ps_ref, e_ref, r_ref, u_ref, xb_ref, dest_ref, sem, *, tm):
    def body(r, carry):
        for k in range(TOP_K):
            j = r * TOP_K + k
            d = ps_ref[e_ref[j]] + r_ref[j]
            dest_ref[j] = d
            pltpu.make_async_copy(u_ref.at[pl.ds(r, 1)], xb_ref.at[pl.ds(d, 1)], sem).start()
        return carry

    lax.fori_loop(0, tm, body, 0)
    for _ in range(TOP_K):
        pltpu.make_async_copy(u_ref, xb_ref.at[pl.ds(0, tm)], sem).wait()


def _dispatch(pstart, e_flat, r_flat, u_pk, n_slots, tm):
    T = u_pk.shape[0]
    grid_spec = pltpu.PrefetchScalarGridSpec(
        num_scalar_prefetch=1,
        grid=(T // tm,),
        in_specs=[
            pl.BlockSpec((tm * TOP_K,), lambda i, ps: (i,), memory_space=pltpu.SMEM),
            pl.BlockSpec((tm * TOP_K,), lambda i, ps: (i,), memory_space=pltpu.SMEM),
            pl.BlockSpec((tm, HALF), lambda i, ps: (i, 0)),
        ],
        out_specs=[
            pl.BlockSpec(memory_space=pl.ANY),
            pl.BlockSpec((tm * TOP_K,), lambda i, ps: (i,), memory_space=pltpu.SMEM),
        ],
        scratch_shapes=[pltpu.SemaphoreType.DMA(())],
    )
    return pl.pallas_call(
        functools.partial(_dispatch_kernel, tm=tm),
        grid_spec=grid_spec,
        out_shape=[
            jax.ShapeDtypeStruct((n_slots, HALF), jnp.uint32),
            jax.ShapeDtypeStruct((T * TOP_K,), jnp.int32),
        ],
        compiler_params=_cparams(("arbitrary",)),
        name="dispatch",
    )(pstart, e_flat, r_flat, u_pk)


def _row_blocks(nrows, tm, sub, weights, block):
    @pl.when(nrows == tm)
    def _():
        block(0, tm, weights())

    @pl.when(jnp.logical_and(nrows > 0, nrows < tm))
    def _():
        w = weights()
        for sb in range(tm // sub):
            @pl.when(sb * sub < nrows)
            def _():
                block(sb * sub, sub, w)


def _ffn_up_kernel(te_ref, nu_ref, tr_ref, x_ref, wg_ref, wl_ref, bg_ref, bl_ref, o_ref, xs, *, tm, sub):
    nrows = tr_ref[pl.program_id(0)]

    @pl.when(jnp.logical_and(nrows > 0, pl.program_id(1) == 0))
    def _():
        w = x_ref[...]
        xs[:, :HALF] = _unpack_lo(w).astype(BF16)
        xs[:, HALF:] = _unpack_hi(w).astype(BF16)

    def weights():
        return wg_ref[0].astype(BF16), wl_ref[0].astype(BF16)

    def block(r0, nr, w):
        x = xs[r0:r0 + nr, :]
        glu = jnp.minimum(jnp.dot(x, w[0], preferred_element_type=F32) + bg_ref[0], SWIGLU_LIMIT)
        lin = jnp.clip(jnp.dot(x, w[1], preferred_element_type=F32) + bl_ref[0], -SWIGLU_LIMIT, SWIGLU_LIMIT)
        act = glu * jax.nn.sigmoid(SWIGLU_ALPHA * glu) * (lin + 1.0)
        o_ref[r0:r0 + nr, :] = act.astype(o_ref.dtype)

    _row_blocks(nrows, tm, sub, weights, block)


def _ffn_down_kernel(te_ref, nu_ref, tr_ref, a_ref, wa_ref, wb_ref, ba_ref, bb_ref, o_ref, *, tm, sub):
    nrows = tr_ref[pl.program_id(0)]

    def weights():
        return wa_ref[0].astype(BF16), wb_ref[0].astype(BF16)

    def block(r0, nr, w):
        a = a_ref[r0:r0 + nr, :]
        ya = jnp.dot(a, w[0], preferred_element_type=F32) + ba_ref[0]
        yb = jnp.dot(a, w[1], preferred_element_type=F32) + bb_ref[0]
        lo = pltpu.bitcast(ya.astype(BF16).astype(F32), jnp.uint32) >> 16
        hi = pltpu.bitcast(yb.astype(BF16).astype(F32), jnp.uint32) & jnp.uint32(0xFFFF0000)
        o_ref[r0:r0 + nr, :] = lo | hi

    _row_blocks(nrows, tm, sub, weights, block)


def _ffn(tile_expert, n_used, tile_rows, xb, w_up, b_up3, w_down, b_down3, tm, sub, fc, nc):
    n_slots = xb.shape[0]
    n_tiles = n_slots // tm
    nf = D_FF // fc
    nj = HALF // nc

    def live(i, nu):
        return jnp.minimum(i, nu[0] - 1)

    def frozen(i, j, nu, last):
        return jnp.where(i < nu[0], j, last)

    up_spec = pltpu.PrefetchScalarGridSpec(
        num_scalar_prefetch=3,
        grid=(n_tiles, nf),
        in_specs=[
            pl.BlockSpec((tm, HALF), lambda i, f, te, nu, tr: (live(i, nu), 0)),
            pl.BlockSpec((1, D_MODEL, fc), lambda i, f, te, nu, tr: (te[i], 0, frozen(i, f, nu, nf - 1))),
            pl.BlockSpec((1, D_MODEL, fc), lambda i, f, te, nu, tr: (te[i], 0, nf + frozen(i, f, nu, nf - 1))),
            pl.BlockSpec((1, 1, fc), lambda i, f, te, nu, tr: (te[i], 0, frozen(i, f, nu, nf - 1))),
            pl.BlockSpec((1, 1, fc), lambda i, f, te, nu, tr: (te[i], 0, nf + frozen(i, f, nu, nf - 1))),
        ],
        out_specs=pl.BlockSpec((tm, fc), lambda i, f, te, nu, tr: (live(i, nu), frozen(i, f, nu, nf - 1))),
        scratch_shapes=[pltpu.VMEM((tm, D_MODEL), BF16)],
    )
    act = pl.pallas_call(
        functools.partial(_ffn_up_kernel, tm=tm, sub=sub),
        grid_spec=up_spec,
        out_shape=jax.ShapeDtypeStruct((n_slots, D_FF), BF16),
        compiler_params=_cparams(("arbitrary", "arbitrary")),
        name="ffn_up",
    )(tile_expert, n_used, tile_rows, xb, w_up, w_up, b_up3, b_up3)

    down_spec = pltpu.PrefetchScalarGridSpec(
        num_scalar_prefetch=3,
        grid=(n_tiles, nj),
        in_specs=[
            pl.BlockSpec((tm, D_FF), lambda i, j, te, nu, tr: (live(i, nu), 0)),
            pl.BlockSpec((1, D_FF, nc), lambda i, j, te, nu, tr: (te[i], 0, frozen(i, j, nu, nj - 1))),
            pl.BlockSpec((1, D_FF, nc), lambda i, j, te, nu, tr: (te[i], 0, nj + frozen(i, j, nu, nj - 1))),
            pl.BlockSpec((1, 1, nc), lambda i, j, te, nu, tr: (te[i], 0, frozen(i, j, nu, nj - 1))),
            pl.BlockSpec((1, 1, nc), lambda i, j, te, nu, tr: (te[i], 0, nj + frozen(i, j, nu, nj - 1))),
        ],
        out_specs=pl.BlockSpec((tm, nc), lambda i, j, te, nu, tr: (live(i, nu), frozen(i, j, nu, nj - 1))),
    )
    return pl.pallas_call(
        functools.partial(_ffn_down_kernel, tm=tm, sub=sub),
        grid_spec=down_spec,
        out_shape=jax.ShapeDtypeStruct((n_slots, HALF), jnp.uint32),
        compiler_params=_cparams(("arbitrary", "arbitrary")),
        name="ffn_down",
    )(tile_expert, n_used, tile_rows, act, w_down, w_down, b_down3, b_down3)


def _combine_kernel(d_ref, h_ref, g_ref, nw_ref, yb_ref, o_ref, ybuf, sem, *, tm):
    def body(r, carry):
        for k in range(TOP_K):
            d = d_ref[r * TOP_K + k]
            pltpu.make_async_copy(yb_ref.at[pl.ds(d, 1)], ybuf.at[k, pl.ds(r, 1)], sem).start()
        return carry

    lax.fori_loop(0, tm, body, 0)
    for k in range(TOP_K):
        pltpu.make_async_copy(yb_ref.at[pl.ds(0, tm)], ybuf.at[k], sem).wait()
    lo = h_ref[:, :HALF]
    hi = h_ref[:, HALF:]
    for k in range(TOP_K):
        w = ybuf[k]
        g = g_ref[:, k:k + 1]
        lo = lo + g * _unpack_lo(w)
        hi = hi + g * _unpack_hi(w)
    ms = (jnp.sum(lo * lo, axis=-1, keepdims=True) + jnp.sum(hi * hi, axis=-1, keepdims=True)) / D_MODEL
    scale = lax.rsqrt(ms + EPS)
    o_ref[:, :HALF] = lo * scale * nw_ref[:, :HALF]
    o_ref[:, HALF:] = hi * scale * nw_ref[:, HALF:]


def _combine(dest, h, gates, nw, yb, tm):
    T = h.shape[0]
    return pl.pallas_call(
        functools.partial(_combine_kernel, tm=tm),
        grid=(T // tm,),
        in_specs=[
            pl.BlockSpec((tm * TOP_K,), lambda i: (i,), memory_space=pltpu.SMEM),
            pl.BlockSpec((tm, D_MODEL), lambda i: (i, 0)),
            pl.BlockSpec((tm, LANES), lambda i: (i, 0)),
            pl.BlockSpec((1, D_MODEL), lambda i: (0, 0)),
            pl.BlockSpec(memory_space=pl.ANY),
        ],
        out_specs=pl.BlockSpec((tm, D_MODEL), lambda i: (i, 0)),
        out_shape=jax.ShapeDtypeStruct((T, D_MODEL), F32),
        scratch_shapes=[pltpu.VMEM((TOP_K, tm, HALF), jnp.uint32), pltpu.SemaphoreType.DMA(())],
        compiler_params=_cparams(("arbitrary",)),
        name="combine",
    )(dest, h, gates, nw, yb)


def _row(v, width=None):
    v = v.astype(F32).reshape(1, -1)
    if width is not None and v.shape[1] < width:
        v = jnp.pad(v, ((0, 0), (0, width - v.shape[1])))
    return v


def _tile(n, pref):
    t = pref
    while n % t:
        t //= 2
    return t


def kernel(x, norm_mix_w, w_in, w_pool, pool_scale, conv_w, conv_b, dt_bias, a_log, d_skip,
           ssd_norm_w, w_ssd_out, w_out, norm_ffn_w, router_w, router_b, w_up, b_up,
           w_down, b_down, norm_final_w):
    B, S, D = x.shape
    T = B * S
    x2 = x.reshape(T, D)

    o_p, o_z, o_xbc = 0, D_MODEL, D_MODEL + SSD_INNER
    o_dt = o_xbc + SSD_INNER + 2 * SSD_BC
    o_gp = o_dt + SSD_HEADS
    o_gs = o_gp + D_MODEL
    w_main = jnp.concatenate(
        [w_in[:, o_z:o_z + SSD_INNER], w_in[:, o_xbc:o_dt], w_in[:, o_p:o_p + D_MODEL],
         w_in[:, o_gp:o_gp + D_MODEL], w_in[:, o_gs:o_gs + D_MODEL]], axis=1).astype(BF16)
    w_dt = jnp.pad(w_in[:, o_dt:o_dt + SSD_HEADS], ((0, 0), (0, LANES - SSD_HEADS))).astype(BF16)

    proj, dt_raw = _inproj(x2, _row(norm_mix_w), w_main, w_dt, _tile(T, 1024), 2048)

    pool_g = _pool(proj, w_pool.astype(BF16), _row(pool_scale), _tile(S, 256), S)

    yn = _ssd(proj, dt_raw,
              conv_w[:, :SSD_INNER].astype(F32), _row(conv_b[:SSD_INNER]),
              conv_w[:, SSD_INNER:].astype(F32), _row(conv_b[SSD_INNER:]),
              _row(dt_bias, LANES), _row(a_log, LANES),
              _row(jnp.repeat(d_skip, SSD_HEADDIM)), _row(ssd_norm_w), B, S)

    rw = jnp.pad(router_w.astype(F32), ((0, 0), (0, LANES - N_EXPERTS)))
    rw_hi = rw.astype(BF16)
    rw_lo = (rw - rw_hi.astype(F32)).astype(BF16)
    h, u_pk, logits = _mixout(pool_g, yn, proj, x2, w_ssd_out.astype(BF16), w_out.astype(BF16),
                              _row(norm_ffn_w), rw_hi, rw_lo, _row(router_b, LANES), _tile(T, 256))

    er, gates, cnt = _router(logits, _tile(T, 256))

    TK = T * TOP_K
    tm_ffn = 1024 if TK >= 32768 else 256
    sub = 256
    n_tiles = TK // tm_ffn + N_EXPERTS
    counts = cnt[0, :N_EXPERTS].astype(jnp.int32)
    tiles_e = (counts + tm_ffn - 1) // tm_ffn
    tend = jnp.cumsum(tiles_e)
    tstart = tend - tiles_e
    pstart = (tstart * tm_ffn).astype(jnp.int32)
    tile_id = jnp.arange(n_tiles, dtype=jnp.int32)
    tile_expert = jnp.minimum(jnp.searchsorted(tend, tile_id, side='right'), N_EXPERTS - 1).astype(jnp.int32)
    tile_rows = jnp.clip(counts[tile_expert] - (tile_id - tstart[tile_expert]) * tm_ffn, 0, tm_ffn).astype(jnp.int32)
    n_used = tend[-1].astype(jnp.int32).reshape(1)
    e_flat = er[:, :TOP_K].reshape(TK)
    r_flat = er[:, TOP_K:2 * TOP_K].reshape(TK)

    xb, dest = _dispatch(pstart, e_flat, r_flat, u_pk, n_tiles * tm_ffn, _tile(T, 512))
    yb = _ffn(tile_expert, n_used, tile_rows, xb, w_up, b_up.reshape(N_EXPERTS, 1, 2 * D_FF),
              w_down, b_down.reshape(N_EXPERTS, 1, D_MODEL), tm_ffn, sub, 512, 512)
    out = _combine(dest, h, gates, _row(norm_final_w), yb, _tile(T, 256))
    return out.reshape(B, S, D)
```

```python
import functools

import jax
import jax.numpy as jnp
from jax import lax
from jax.experimental import pallas as pl
from jax.experimental.pallas import tpu as pltpu

F32 = jnp.float32
BF16 = jnp.bfloat16

D_MODEL = 2048
EPS = 1e-5
POOL_GROUPS = 4
POOL_WINDOWS = (2, 4, 8, 16)
POOL_GW = D_MODEL // POOL_GROUPS
POOL_HALO = 16
SSD_INNER = 2 * D_MODEL
SSD_HEADDIM = 64
SSD_HEADS = SSD_INNER // SSD_HEADDIM
SSD_GROUPS = 8
SSD_HPG = SSD_HEADS // SSD_GROUPS
SSD_STATE = 128
SSD_CONV = 4
SSD_CHUNK = 128
SSD_GW = SSD_HPG * SSD_HEADDIM
SSD_BC = SSD_GROUPS * SSD_STATE
N_EXPERTS = 32
TOP_K = 4
D_FF = D_MODEL
SWIGLU_ALPHA = 1.702
SWIGLU_LIMIT = 7.0

LANES = 128
HALF = D_MODEL // 2
CONV_SLAB = 256
QUAD = 4
QUAD_W = QUAD * SSD_HEADDIM
ROW_GROUP = 8
VMEM_LIMIT = 56 * 1024 * 1024

COL_Z = 0
COL_XS = SSD_INNER
COL_BC = 2 * SSD_INNER
COL_P = 2 * SSD_INNER + 2 * SSD_BC
COL_GP = COL_P + D_MODEL
COL_GS = COL_GP + D_MODEL
PROJ_COLS = COL_GS + D_MODEL

NEG_BIG = -1e30


def _cparams(sem):
    return pltpu.CompilerParams(dimension_semantics=sem, vmem_limit_bytes=VMEM_LIMIT)


def _inproj_kernel(x_ref, nw_ref, w_ref, wdt_ref, o_ref, dt_ref, u_scr):
    @pl.when(pl.program_id(1) == 0)
    def _():
        x = x_ref[...]
        ms = jnp.mean(x * x, axis=-1, keepdims=True)
        u = (x * lax.rsqrt(ms + EPS) * nw_ref[...]).astype(BF16)
        u_scr[...] = u
        dt_ref[...] = jnp.dot(u, wdt_ref[...], preferred_element_type=F32)

    o_ref[...] = jnp.dot(u_scr[...], w_ref[...], preferred_element_type=F32).astype(o_ref.dtype)


def _inproj(x2, norm_w, w_main, w_dt, tm, tn):
    T = x2.shape[0]
    return pl.pallas_call(
        _inproj_kernel,
        grid=(T // tm, PROJ_COLS // tn),
        in_specs=[
            pl.BlockSpec((tm, D_MODEL), lambda i, j: (i, 0)),
            pl.BlockSpec((1, D_MODEL), lambda i, j: (0, 0)),
            pl.BlockSpec((D_MODEL, tn), lambda i, j: (0, j)),
            pl.BlockSpec((D_MODEL, LANES), lambda i, j: (0, 0)),
        ],
        out_specs=[
            pl.BlockSpec((tm, tn), lambda i, j: (i, j)),
            pl.BlockSpec((tm, LANES), lambda i, j: (i, 0)),
        ],
        out_shape=[
            jax.ShapeDtypeStruct((T, PROJ_COLS), BF16),
            jax.ShapeDtypeStruct((T, LANES), F32),
        ],
        scratch_shapes=[pltpu.VMEM((tm, D_MODEL), BF16)],
        compiler_params=_cparams(("parallel", "arbitrary")),
        name="inproj",
    )(x2, norm_w, w_main, w_dt)


def _pool_kernel(halo_ref, p_ref, g_ref, wp_ref, ps_ref, o_ref, buf, *, tm, seq):
    row0 = (pl.program_id(0) * tm) % seq
    halo = jnp.where(row0 == 0, 0.0, halo_ref[...].astype(F32))
    pos = row0 + lax.broadcasted_iota(jnp.int32, (tm, 1), 0)
    for g, w in enumerate(POOL_WINDOWS):
        cols = slice(g * POOL_GW, (g + 1) * POOL_GW)
        buf[0:POOL_HALO, :] = halo[:, cols]
        buf[POOL_HALO:, :] = p_ref[:, cols].astype(F32)
        cur = buf[POOL_HALO:POOL_HALO + tm, :]
        acc = cur
        for k in range(1, w):
            acc = acc + buf[POOL_HALO - k:POOL_HALO - k + tm, :]
        cnt = jnp.minimum(pos + 1, w).astype(F32)
        pooled = acc / cnt - cur
        y = jnp.dot(pooled.astype(BF16), wp_ref[g], preferred_element_type=F32)
        y = y * ps_ref[:, cols] * jax.nn.sigmoid(g_ref[:, cols].astype(F32))
        o_ref[:, cols] = y.astype(o_ref.dtype)


def _pool(proj, w_pool, pool_scale, tm, seq):
    T = proj.shape[0]
    hb = tm // POOL_HALO
    return pl.pallas_call(
        functools.partial(_pool_kernel, tm=tm, seq=seq),
        grid=(T // tm,),
        in_specs=[
            pl.BlockSpec((POOL_HALO, D_MODEL),
                         lambda i: (jnp.maximum(i * hb - 1, 0), COL_P // D_MODEL)),
            pl.BlockSpec((tm, D_MODEL), lambda i: (i, COL_P // D_MODEL)),
            pl.BlockSpec((tm, D_MODEL), lambda i: (i, COL_GP // D_MODEL)),
            pl.BlockSpec((POOL_GROUPS, POOL_GW, POOL_GW), lambda i: (0, 0, 0)),
            pl.BlockSpec((1, D_MODEL), lambda i: (0, 0)),
        ],
        out_specs=pl.BlockSpec((tm, D_MODEL), lambda i: (i, 0)),
        out_shape=jax.ShapeDtypeStruct((T, D_MODEL), BF16),
        scratch_shapes=[pltpu.VMEM((POOL_HALO + tm, POOL_GW), F32)],
        compiler_params=_cparams(("parallel",)),
        name="pool",
    )(proj, proj, proj, w_pool, pool_scale)


def _split3(v):
    hi = v.astype(BF16)
    r1 = v - hi.astype(F32)
    mid = r1.astype(BF16)
    lo = (r1 - mid.astype(F32)).astype(BF16)
    return hi, mid, lo


def _ssd_kernel(z_ref, xs_ref, bc_ref, dt_ref, cwx_ref, cbx_ref, cwbc_ref, cbbc_ref,
                dtb_ref, alog_ref, dsk_ref, nw_ref, qm_ref, sh_ref, o_ref,
                prev_x, prev_bc, xs_scr, b_scr, c_scr, a_scr, qt_scr, wt_scr, state):
    L = SSD_CHUNK
    c = pl.program_id(1)

    @pl.when(c == 0)
    def _():
        prev_x[...] = jnp.zeros_like(prev_x)
        prev_bc[...] = jnp.zeros_like(prev_bc)
        state[...] = jnp.zeros_like(state)

    def conv_slab(src_ref, prev_ref, cw_ref, cb_ref, col):
        cols = slice(col, col + CONV_SLAB)
        raw = src_ref[:, cols]
        both = jnp.concatenate([prev_ref[:, cols], raw], axis=0)
        prev_ref[:, cols] = raw
        acc = cb_ref[:, cols] + cw_ref[SSD_CONV - 1:SSD_CONV, cols] * raw.astype(F32)
        for k in range(SSD_CONV - 1):
            shifted = jnp.dot(sh_ref[k * L:(k + 1) * L, :], both, preferred_element_type=F32)
            acc = acc + cw_ref[k:k + 1, cols] * shifted
        return acc * jax.nn.sigmoid(acc)

    for s in range(SSD_INNER // CONV_SLAB):
        xs_scr[:, s * CONV_SLAB:(s + 1) * CONV_SLAB] = conv_slab(xs_ref, prev_x, cwx_ref, cbx_ref, s * CONV_SLAB)
    for s in range(SSD_BC // CONV_SLAB):
        b_scr[:, s * CONV_SLAB:(s + 1) * CONV_SLAB] = conv_slab(bc_ref, prev_bc, cwbc_ref, cbbc_ref, s * CONV_SLAB)
    for s in range(SSD_BC // CONV_SLAB):
        c_scr[:, s * CONV_SLAB:(s + 1) * CONV_SLAB] = conv_slab(
            bc_ref, prev_bc, cwbc_ref, cbbc_ref, SSD_BC + s * CONV_SLAB).astype(BF16)

    dtv = jax.nn.softplus(dt_ref[...] + dtb_ref[...])
    d_a = dtv * (-jnp.exp(alog_ref[...]))
    rr = lax.broadcasted_iota(jnp.int32, (L, L), 0)
    cc = lax.broadcasted_iota(jnp.int32, (L, L), 1)
    causal = rr >= cc
    low_half = cc < SSD_HEADDIM
    tri = jnp.where(causal, 1.0, 0.0).astype(BF16)
    a_cum = sum(jnp.dot(tri, part, preferred_element_type=F32) for part in _split3(d_a))
    a_t = a_cum.T
    dt_t = dtv.T
    a_scr[...] = a_cum
    qt_scr[...] = a_t - jnp.log(dt_t)
    wt_scr[...] = dt_t * jnp.exp(a_t[:, L - 1:L] - a_t)

    def group_body(g, carry):
        off = pl.multiple_of(g * SSD_GW, SSD_GW)
        offn = pl.multiple_of(g * SSD_STATE, SSD_STATE)
        xg = xs_scr[:, pl.ds(off, SSD_GW)]
        xg_b = xg.astype(BF16)
        b_f = b_scr[:, pl.ds(offn, SSD_STATE)]
        c_b = c_scr[:, pl.ds(offn, SSD_STATE)]
        b_t = b_f.T
        cbm = lax.dot_general(c_b, b_f.astype(BF16), (((1,), (1,)), ((), ())),
                              preferred_element_type=F32)
        s_g = state[g]
        y_off = jnp.dot(c_b, s_g.astype(BF16), preferred_element_type=F32)
        a_g = pltpu.roll(a_scr[...], (LANES - SSD_HPG * g) % LANES, 1)
        ys = []
        for quad in range(SSD_HPG // QUAD):
            qs = slice(quad * QUAD_W, (quad + 1) * QUAD_W)
            m_parts, b_parts, a_cols = [], [], []
            for jj in range(QUAD):
                j = quad * QUAD + jj
                h = g * SSD_HPG + j
                a_col = jnp.broadcast_to(a_g[:, j:j + 1], (L, L))
                seg = a_col - qt_scr[pl.ds(h, 1), :]
                m_parts.append((cbm * jnp.exp(jnp.where(causal, seg, NEG_BIG))).astype(BF16))
                b_parts.append((b_t * wt_scr[pl.ds(h, 1), :]).astype(BF16))
                a_cols.append(a_col)
            lhs = jnp.concatenate([jnp.concatenate(m_parts, axis=1), jnp.concatenate(b_parts, axis=1)], axis=0)
            xq = xg_b[:, qs]
            rhs = jnp.concatenate([xq * qm_ref[jj] for jj in range(QUAD)], axis=0)
            res = jnp.dot(lhs, rhs, preferred_element_type=F32)
            ea_q = jnp.concatenate(
                [jnp.exp(jnp.where(low_half, a_cols[2 * p], a_cols[2 * p + 1])) for p in range(QUAD // 2)], axis=1)
            ys.append(res[:L] + y_off[:, qs] * ea_q)
            state[g, :, qs] = s_g[:, qs] * ea_q[L - 1:L, :] + res[L:]
        y = jnp.concatenate(ys, axis=1) + dsk_ref[:, pl.ds(off, SSD_GW)] * xg
        zg = z_ref[:, pl.ds(off, SSD_GW)].astype(F32)
        yg = y * (zg * jax.nn.sigmoid(zg))
        ms = jnp.mean(yg * yg, axis=-1, keepdims=True)
        yn = yg * lax.rsqrt(ms + EPS) * nw_ref[:, pl.ds(off, SSD_GW)]
        o_ref[:, pl.ds(off, SSD_GW)] = yn.astype(o_ref.dtype)
        return carry

    lax.fori_loop(0, SSD_GROUPS, group_body, 0)


def _ssd(proj, dt_raw, cwx, cbx, cwbc, cbbc, dtb, alog, dsk, nw, batch, seq):
    L = SSD_CHUNK
    nc = seq // L
    T = proj.shape[0]
    row = lambda b, c: b * nc + c
    const = lambda b, c: (0, 0)
    head_of_lane = jnp.arange(QUAD_W, dtype=jnp.int32) // SSD_HEADDIM
    qmask = (head_of_lane[None, None, :] == jnp.arange(QUAD, dtype=jnp.int32)[:, None, None])
    qmask = jnp.broadcast_to(qmask, (QUAD, L, QUAD_W)).astype(BF16)
    tap = jnp.arange(SSD_CONV - 1, dtype=jnp.int32)[:, None, None]
    t_out = jnp.arange(L, dtype=jnp.int32)[None, :, None]
    src = jnp.arange(2 * L, dtype=jnp.int32)[None, None, :]
    shift = (src == L + t_out - (SSD_CONV - 1) + tap).astype(BF16).reshape((SSD_CONV - 1) * L, 2 * L)
    return pl.pallas_call(
        _ssd_kernel,
        grid=(batch, nc),
        in_specs=[
            pl.BlockSpec((L, SSD_INNER), lambda b, c: (row(b, c), COL_Z // SSD_INNER)),
            pl.BlockSpec((L, SSD_INNER), lambda b, c: (row(b, c), COL_XS // SSD_INNER)),
            pl.BlockSpec((L, 2 * SSD_BC), lambda b, c: (row(b, c), COL_BC // (2 * SSD_BC))),
            pl.BlockSpec((L, LANES), lambda b, c: (row(b, c), 0)),
            pl.BlockSpec((SSD_CONV, SSD_INNER), const),
            pl.BlockSpec((1, SSD_INNER), const),
            pl.BlockSpec((SSD_CONV, 2 * SSD_BC), const),
            pl.BlockSpec((1, 2 * SSD_BC), const),
            pl.BlockSpec((1, LANES), const),
            pl.BlockSpec((1, LANES), const),
            pl.BlockSpec((1, SSD_INNER), const),
            pl.BlockSpec((1, SSD_INNER), const),
            pl.BlockSpec((QUAD, L, QUAD_W), lambda b, c: (0, 0, 0)),
            pl.BlockSpec(((SSD_CONV - 1) * L, 2 * L), const),
        ],
        out_specs=pl.BlockSpec((L, SSD_INNER), lambda b, c: (row(b, c), 0)),
        out_shape=jax.ShapeDtypeStruct((T, SSD_INNER), BF16),
        scratch_shapes=[
            pltpu.VMEM((L, SSD_INNER), BF16),
            pltpu.VMEM((L, 2 * SSD_BC), BF16),
            pltpu.VMEM((L, SSD_INNER), F32),
            pltpu.VMEM((L, SSD_BC), F32),
            pltpu.VMEM((L, SSD_BC), BF16),
            pltpu.VMEM((L, LANES), F32),
            pltpu.VMEM((LANES, L), F32),
            pltpu.VMEM((LANES, L), F32),
            pltpu.VMEM((SSD_GROUPS, SSD_STATE, SSD_GW), F32),
        ],
        compiler_params=_cparams(("arbitrary", "arbitrary")),
        name="ssd",
    )(proj, proj, proj, dt_raw, cwx, cbx, cwbc, cbbc, dtb, alog, dsk, nw, qmask, shift)


def _pack_bf16_pairs(v):
    bits = pltpu.bitcast(v.astype(BF16).astype(F32), jnp.uint32)
    return (bits[:, :HALF] >> 16) | (bits[:, HALF:] & jnp.uint32(0xFFFF0000))


def _unpack_lo(w):
    return pltpu.bitcast(w << 16, F32)


def _unpack_hi(w):
    return pltpu.bitcast(w & jnp.uint32(0xFFFF0000), F32)


def _mixout_kernel(pg_ref, yn_ref, gs_ref, x_ref, wso_ref, wo_ref, nw_ref, rwh_ref, rwl_ref, rb_ref,
                   h_ref, u_ref, lg_ref):
    ssd = jnp.dot(yn_ref[...], wso_ref[...], preferred_element_type=F32)
    merged = pg_ref[...].astype(F32) + jax.nn.sigmoid(gs_ref[...].astype(F32)) * ssd
    h = x_ref[...] + jnp.dot(merged.astype(BF16), wo_ref[...], preferred_element_type=F32)
    h_ref[...] = h
    ms = jnp.mean(h * h, axis=-1, keepdims=True)
    u = h * lax.rsqrt(ms + EPS) * nw_ref[...]
    u_hi = u.astype(BF16)
    u_ref[...] = _pack_bf16_pairs(u)
    u_lo = (u - u_hi.astype(F32)).astype(BF16)
    rwh = rwh_ref[...]
    lg = (jnp.dot(u_hi, rwh, preferred_element_type=F32)
          + jnp.dot(u_lo, rwh, preferred_element_type=F32)
          + jnp.dot(u_hi, rwl_ref[...], preferred_element_type=F32))
    lg_ref[...] = lg + rb_ref[...]


def _mixout(pool_g, yn, proj, x2, wso, wo, nw, rwh, rwl, rb, tm):
    T = x2.shape[0]
    const = lambda i: (0, 0)
    return pl.pallas_call(
        _mixout_kernel,
        grid=(T // tm,),
        in_specs=[
            pl.BlockSpec((tm, D_MODEL), lambda i: (i, 0)),
            pl.BlockSpec((tm, SSD_INNER), lambda i: (i, 0)),
            pl.BlockSpec((tm, D_MODEL), lambda i: (i, COL_GS // D_MODEL)),
            pl.BlockSpec((tm, D_MODEL), lambda i: (i, 0)),
            pl.BlockSpec((SSD_INNER, D_MODEL), const),
            pl.BlockSpec((D_MODEL, D_MODEL), const),
            pl.BlockSpec((1, D_MODEL), const),
            pl.BlockSpec((D_MODEL, LANES), const),
            pl.BlockSpec((D_MODEL, LANES), const),
            pl.BlockSpec((1, LANES), const),
        ],
        out_specs=[
            pl.BlockSpec((tm, D_MODEL), lambda i: (i, 0)),
            pl.BlockSpec((tm, HALF), lambda i: (i, 0)),
            pl.BlockSpec((tm, LANES), lambda i: (i, 0)),
        ],
        out_shape=[
            jax.ShapeDtypeStruct((T, D_MODEL), F32),
            jax.ShapeDtypeStruct((T, HALF), jnp.uint32),
            jax.ShapeDtypeStruct((T, LANES), F32),
        ],
        compiler_params=_cparams(("parallel",)),
        name="mixout",
    )(pool_g, yn, proj, x2, wso, wo, nw, rwh, rwl, rb)


def _router_kernel(lg_ref, er_ref, gt_ref, cnt_ref, run_scr, *, tm):
    @pl.when(pl.program_id(0) == 0)
    def _():
        run_scr[...] = jnp.zeros_like(run_scr)

    lane = lax.broadcasted_iota(jnp.int32, (tm, LANES), 1)
    lg = jnp.where(lane < N_EXPERTS, lg_ref[...], -jnp.inf)
    idxs, vals, hots = [], [], []
    for _ in range(TOP_K):
        m = jnp.max(lg, axis=-1, keepdims=True)
        idx = jnp.min(jnp.where(lg == m, lane, LANES), axis=-1, keepdims=True)
        hot = lane == idx
        lg = jnp.where(hot, -jnp.inf, lg)
        idxs.append(idx)
        vals.append(m)
        hots.append(hot)
    exps = [jnp.exp(v - vals[0]) for v in vals]
    den = exps[0] + exps[1] + exps[2] + exps[3]
    cnt = sum(jnp.where(hot, 1.0, 0.0) for hot in hots)
    rr = lax.broadcasted_iota(jnp.int32, (tm, tm), 0)
    cc = lax.broadcasted_iota(jnp.int32, (tm, tm), 1)
    before = jnp.where(rr > cc, 1.0, 0.0).astype(BF16)
    base = jnp.dot(before, cnt.astype(BF16), preferred_element_type=F32) + run_scr[...]
    er = jnp.zeros((tm, LANES), jnp.int32)
    gt = jnp.zeros((tm, LANES), F32)
    for k in range(TOP_K):
        rank = jnp.sum(jnp.where(hots[k], base, 0.0), axis=-1, keepdims=True)
        er = jnp.where(lane == k, idxs[k], er)
        er = jnp.where(lane == TOP_K + k, rank.astype(jnp.int32), er)
        gt = jnp.where(lane == k, exps[k] / den, gt)
    er_ref[...] = er
    gt_ref[...] = gt
    run_scr[...] += jnp.sum(cnt, axis=0, keepdims=True)
    cnt_ref[...] = run_scr[...]


def _router(logits, tm):
    T = logits.shape[0]
    return pl.pallas_call(
        functools.partial(_router_kernel, tm=tm),
        grid=(T // tm,),
        in_specs=[pl.BlockSpec((tm, LANES), lambda i: (i, 0))],
        out_specs=[
            pl.BlockSpec((tm, LANES), lambda i: (i, 0)),
            pl.BlockSpec((tm, LANES), lambda i: (i, 0)),
            pl.BlockSpec((1, LANES), lambda i: (0, 0)),
        ],
        out_shape=[
            jax.ShapeDtypeStruct((T, LANES), jnp.int32),
            jax.ShapeDtypeStruct((T, LANES), F32),
            jax.ShapeDtypeStruct((1, LANES), F32),
        ],
        scratch_shapes=[pltpu.VMEM((1, LANES), F32)],
        compiler_params=_cparams(("arbitrary",)),
        name="router",
    )(logits)


def _dispatch_kernel(ps_ref, e_ref, r_ref, u_ref, xb_ref, dest_ref, sem, *, tm):
    def body(g, carry):
        r0 = pl.multiple_of(g * ROW_GROUP, ROW_GROUP)
        for i in range(ROW_GROUP):
            for k in range(TOP_K):
                j = (r0 + i) * TOP_K + k
                d = ps_ref[e_ref[j]] + r_ref[j]
                dest_ref[j] = d
                pltpu.make_async_copy(u_ref.at[pl.ds(r0 + i, 1)], xb_ref.at[pl.ds(d, 1)], sem).start()
        return carry

    lax.fori_loop(0, tm // ROW_GROUP, body, 0)
    for _ in range(TOP_K):
        pltpu.make_async_copy(u_ref, xb_ref.at[pl.ds(0, tm)], sem).wait()


def _dispatch(pstart, e_flat, r_flat, u_pk, n_slots, tm):
    T = u_pk.shape[0]
    grid_spec = pltpu.PrefetchScalarGridSpec(
        num_scalar_prefetch=1,
        grid=(T // tm,),
        in_specs=[
            pl.BlockSpec((tm * TOP_K,), lambda i, ps: (i,), memory_space=pltpu.SMEM),
            pl.BlockSpec((tm * TOP_K,), lambda i, ps: (i,), memory_space=pltpu.SMEM),
            pl.BlockSpec((tm, HALF), lambda i, ps: (i, 0)),
        ],
        out_specs=[
            pl.BlockSpec(memory_space=pl.ANY),
            pl.BlockSpec((tm * TOP_K,), lambda i, ps: (i,), memory_space=pltpu.SMEM),
        ],
        scratch_shapes=[pltpu.SemaphoreType.DMA(())],
    )
    return pl.pallas_call(
        functools.partial(_dispatch_kernel, tm=tm),
        grid_spec=grid_spec,
        out_shape=[
            jax.ShapeDtypeStruct((n_slots, HALF), jnp.uint32),
            jax.ShapeDtypeStruct((T * TOP_K,), jnp.int32),
        ],
        compiler_params=_cparams(("arbitrary",)),
        name="dispatch",
    )(pstart, e_flat, r_flat, u_pk)


def _row_blocks(nrows, tm, sub, weights, block):
    nblk = (nrows + sub - 1) // sub
    for p in range(tm // (2 * sub)):
        @pl.when(nblk >= 2 * p + 2)
        def _():
            block(2 * p * sub, 2 * sub, weights())

        @pl.when(nblk == 2 * p + 1)
        def _():
            block(2 * p * sub, sub, weights())

    if tm % (2 * sub):
        @pl.when(nblk * sub == tm)
        def _():
            block(tm - sub, sub, weights())


def _ffn_up_kernel(te_ref, nu_ref, tr_ref, x_ref, wg_ref, wl_ref, bg_ref, bl_ref, o_ref, xs, *, tm, sub):
    nrows = tr_ref[pl.program_id(0)]

    @pl.when(jnp.logical_and(nrows > 0, pl.program_id(1) == 0))
    def _():
        w = x_ref[...]
        xs[:, :HALF] = _unpack_lo(w).astype(BF16)
        xs[:, HALF:] = _unpack_hi(w).astype(BF16)

    def weights():
        return wg_ref[0].astype(BF16), wl_ref[0].astype(BF16)

    def block(r0, nr, w):
        x = xs[r0:r0 + nr, :]
        glu = jnp.minimum(jnp.dot(x, w[0], preferred_element_type=F32) + bg_ref[0], SWIGLU_LIMIT)
        lin = jnp.clip(jnp.dot(x, w[1], preferred_element_type=F32) + bl_ref[0], -SWIGLU_LIMIT, SWIGLU_LIMIT)
        act = glu * jax.nn.sigmoid(SWIGLU_ALPHA * glu) * (lin + 1.0)
        o_ref[r0:r0 + nr, :] = act.astype(o_ref.dtype)

    _row_blocks(nrows, tm, sub, weights, block)


def _ffn_down_kernel(te_ref, nu_ref, tr_ref, a_ref, wa_ref, wb_ref, ba_ref, bb_ref, o_ref, *, tm, sub):
    nrows = tr_ref[pl.program_id(0)]

    def weights():
        return wa_ref[0].astype(BF16), wb_ref[0].astype(BF16)

    def block(r0, nr, w):
        a = a_ref[r0:r0 + nr, :]
        ya = jnp.dot(a, w[0], preferred_element_type=F32) + ba_ref[0]
        yb = jnp.dot(a, w[1], preferred_element_type=F32) + bb_ref[0]
        lo = pltpu.bitcast(ya.astype(BF16).astype(F32), jnp.uint32) >> 16
        hi = pltpu.bitcast(yb.astype(BF16).astype(F32), jnp.uint32) & jnp.uint32(0xFFFF0000)
        o_ref[r0:r0 + nr, :] = lo | hi

    _row_blocks(nrows, tm, sub, weights, block)


def _ffn(tile_expert, n_used, tile_rows, xb, w_up, b_up3, w_down, b_down3, tm, sub, fc, nc):
    n_slots = xb.shape[0]
    n_tiles = n_slots // tm
    nf = D_FF // fc
    nj = HALF // nc

    def live(i, nu):
        return jnp.minimum(i, nu[0] - 1)

    def frozen(i, j, nu, last):
        return jnp.where(i < nu[0], j, last)

    up_spec = pltpu.PrefetchScalarGridSpec(
        num_scalar_prefetch=3,
        grid=(n_tiles, nf),
        in_specs=[
            pl.BlockSpec((tm, HALF), lambda i, f, te, nu, tr: (live(i, nu), 0)),
            pl.BlockSpec((1, D_MODEL, fc), lambda i, f, te, nu, tr: (te[i], 0, frozen(i, f, nu, nf - 1))),
            pl.BlockSpec((1, D_MODEL, fc), lambda i, f, te, nu, tr: (te[i], 0, nf + frozen(i, f, nu, nf - 1))),
            pl.BlockSpec((1, 1, fc), lambda i, f, te, nu, tr: (te[i], 0, frozen(i, f, nu, nf - 1))),
            pl.BlockSpec((1, 1, fc), lambda i, f, te, nu, tr: (te[i], 0, nf + frozen(i, f, nu, nf - 1))),
        ],
        out_specs=pl.BlockSpec((tm, fc), lambda i, f, te, nu, tr: (live(i, nu), frozen(i, f, nu, nf - 1))),
        scratch_shapes=[pltpu.VMEM((tm, D_MODEL), BF16)],
    )
    act = pl.pallas_call(
        functools.partial(_ffn_up_kernel, tm=tm, sub=sub),
        grid_spec=up_spec,
        out_shape=jax.ShapeDtypeStruct((n_slots, D_FF), BF16),
        compiler_params=_cparams(("arbitrary", "arbitrary")),
        name="ffn_up",
    )(tile_expert, n_used, tile_rows, xb, w_up, w_up, b_up3, b_up3)

    down_spec = pltpu.PrefetchScalarGridSpec(
        num_scalar_prefetch=3,
        grid=(n_tiles, nj),
        in_specs=[
            pl.BlockSpec((tm, D_FF), lambda i, j, te, nu, tr: (live(i, nu), 0)),
            pl.BlockSpec((1, D_FF, nc), lambda i, j, te, nu, tr: (te[i], 0, frozen(i, j, nu, nj - 1))),
            pl.BlockSpec((1, D_FF, nc), lambda i, j, te, nu, tr: (te[i], 0, nj + frozen(i, j, nu, nj - 1))),
            pl.BlockSpec((1, 1, nc), lambda i, j, te, nu, tr: (te[i], 0, frozen(i, j, nu, nj - 1))),
            pl.BlockSpec((1, 1, nc), lambda i, j, te, nu, tr: (te[i], 0, nj + frozen(i, j, nu, nj - 1))),
        ],
        out_specs=pl.BlockSpec((tm, nc), lambda i, j, te, nu, tr: (live(i, nu), frozen(i, j, nu, nj - 1))),
    )
    return pl.pallas_call(
        functools.partial(_ffn_down_kernel, tm=tm, sub=sub),
        grid_spec=down_spec,
        out_shape=jax.ShapeDtypeStruct((n_slots, HALF), jnp.uint32),
        compiler_params=_cparams(("arbitrary", "arbitrary")),
        name="ffn_down",
    )(tile_expert, n_used, tile_rows, act, w_down, w_down, b_down3, b_down3)


def _combine_kernel(d_ref, h_ref, g_ref, nw_ref, yb_ref, o_ref, ybuf, sem, *, tm):
    def body(g, carry):
        r0 = pl.multiple_of(g * ROW_GROUP, ROW_GROUP)
        for i in range(ROW_GROUP):
            for k in range(TOP_K):
                d = d_ref[(r0 + i) * TOP_K + k]
                pltpu.make_async_copy(yb_ref.at[pl.ds(d, 1)], ybuf.at[k, pl.ds(r0 + i, 1)], sem).start()
        return carry

    lax.fori_loop(0, tm // ROW_GROUP, body, 0)
    for k in range(TOP_K):
        pltpu.make_async_copy(yb_ref.at[pl.ds(0, tm)], ybuf.at[k], sem).wait()
    lo = h_ref[:, :HALF]
    hi = h_ref[:, HALF:]
    for k in range(TOP_K):
        w = ybuf[k]
        g = g_ref[:, k:k + 1]
        lo = lo + g * _unpack_lo(w)
        hi = hi + g * _unpack_hi(w)
    ms = (jnp.sum(lo * lo, axis=-1, keepdims=True) + jnp.sum(hi * hi, axis=-1, keepdims=True)) / D_MODEL
    scale = lax.rsqrt(ms + EPS)
    o_ref[:, :HALF] = lo * scale * nw_ref[:, :HALF]
    o_ref[:, HALF:] = hi * scale * nw_ref[:, HALF:]


def _combine(dest, h, gates, nw, yb, tm):
    T = h.shape[0]
    return pl.pallas_call(
        functools.partial(_combine_kernel, tm=tm),
        grid=(T // tm,),
        in_specs=[
            pl.BlockSpec((tm * TOP_K,), lambda i: (i,), memory_space=pltpu.SMEM),
            pl.BlockSpec((tm, D_MODEL), lambda i: (i, 0)),
            pl.BlockSpec((tm, LANES), lambda i: (i, 0)),
            pl.BlockSpec((1, D_MODEL), lambda i: (0, 0)),
            pl.BlockSpec(memory_space=pl.ANY),
        ],
        out_specs=pl.BlockSpec((tm, D_MODEL), lambda i: (i, 0)),
        out_shape=jax.ShapeDtypeStruct((T, D_MODEL), F32),
        scratch_shapes=[pltpu.VMEM((TOP_K, tm, HALF), jnp.uint32), pltpu.SemaphoreType.DMA(())],
        compiler_params=_cparams(("arbitrary",)),
        name="combine",
    )(dest, h, gates, nw, yb)


def _row(v, width=None):
    v = v.astype(F32).reshape(1, -1)
    if width is not None and v.shape[1] < width:
        v = jnp.pad(v, ((0, 0), (0, width - v.shape[1])))
    return v


def _tile(n, pref):
    t = pref
    while n % t:
        t //= 2
    return t


def kernel(x, norm_mix_w, w_in, w_pool, pool_scale, conv_w, conv_b, dt_bias, a_log, d_skip,
           ssd_norm_w, w_ssd_out, w_out, norm_ffn_w, router_w, router_b, w_up, b_up,
           w_down, b_down, norm_final_w):
    B, S, D = x.shape
    T = B * S
    x2 = x.reshape(T, D)

    o_p, o_z, o_xbc = 0, D_MODEL, D_MODEL + SSD_INNER
    o_dt = o_xbc + SSD_INNER + 2 * SSD_BC
    o_gp = o_dt + SSD_HEADS
    o_gs = o_gp + D_MODEL
    w_main = jnp.concatenate(
        [w_in[:, o_z:o_z + SSD_INNER], w_in[:, o_xbc:o_dt], w_in[:, o_p:o_p + D_MODEL],
         w_in[:, o_gp:o_gp + D_MODEL], w_in[:, o_gs:o_gs + D_MODEL]], axis=1).astype(BF16)
    w_dt = jnp.pad(w_in[:, o_dt:o_dt + SSD_HEADS], ((0, 0), (0, LANES - SSD_HEADS))).astype(BF16)

    proj, dt_raw = _inproj(x2, _row(norm_mix_w), w_main, w_dt, _tile(T, 1024), 2048)

    pool_g = _pool(proj, w_pool.astype(BF16), _row(pool_scale), _tile(S, 256), S)

    yn = _ssd(proj, dt_raw,
              conv_w[:, :SSD_INNER].astype(F32), _row(conv_b[:SSD_INNER]),
              conv_w[:, SSD_INNER:].astype(F32), _row(conv_b[SSD_INNER:]),
              _row(dt_bias, LANES), _row(a_log, LANES),
              _row(jnp.repeat(d_skip, SSD_HEADDIM)), _row(ssd_norm_w), B, S)

    rw = jnp.pad(router_w.astype(F32), ((0, 0), (0, LANES - N_EXPERTS)))
    rw_hi = rw.astype(BF16)
    rw_lo = (rw - rw_hi.astype(F32)).astype(BF16)
    h, u_pk, logits = _mixout(pool_g, yn, proj, x2, w_ssd_out.astype(BF16), w_out.astype(BF16),
                              _row(norm_ffn_w), rw_hi, rw_lo, _row(router_b, LANES), _tile(T, 256))

    er, gates, cnt = _router(logits, _tile(T, 256))

    TK = T * TOP_K
    tm_ffn = 1280 if TK >= 32768 else 256
    sub = 256
    n_tiles = -(-TK // tm_ffn) + N_EXPERTS
    counts = cnt[0, :N_EXPERTS].astype(jnp.int32)
    tiles_e = (counts + tm_ffn - 1) // tm_ffn
    tend = jnp.cumsum(tiles_e)
    tstart = tend - tiles_e
    pstart = (tstart * tm_ffn).astype(jnp.int32)
    tile_id = jnp.arange(n_tiles, dtype=jnp.int32)
    tile_expert = jnp.minimum(jnp.searchsorted(tend, tile_id, side='right'), N_EXPERTS - 1).astype(jnp.int32)
    tile_rows = jnp.clip(counts[tile_expert] - (tile_id - tstart[tile_expert]) * tm_ffn, 0, tm_ffn).astype(jnp.int32)
    n_used = tend[-1].astype(jnp.int32).reshape(1)
    e_flat = er[:, :TOP_K].reshape(TK)
    r_flat = er[:, TOP_K:2 * TOP_K].reshape(TK)

    xb, dest = _dispatch(pstart, e_flat, r_flat, u_pk, n_tiles * tm_ffn, _tile(T, 512))
    yb = _ffn(tile_expert, n_used, tile_rows, xb, w_up, b_up.reshape(N_EXPERTS, 1, 2 * D_FF),
              w_down, b_down.reshape(N_EXPERTS, 1, D_MODEL), tm_ffn, sub, 512, 512)
    out = _combine(dest, h, gates, _row(norm_final_w), yb, _tile(T, 256))
    return out.reshape(B, S, D)
```

```python
import functools

import jax
import jax.numpy as jnp
from jax import lax
from jax.experimental import pallas as pl
from jax.experimental.pallas import tpu as pltpu

F32 = jnp.float32
BF16 = jnp.bfloat16

D_MODEL = 2048
EPS = 1e-5
POOL_GROUPS = 4
POOL_WINDOWS = (2, 4, 8, 16)
POOL_GW = D_MODEL // POOL_GROUPS
POOL_HALO = 16
SSD_INNER = 2 * D_MODEL
SSD_HEADDIM = 64
SSD_HEADS = SSD_INNER // SSD_HEADDIM
SSD_GROUPS = 8
SSD_HPG = SSD_HEADS // SSD_GROUPS
SSD_STATE = 128
SSD_CONV = 4
SSD_CHUNK = 128
SSD_GW = SSD_HPG * SSD_HEADDIM
SSD_BC = SSD_GROUPS * SSD_STATE
N_EXPERTS = 32
TOP_K = 4
D_FF = D_MODEL
SWIGLU_ALPHA = 1.702
SWIGLU_LIMIT = 7.0

LANES = 128
HALF = D_MODEL // 2
CONV_SLAB = 256
QUAD = 4
QUAD_W = QUAD * SSD_HEADDIM
FFN_BIG_BLOCK = 1024
ROW_GROUP = 8
VMEM_LIMIT = 56 * 1024 * 1024

COL_Z = 0
COL_XS = SSD_INNER
COL_BC = 2 * SSD_INNER
COL_P = 2 * SSD_INNER + 2 * SSD_BC
COL_GP = COL_P + D_MODEL
COL_GS = COL_GP + D_MODEL
PROJ_COLS = COL_GS + D_MODEL

NEG_BIG = -1e30


def _cparams(sem):
    return pltpu.CompilerParams(dimension_semantics=sem, vmem_limit_bytes=VMEM_LIMIT)


def _inproj_kernel(x_ref, nw_ref, w_ref, wdt_ref, o_ref, dt_ref, u_scr):
    @pl.when(pl.program_id(1) == 0)
    def _():
        x = x_ref[...]
        ms = jnp.mean(x * x, axis=-1, keepdims=True)
        u = (x * lax.rsqrt(ms + EPS) * nw_ref[...]).astype(BF16)
        u_scr[...] = u
        dt_ref[...] = jnp.dot(u, wdt_ref[...], preferred_element_type=F32)

    o_ref[...] = jnp.dot(u_scr[...], w_ref[...], preferred_element_type=F32).astype(o_ref.dtype)


def _inproj(x2, norm_w, w_main, w_dt, tm, tn):
    T = x2.shape[0]
    return pl.pallas_call(
        _inproj_kernel,
        grid=(T // tm, PROJ_COLS // tn),
        in_specs=[
            pl.BlockSpec((tm, D_MODEL), lambda i, j: (i, 0)),
            pl.BlockSpec((1, D_MODEL), lambda i, j: (0, 0)),
            pl.BlockSpec((D_MODEL, tn), lambda i, j: (0, j)),
            pl.BlockSpec((D_MODEL, LANES), lambda i, j: (0, 0)),
        ],
        out_specs=[
            pl.BlockSpec((tm, tn), lambda i, j: (i, j)),
            pl.BlockSpec((tm, LANES), lambda i, j: (i, 0)),
        ],
        out_shape=[
            jax.ShapeDtypeStruct((T, PROJ_COLS), BF16),
            jax.ShapeDtypeStruct((T, LANES), F32),
        ],
        scratch_shapes=[pltpu.VMEM((tm, D_MODEL), BF16)],
        compiler_params=_cparams(("parallel", "arbitrary")),
        name="inproj",
    )(x2, norm_w, w_main, w_dt)


def _pool_kernel(halo_ref, p_ref, g_ref, wp_ref, ps_ref, o_ref, buf, *, tm, seq):
    row0 = (pl.program_id(0) * tm) % seq
    halo = jnp.where(row0 == 0, 0.0, halo_ref[...].astype(F32))
    pos = row0 + lax.broadcasted_iota(jnp.int32, (tm, 1), 0)
    for g, w in enumerate(POOL_WINDOWS):
        cols = slice(g * POOL_GW, (g + 1) * POOL_GW)
        buf[0:POOL_HALO, :] = halo[:, cols]
        buf[POOL_HALO:, :] = p_ref[:, cols].astype(F32)
        cur = buf[POOL_HALO:POOL_HALO + tm, :]
        acc = cur
        for k in range(1, w):
            acc = acc + buf[POOL_HALO - k:POOL_HALO - k + tm, :]
        cnt = jnp.minimum(pos + 1, w).astype(F32)
        pooled = acc / cnt - cur
        y = jnp.dot(pooled.astype(BF16), wp_ref[g], preferred_element_type=F32)
        y = y * ps_ref[:, cols] * jax.nn.sigmoid(g_ref[:, cols].astype(F32))
        o_ref[:, cols] = y.astype(o_ref.dtype)


def _pool(proj, w_pool, pool_scale, tm, seq):
    T = proj.shape[0]
    hb = tm // POOL_HALO
    return pl.pallas_call(
        functools.partial(_pool_kernel, tm=tm, seq=seq),
        grid=(T // tm,),
        in_specs=[
            pl.BlockSpec((POOL_HALO, D_MODEL),
                         lambda i: (jnp.maximum(i * hb - 1, 0), COL_P // D_MODEL)),
            pl.BlockSpec((tm, D_MODEL), lambda i: (i, COL_P // D_MODEL)),
            pl.BlockSpec((tm, D_MODEL), lambda i: (i, COL_GP // D_MODEL)),
            pl.BlockSpec((POOL_GROUPS, POOL_GW, POOL_GW), lambda i: (0, 0, 0)),
            pl.BlockSpec((1, D_MODEL), lambda i: (0, 0)),
        ],
        out_specs=pl.BlockSpec((tm, D_MODEL), lambda i: (i, 0)),
        out_shape=jax.ShapeDtypeStruct((T, D_MODEL), BF16),
        scratch_shapes=[pltpu.VMEM((POOL_HALO + tm, POOL_GW), F32)],
        compiler_params=_cparams(("parallel",)),
        name="pool",
    )(proj, proj, proj, w_pool, pool_scale)


def _split3(v):
    hi = v.astype(BF16)
    r1 = v - hi.astype(F32)
    mid = r1.astype(BF16)
    lo = (r1 - mid.astype(F32)).astype(BF16)
    return hi, mid, lo


def _ssd_kernel(z_ref, xs_ref, bc_ref, dt_ref, cwx_ref, cbx_ref, cwbc_ref, cbbc_ref,
                dtb_ref, alog_ref, dsk_ref, nw_ref, qm_ref, sh_ref, o_ref,
                prev_x, prev_bc, xs_scr, b_scr, c_scr, a_scr, qt_scr, wt_scr, state):
    L = SSD_CHUNK
    c = pl.program_id(1)

    @pl.when(c == 0)
    def _():
        prev_x[...] = jnp.zeros_like(prev_x)
        prev_bc[...] = jnp.zeros_like(prev_bc)
        state[...] = jnp.zeros_like(state)

    def conv_slab(src_ref, prev_ref, cw_ref, cb_ref, col):
        cols = slice(col, col + CONV_SLAB)
        raw = src_ref[:, cols]
        both = jnp.concatenate([prev_ref[:, cols], raw], axis=0)
        prev_ref[:, cols] = raw
        acc = cb_ref[:, cols] + cw_ref[SSD_CONV - 1:SSD_CONV, cols] * raw.astype(F32)
        for k in range(SSD_CONV - 1):
            shifted = jnp.dot(sh_ref[k * L:(k + 1) * L, :], both, preferred_element_type=F32)
            acc = acc + cw_ref[k:k + 1, cols] * shifted
        return acc * jax.nn.sigmoid(acc)

    for s in range(SSD_INNER // CONV_SLAB):
        xs_scr[:, s * CONV_SLAB:(s + 1) * CONV_SLAB] = conv_slab(xs_ref, prev_x, cwx_ref, cbx_ref, s * CONV_SLAB)
    for s in range(SSD_BC // CONV_SLAB):
        b_scr[:, s * CONV_SLAB:(s + 1) * CONV_SLAB] = conv_slab(bc_ref, prev_bc, cwbc_ref, cbbc_ref, s * CONV_SLAB)
    for s in range(SSD_BC // CONV_SLAB):
        c_scr[:, s * CONV_SLAB:(s + 1) * CONV_SLAB] = conv_slab(
            bc_ref, prev_bc, cwbc_ref, cbbc_ref, SSD_BC + s * CONV_SLAB).astype(BF16)

    dtv = jax.nn.softplus(dt_ref[...] + dtb_ref[...])
    d_a = dtv * (-jnp.exp(alog_ref[...]))
    rr = lax.broadcasted_iota(jnp.int32, (L, L), 0)
    cc = lax.broadcasted_iota(jnp.int32, (L, L), 1)
    causal = rr >= cc
    low_half = cc < SSD_HEADDIM
    tri = jnp.where(causal, 1.0, 0.0).astype(BF16)
    a_cum = sum(jnp.dot(tri, part, preferred_element_type=F32) for part in _split3(d_a))
    a_t = a_cum.T
    dt_t = dtv.T
    a_scr[...] = a_cum
    qt_scr[...] = a_t - jnp.log(dt_t)
    wt_scr[...] = dt_t * jnp.exp(a_t[:, L - 1:L] - a_t)

    def group_body(g, carry):
        off = pl.multiple_of(g * SSD_GW, SSD_GW)
        offn = pl.multiple_of(g * SSD_STATE, SSD_STATE)
        xg = xs_scr[:, pl.ds(off, SSD_GW)]
        xg_b = xg.astype(BF16)
        b_f = b_scr[:, pl.ds(offn, SSD_STATE)]
        c_b = c_scr[:, pl.ds(offn, SSD_STATE)]
        b_t = b_f.T
        cbm = lax.dot_general(c_b, b_f.astype(BF16), (((1,), (1,)), ((), ())),
                              preferred_element_type=F32)
        s_g = state[g]
        y_off = jnp.dot(c_b, s_g.astype(BF16), preferred_element_type=F32)
        a_g = pltpu.roll(a_scr[...], (LANES - SSD_HPG * g) % LANES, 1)
        ys = []
        for quad in range(SSD_HPG // QUAD):
            qs = slice(quad * QUAD_W, (quad + 1) * QUAD_W)
            m_parts, b_parts, a_cols = [], [], []
            for jj in range(QUAD):
                j = quad * QUAD + jj
                h = g * SSD_HPG + j
                a_col = jnp.broadcast_to(a_g[:, j:j + 1], (L, L))
                seg = a_col - qt_scr[pl.ds(h, 1), :]
                m_parts.append((cbm * jnp.exp(jnp.where(causal, seg, NEG_BIG))).astype(BF16))
                b_parts.append((b_t * wt_scr[pl.ds(h, 1), :]).astype(BF16))
                a_cols.append(a_col)
            lhs = jnp.concatenate([jnp.concatenate(m_parts, axis=1), jnp.concatenate(b_parts, axis=1)], axis=0)
            xq = xg_b[:, qs]
            rhs = jnp.concatenate([xq * qm_ref[jj] for jj in range(QUAD)], axis=0)
            res = jnp.dot(lhs, rhs, preferred_element_type=F32)
            ea_q = jnp.concatenate(
                [jnp.exp(jnp.where(low_half, a_cols[2 * p], a_cols[2 * p + 1])) for p in range(QUAD // 2)], axis=1)
            ys.append(res[:L] + y_off[:, qs] * ea_q)
            state[g, :, qs] = s_g[:, qs] * ea_q[L - 1:L, :] + res[L:]
        y = jnp.concatenate(ys, axis=1) + dsk_ref[:, pl.ds(off, SSD_GW)] * xg
        zg = z_ref[:, pl.ds(off, SSD_GW)].astype(F32)
        yg = y * (zg * jax.nn.sigmoid(zg))
        ms = jnp.mean(yg * yg, axis=-1, keepdims=True)
        yn = yg * lax.rsqrt(ms + EPS) * nw_ref[:, pl.ds(off, SSD_GW)]
        o_ref[:, pl.ds(off, SSD_GW)] = yn.astype(o_ref.dtype)
        return carry

    lax.fori_loop(0, SSD_GROUPS, group_body, 0)


def _ssd(proj, dt_raw, cwx, cbx, cwbc, cbbc, dtb, alog, dsk, nw, batch, seq):
    L = SSD_CHUNK
    nc = seq // L
    T = proj.shape[0]
    row = lambda b, c: b * nc + c
    const = lambda b, c: (0, 0)
    head_of_lane = jnp.arange(QUAD_W, dtype=jnp.int32) // SSD_HEADDIM
    qmask = (head_of_lane[None, None, :] == jnp.arange(QUAD, dtype=jnp.int32)[:, None, None])
    qmask = jnp.broadcast_to(qmask, (QUAD, L, QUAD_W)).astype(BF16)
    tap = jnp.arange(SSD_CONV - 1, dtype=jnp.int32)[:, None, None]
    t_out = jnp.arange(L, dtype=jnp.int32)[None, :, None]
    src = jnp.arange(2 * L, dtype=jnp.int32)[None, None, :]
    shift = (src == L + t_out - (SSD_CONV - 1) + tap).astype(BF16).reshape((SSD_CONV - 1) * L, 2 * L)
    return pl.pallas_call(
        _ssd_kernel,
        grid=(batch, nc),
        in_specs=[
            pl.BlockSpec((L, SSD_INNER), lambda b, c: (row(b, c), COL_Z // SSD_INNER)),
            pl.BlockSpec((L, SSD_INNER), lambda b, c: (row(b, c), COL_XS // SSD_INNER)),
            pl.BlockSpec((L, 2 * SSD_BC), lambda b, c: (row(b, c), COL_BC // (2 * SSD_BC))),
            pl.BlockSpec((L, LANES), lambda b, c: (row(b, c), 0)),
            pl.BlockSpec((SSD_CONV, SSD_INNER), const),
            pl.BlockSpec((1, SSD_INNER), const),
            pl.BlockSpec((SSD_CONV, 2 * SSD_BC), const),
            pl.BlockSpec((1, 2 * SSD_BC), const),
            pl.BlockSpec((1, LANES), const),
            pl.BlockSpec((1, LANES), const),
            pl.BlockSpec((1, SSD_INNER), const),
            pl.BlockSpec((1, SSD_INNER), const),
            pl.BlockSpec((QUAD, L, QUAD_W), lambda b, c: (0, 0, 0)),
            pl.BlockSpec(((SSD_CONV - 1) * L, 2 * L), const),
        ],
        out_specs=pl.BlockSpec((L, SSD_INNER), lambda b, c: (row(b, c), 0)),
        out_shape=jax.ShapeDtypeStruct((T, SSD_INNER), BF16),
        scratch_shapes=[
            pltpu.VMEM((L, SSD_INNER), BF16),
            pltpu.VMEM((L, 2 * SSD_BC), BF16),
            pltpu.VMEM((L, SSD_INNER), F32),
            pltpu.VMEM((L, SSD_BC), F32),
            pltpu.VMEM((L, SSD_BC), BF16),
            pltpu.VMEM((L, LANES), F32),
            pltpu.VMEM((LANES, L), F32),
            pltpu.VMEM((LANES, L), F32),
            pltpu.VMEM((SSD_GROUPS, SSD_STATE, SSD_GW), F32),
        ],
        compiler_params=_cparams(("arbitrary", "arbitrary")),
        name="ssd",
    )(proj, proj, proj, dt_raw, cwx, cbx, cwbc, cbbc, dtb, alog, dsk, nw, qmask, shift)


def _pack_bf16_pairs(v):
    bits = pltpu.bitcast(v.astype(BF16).astype(F32), jnp.uint32)
    return (bits[:, :HALF] >> 16) | (bits[:, HALF:] & jnp.uint32(0xFFFF0000))


def _unpack_lo(w):
    return pltpu.bitcast(w << 16, F32)


def _unpack_hi(w):
    return pltpu.bitcast(w & jnp.uint32(0xFFFF0000), F32)


def _mixout_kernel(pg_ref, yn_ref, gs_ref, x_ref, wso_ref, wo_ref, nw_ref, rwh_ref, rwl_ref, rb_ref,
                   h_ref, u_ref, lg_ref):
    ssd = jnp.dot(yn_ref[...], wso_ref[...], preferred_element_type=F32)
    merged = pg_ref[...].astype(F32) + jax.nn.sigmoid(gs_ref[...].astype(F32)) * ssd
    h = x_ref[...] + jnp.dot(merged.astype(BF16), wo_ref[...], preferred_element_type=F32)
    h_ref[...] = h
    ms = jnp.mean(h * h, axis=-1, keepdims=True)
    u = h * lax.rsqrt(ms + EPS) * nw_ref[...]
    u_hi = u.astype(BF16)
    u_ref[...] = _pack_bf16_pairs(u)
    u_lo = (u - u_hi.astype(F32)).astype(BF16)
    rwh = rwh_ref[...]
    lg = (jnp.dot(u_hi, rwh, preferred_element_type=F32)
          + jnp.dot(u_lo, rwh, preferred_element_type=F32)
          + jnp.dot(u_hi, rwl_ref[...], preferred_element_type=F32))
    lg_ref[...] = lg + rb_ref[...]


def _mixout(pool_g, yn, proj, x2, wso, wo, nw, rwh, rwl, rb, tm):
    T = x2.shape[0]
    const = lambda i: (0, 0)
    return pl.pallas_call(
        _mixout_kernel,
        grid=(T // tm,),
        in_specs=[
            pl.BlockSpec((tm, D_MODEL), lambda i: (i, 0)),
            pl.BlockSpec((tm, SSD_INNER), lambda i: (i, 0)),
            pl.BlockSpec((tm, D_MODEL), lambda i: (i, COL_GS // D_MODEL)),
            pl.BlockSpec((tm, D_MODEL), lambda i: (i, 0)),
            pl.BlockSpec((SSD_INNER, D_MODEL), const),
            pl.BlockSpec((D_MODEL, D_MODEL), const),
            pl.BlockSpec((1, D_MODEL), const),
            pl.BlockSpec((D_MODEL, LANES), const),
            pl.BlockSpec((D_MODEL, LANES), const),
            pl.BlockSpec((1, LANES), const),
        ],
        out_specs=[
            pl.BlockSpec((tm, D_MODEL), lambda i: (i, 0)),
            pl.BlockSpec((tm, HALF), lambda i: (i, 0)),
            pl.BlockSpec((tm, LANES), lambda i: (i, 0)),
        ],
        out_shape=[
            jax.ShapeDtypeStruct((T, D_MODEL), F32),
            jax.ShapeDtypeStruct((T, HALF), jnp.uint32),
            jax.ShapeDtypeStruct((T, LANES), F32),
        ],
        compiler_params=_cparams(("parallel",)),
        name="mixout",
    )(pool_g, yn, proj, x2, wso, wo, nw, rwh, rwl, rb)


def _router_kernel(lg_ref, er_ref, gt_ref, cnt_ref, run_scr, *, tm):
    @pl.when(pl.program_id(0) == 0)
    def _():
        run_scr[...] = jnp.zeros_like(run_scr)

    lane = lax.broadcasted_iota(jnp.int32, (tm, LANES), 1)
    lg = jnp.where(lane < N_EXPERTS, lg_ref[...], -jnp.inf)
    idxs, vals, hots = [], [], []
    for _ in range(TOP_K):
        m = jnp.max(lg, axis=-1, keepdims=True)
        idx = jnp.min(jnp.where(lg == m, lane, LANES), axis=-1, keepdims=True)
        hot = lane == idx
        lg = jnp.where(hot, -jnp.inf, lg)
        idxs.append(idx)
        vals.append(m)
        hots.append(hot)
    exps = [jnp.exp(v - vals[0]) for v in vals]
    den = exps[0] + exps[1] + exps[2] + exps[3]
    cnt = sum(jnp.where(hot, 1.0, 0.0) for hot in hots)
    rr = lax.broadcasted_iota(jnp.int32, (tm, tm), 0)
    cc = lax.broadcasted_iota(jnp.int32, (tm, tm), 1)
    before = jnp.where(rr > cc, 1.0, 0.0).astype(BF16)
    base = jnp.dot(before, cnt.astype(BF16), preferred_element_type=F32) + run_scr[...]
    er = jnp.zeros((tm, LANES), jnp.int32)
    gt = jnp.zeros((tm, LANES), F32)
    for k in range(TOP_K):
        rank = jnp.sum(jnp.where(hots[k], base, 0.0), axis=-1, keepdims=True)
        er = jnp.where(lane == k, idxs[k], er)
        er = jnp.where(lane == TOP_K + k, rank.astype(jnp.int32), er)
        gt = jnp.where(lane == k, exps[k] / den, gt)
    er_ref[...] = er
    gt_ref[...] = gt
    run_scr[...] += jnp.sum(cnt, axis=0, keepdims=True)
    cnt_ref[...] = run_scr[...]


def _router(logits, tm):
    T = logits.shape[0]
    return pl.pallas_call(
        functools.partial(_router_kernel, tm=tm),
        grid=(T // tm,),
        in_specs=[pl.BlockSpec((tm, LANES), lambda i: (i, 0))],
        out_specs=[
            pl.BlockSpec((tm, LANES), lambda i: (i, 0)),
            pl.BlockSpec((tm, LANES), lambda i: (i, 0)),
            pl.BlockSpec((1, LANES), lambda i: (0, 0)),
        ],
        out_shape=[
            jax.ShapeDtypeStruct((T, LANES), jnp.int32),
            jax.ShapeDtypeStruct((T, LANES), F32),
            jax.ShapeDtypeStruct((1, LANES), F32),
        ],
        scratch_shapes=[pltpu.VMEM((1, LANES), F32)],
        compiler_params=_cparams(("arbitrary",)),
        name="router",
    )(logits)


def _dispatch_kernel(ps_ref, e_ref, r_ref, u_ref, xb_ref, dest_ref, sem, *, tm):
    def body(g, carry):
        r0 = pl.multiple_of(g * ROW_GROUP, ROW_GROUP)
        for i in range(ROW_GROUP):
            for k in range(TOP_K):
                j = (r0 + i) * TOP_K + k
                d = ps_ref[e_ref[j]] + r_ref[j]
                dest_ref[j] = d
                pltpu.make_async_copy(u_ref.at[pl.ds(r0 + i, 1)], xb_ref.at[pl.ds(d, 1)], sem).start()
        return carry

    lax.fori_loop(0, tm // ROW_GROUP, body, 0)
    for _ in range(TOP_K):
        pltpu.make_async_copy(u_ref, xb_ref.at[pl.ds(0, tm)], sem).wait()


def _dispatch(pstart, e_flat, r_flat, u_pk, n_slots, tm):
    T = u_pk.shape[0]
    grid_spec = pltpu.PrefetchScalarGridSpec(
        num_scalar_prefetch=1,
        grid=(T // tm,),
        in_specs=[
            pl.BlockSpec((tm * TOP_K,), lambda i, ps: (i,), memory_space=pltpu.SMEM),
            pl.BlockSpec((tm * TOP_K,), lambda i, ps: (i,), memory_space=pltpu.SMEM),
            pl.BlockSpec((tm, HALF), lambda i, ps: (i, 0)),
        ],
        out_specs=[
            pl.BlockSpec(memory_space=pl.ANY),
            pl.BlockSpec((tm * TOP_K,), lambda i, ps: (i,), memory_space=pltpu.SMEM),
        ],
        scratch_shapes=[pltpu.SemaphoreType.DMA(())],
    )
    return pl.pallas_call(
        functools.partial(_dispatch_kernel, tm=tm),
        grid_spec=grid_spec,
        out_shape=[
            jax.ShapeDtypeStruct((n_slots, HALF), jnp.uint32),
            jax.ShapeDtypeStruct((T * TOP_K,), jnp.int32),
        ],
        compiler_params=_cparams(("arbitrary",)),
        name="dispatch",
    )(pstart, e_flat, r_flat, u_pk)


def _row_blocks(nrows, tm, sub, weights, block):
    nblk = (nrows + sub - 1) // sub
    per_big = min(FFN_BIG_BLOCK, tm) // sub
    nbig = tm // (per_big * sub)
    for b in range(nbig):
        @pl.when(nblk >= (b + 1) * per_big)
        def _():
            block(b * per_big * sub, per_big * sub, weights())

    def rest(sb, carry):
        block(pl.multiple_of(sb * sub, sub), sub, weights())
        return carry

    lax.fori_loop(jnp.minimum(nblk // per_big, nbig) * per_big, nblk, rest, 0)


def _ffn_up_kernel(te_ref, nu_ref, tr_ref, x_ref, wg_ref, wl_ref, bg_ref, bl_ref, o_ref, *, tm, sub):
    nrows = tr_ref[pl.program_id(0)]

    def weights():
        return wg_ref[0].astype(BF16), wl_ref[0].astype(BF16)

    def block(r0, nr, w):
        packed = x_ref[pl.ds(r0, nr), :]
        x = jnp.concatenate([_unpack_lo(packed).astype(BF16), _unpack_hi(packed).astype(BF16)], axis=1)
        glu = jnp.minimum(jnp.dot(x, w[0], preferred_element_type=F32) + bg_ref[0], SWIGLU_LIMIT)
        lin = jnp.clip(jnp.dot(x, w[1], preferred_element_type=F32) + bl_ref[0], -SWIGLU_LIMIT, SWIGLU_LIMIT)
        act = glu * jax.nn.sigmoid(SWIGLU_ALPHA * glu) * (lin + 1.0)
        o_ref[pl.ds(r0, nr), :] = act.astype(o_ref.dtype)

    _row_blocks(nrows, tm, sub, weights, block)


def _ffn_down_kernel(te_ref, nu_ref, tr_ref, a_ref, wa_ref, wb_ref, ba_ref, bb_ref, o_ref, *, tm, sub):
    nrows = tr_ref[pl.program_id(0)]

    def weights():
        return wa_ref[0].astype(BF16), wb_ref[0].astype(BF16)

    def block(r0, nr, w):
        a = a_ref[pl.ds(r0, nr), :]
        ya = jnp.dot(a, w[0], preferred_element_type=F32) + ba_ref[0]
        yb = jnp.dot(a, w[1], preferred_element_type=F32) + bb_ref[0]
        lo = pltpu.bitcast(ya.astype(BF16).astype(F32), jnp.uint32) >> 16
        hi = pltpu.bitcast(yb.astype(BF16).astype(F32), jnp.uint32) & jnp.uint32(0xFFFF0000)
        o_ref[pl.ds(r0, nr), :] = lo | hi

    _row_blocks(nrows, tm, sub, weights, block)


def _ffn(tile_expert, n_used, tile_rows, xb, w_up, b_up3, w_down, b_down3, tm, sub, fc, nc):
    n_slots = xb.shape[0]
    n_tiles = n_slots // tm
    nf = D_FF // fc
    nj = HALF // nc

    def live(i, nu):
        return jnp.minimum(i, nu[0] - 1)

    def frozen(i, j, nu, last):
        return jnp.where(i < nu[0], j, last)

    up_spec = pltpu.PrefetchScalarGridSpec(
        num_scalar_prefetch=3,
        grid=(n_tiles, nf),
        in_specs=[
            pl.BlockSpec((tm, HALF), lambda i, f, te, nu, tr: (live(i, nu), 0)),
            pl.BlockSpec((1, D_MODEL, fc), lambda i, f, te, nu, tr: (te[i], 0, frozen(i, f, nu, nf - 1))),
            pl.BlockSpec((1, D_MODEL, fc), lambda i, f, te, nu, tr: (te[i], 0, nf + frozen(i, f, nu, nf - 1))),
            pl.BlockSpec((1, 1, fc), lambda i, f, te, nu, tr: (te[i], 0, frozen(i, f, nu, nf - 1))),
            pl.BlockSpec((1, 1, fc), lambda i, f, te, nu, tr: (te[i], 0, nf + frozen(i, f, nu, nf - 1))),
        ],
        out_specs=pl.BlockSpec((tm, fc), lambda i, f, te, nu, tr: (live(i, nu), frozen(i, f, nu, nf - 1))),
    )
    act = pl.pallas_call(
        functools.partial(_ffn_up_kernel, tm=tm, sub=sub),
        grid_spec=up_spec,
        out_shape=jax.ShapeDtypeStruct((n_slots, D_FF), BF16),
        compiler_params=_cparams(("arbitrary", "arbitrary")),
        name="ffn_up",
    )(tile_expert, n_used, tile_rows, xb, w_up, w_up, b_up3, b_up3)

    down_spec = pltpu.PrefetchScalarGridSpec(
        num_scalar_prefetch=3,
        grid=(n_tiles, nj),
        in_specs=[
            pl.BlockSpec((tm, D_FF), lambda i, j, te, nu, tr: (live(i, nu), 0)),
            pl.BlockSpec((1, D_FF, nc), lambda i, j, te, nu, tr: (te[i], 0, frozen(i, j, nu, nj - 1))),
            pl.BlockSpec((1, D_FF, nc), lambda i, j, te, nu, tr: (te[i], 0, nj + frozen(i, j, nu, nj - 1))),
            pl.BlockSpec((1, 1, nc), lambda i, j, te, nu, tr: (te[i], 0, frozen(i, j, nu, nj - 1))),
            pl.BlockSpec((1, 1, nc), lambda i, j, te, nu, tr: (te[i], 0, nj + frozen(i, j, nu, nj - 1))),
        ],
        out_specs=pl.BlockSpec((tm, nc), lambda i, j, te, nu, tr: (live(i, nu), frozen(i, j, nu, nj - 1))),
    )
    return pl.pallas_call(
        functools.partial(_ffn_down_kernel, tm=tm, sub=sub),
        grid_spec=down_spec,
        out_shape=jax.ShapeDtypeStruct((n_slots, HALF), jnp.uint32),
        compiler_params=_cparams(("arbitrary", "arbitrary")),
        name="ffn_down",
    )(tile_expert, n_used, tile_rows, act, w_down, w_down, b_down3, b_down3)


def _combine_kernel(d_ref, h_ref, g_ref, nw_ref, yb_ref, o_ref, ybuf, sem, *, tm):
    def body(g, carry):
        r0 = pl.multiple_of(g * ROW_GROUP, ROW_GROUP)
        for i in range(ROW_GROUP):
            for k in range(TOP_K):
                d = d_ref[(r0 + i) * TOP_K + k]
                pltpu.make_async_copy(yb_ref.at[pl.ds(d, 1)], ybuf.at[k, pl.ds(r0 + i, 1)], sem).start()
        return carry

    lax.fori_loop(0, tm // ROW_GROUP, body, 0)
    for k in range(TOP_K):
        pltpu.make_async_copy(yb_ref.at[pl.ds(0, tm)], ybuf.at[k], sem).wait()
    lo = h_ref[:, :HALF]
    hi = h_ref[:, HALF:]
    for k in range(TOP_K):
        w = ybuf[k]
        g = g_ref[:, k:k + 1]
        lo = lo + g * _unpack_lo(w)
        hi = hi + g * _unpack_hi(w)
    ms = (jnp.sum(lo * lo, axis=-1, keepdims=True) + jnp.sum(hi * hi, axis=-1, keepdims=True)) / D_MODEL
    scale = lax.rsqrt(ms + EPS)
    o_ref[:, :HALF] = lo * scale * nw_ref[:, :HALF]
    o_ref[:, HALF:] = hi * scale * nw_ref[:, HALF:]


def _combine(dest, h, gates, nw, yb, tm):
    T = h.shape[0]
    return pl.pallas_call(
        functools.partial(_combine_kernel, tm=tm),
        grid=(T // tm,),
        in_specs=[
            pl.BlockSpec((tm * TOP_K,), lambda i: (i,), memory_space=pltpu.SMEM),
            pl.BlockSpec((tm, D_MODEL), lambda i: (i, 0)),
            pl.BlockSpec((tm, LANES), lambda i: (i, 0)),
            pl.BlockSpec((1, D_MODEL), lambda i: (0, 0)),
            pl.BlockSpec(memory_space=pl.ANY),
        ],
        out_specs=pl.BlockSpec((tm, D_MODEL), lambda i: (i, 0)),
        out_shape=jax.ShapeDtypeStruct((T, D_MODEL), F32),
        scratch_shapes=[pltpu.VMEM((TOP_K, tm, HALF), jnp.uint32), pltpu.SemaphoreType.DMA(())],
        compiler_params=_cparams(("arbitrary",)),
        name="combine",
    )(dest, h, gates, nw, yb)


def _row(v, width=None):
    v = v.astype(F32).reshape(1, -1)
    if width is not None and v.shape[1] < width:
        v = jnp.pad(v, ((0, 0), (0, width - v.shape[1])))
    return v


def _tile(n, pref):
    t = pref
    while n % t:
        t //= 2
    return t


def kernel(x, norm_mix_w, w_in, w_pool, pool_scale, conv_w, conv_b, dt_bias, a_log, d_skip,
           ssd_norm_w, w_ssd_out, w_out, norm_ffn_w, router_w, router_b, w_up, b_up,
           w_down, b_down, norm_final_w):
    B, S, D = x.shape
    T = B * S
    x2 = x.reshape(T, D)

    o_p, o_z, o_xbc = 0, D_MODEL, D_MODEL + SSD_INNER
    o_dt = o_xbc + SSD_INNER + 2 * SSD_BC
    o_gp = o_dt + SSD_HEADS
    o_gs = o_gp + D_MODEL
    w_main = jnp.concatenate(
        [w_in[:, o_z:o_z + SSD_INNER], w_in[:, o_xbc:o_dt], w_in[:, o_p:o_p + D_MODEL],
         w_in[:, o_gp:o_gp + D_MODEL], w_in[:, o_gs:o_gs + D_MODEL]], axis=1).astype(BF16)
    w_dt = jnp.pad(w_in[:, o_dt:o_dt + SSD_HEADS], ((0, 0), (0, LANES - SSD_HEADS))).astype(BF16)

    proj, dt_raw = _inproj(x2, _row(norm_mix_w), w_main, w_dt, _tile(T, 1024), 2048)

    pool_g = _pool(proj, w_pool.astype(BF16), _row(pool_scale), _tile(S, 256), S)

    yn = _ssd(proj, dt_raw,
              conv_w[:, :SSD_INNER].astype(F32), _row(conv_b[:SSD_INNER]),
              conv_w[:, SSD_INNER:].astype(F32), _row(conv_b[SSD_INNER:]),
              _row(dt_bias, LANES), _row(a_log, LANES),
              _row(jnp.repeat(d_skip, SSD_HEADDIM)), _row(ssd_norm_w), B, S)

    rw = jnp.pad(router_w.astype(F32), ((0, 0), (0, LANES - N_EXPERTS)))
    rw_hi = rw.astype(BF16)
    rw_lo = (rw - rw_hi.astype(F32)).astype(BF16)
    h, u_pk, logits = _mixout(pool_g, yn, proj, x2, w_ssd_out.astype(BF16), w_out.astype(BF16),
                              _row(norm_ffn_w), rw_hi, rw_lo, _row(router_b, LANES), _tile(T, 256))

    er, gates, cnt = _router(logits, _tile(T, 256))

    TK = T * TOP_K
    tm_ffn = 2304 if TK >= 32768 else 256
    sub = 256
    n_tiles = -(-TK // tm_ffn) + N_EXPERTS
    counts = cnt[0, :N_EXPERTS].astype(jnp.int32)
    tiles_e = (counts + tm_ffn - 1) // tm_ffn
    tend = jnp.cumsum(tiles_e)
    tstart = tend - tiles_e
    pstart = (tstart * tm_ffn).astype(jnp.int32)
    tile_id = jnp.arange(n_tiles, dtype=jnp.int32)
    tile_expert = jnp.minimum(jnp.searchsorted(tend, tile_id, side='right'), N_EXPERTS - 1).astype(jnp.int32)
    tile_rows = jnp.clip(counts[tile_expert] - (tile_id - tstart[tile_expert]) * tm_ffn, 0, tm_ffn).astype(jnp.int32)
    n_used = tend[-1].astype(jnp.int32).reshape(1)
    e_flat = er[:, :TOP_K].reshape(TK)
    r_flat = er[:, TOP_K:2 * TOP_K].reshape(TK)

    xb, dest = _dispatch(pstart, e_flat, r_flat, u_pk, n_tiles * tm_ffn, _tile(T, 512))
    yb = _ffn(tile_expert, n_used, tile_rows, xb, w_up, b_up.reshape(N_EXPERTS, 1, 2 * D_FF),
              w_down, b_down.reshape(N_EXPERTS, 1, D_MODEL), tm_ffn, sub, 512, 512)
    out = _combine(dest, h, gates, _row(norm_final_w), yb, _tile(T, 256))
    return out.reshape(B, S, D)
```

```python
import functools

import jax
import jax.numpy as jnp
from jax import lax
from jax.experimental import pallas as pl
from jax.experimental.pallas import tpu as pltpu

F32 = jnp.float32
BF16 = jnp.bfloat16

D_MODEL = 2048
EPS = 1e-5
POOL_GROUPS = 4
POOL_WINDOWS = (2, 4, 8, 16)
POOL_GW = D_MODEL // POOL_GROUPS
SSD_INNER = 2 * D_MODEL
SSD_HEADDIM = 64
SSD_HEADS = SSD_INNER // SSD_HEADDIM
SSD_GROUPS = 8
SSD_HPG = SSD_HEADS // SSD_GROUPS
SSD_STATE = 128
SSD_CONV = 4
SSD_CHUNK = 128
SSD_GW = SSD_HPG * SSD_HEADDIM
SSD_BC = SSD_GROUPS * SSD_STATE
N_EXPERTS = 32
TOP_K = 4
D_FF = D_MODEL
SWIGLU_ALPHA = 1.702
SWIGLU_LIMIT = 7.0

LANES = 128
HALF = D_MODEL // 2
CONV_SLAB = 256
QUAD = 4
QUAD_W = QUAD * SSD_HEADDIM
FFN_BIG_BLOCK = 1024
ROW_GROUP = 8
VMEM_LIMIT = 56 * 1024 * 1024

COL_Z = 0
COL_XS = SSD_INNER
COL_BC = 2 * SSD_INNER
COL_P = 2 * SSD_INNER + 2 * SSD_BC
COL_GP = COL_P + D_MODEL
COL_GS = COL_GP + D_MODEL
PROJ_COLS = COL_GS + D_MODEL

NEG_BIG = -1e30


def _cparams(sem):
    return pltpu.CompilerParams(dimension_semantics=sem, vmem_limit_bytes=VMEM_LIMIT)


def _inproj_kernel(x_ref, nw_ref, w_ref, wdt_ref, o_ref, dt_ref, u_scr):
    @pl.when(pl.program_id(1) == 0)
    def _():
        x = x_ref[...]
        ms = jnp.mean(x * x, axis=-1, keepdims=True)
        u = (x * lax.rsqrt(ms + EPS) * nw_ref[...]).astype(BF16)
        u_scr[...] = u
        dt_ref[...] = jnp.dot(u, wdt_ref[...], preferred_element_type=F32)

    o_ref[...] = jnp.dot(u_scr[...], w_ref[...], preferred_element_type=F32).astype(o_ref.dtype)


def _inproj(x2, norm_w, w_main, w_dt, tm, tn):
    T = x2.shape[0]
    return pl.pallas_call(
        _inproj_kernel,
        grid=(T // tm, PROJ_COLS // tn),
        in_specs=[
            pl.BlockSpec((tm, D_MODEL), lambda i, j: (i, 0)),
            pl.BlockSpec((1, D_MODEL), lambda i, j: (0, 0)),
            pl.BlockSpec((D_MODEL, tn), lambda i, j: (0, j)),
            pl.BlockSpec((D_MODEL, LANES), lambda i, j: (0, 0)),
        ],
        out_specs=[
            pl.BlockSpec((tm, tn), lambda i, j: (i, j)),
            pl.BlockSpec((tm, LANES), lambda i, j: (i, 0)),
        ],
        out_shape=[
            jax.ShapeDtypeStruct((T, PROJ_COLS), BF16),
            jax.ShapeDtypeStruct((T, LANES), F32),
        ],
        scratch_shapes=[pltpu.VMEM((tm, D_MODEL), BF16)],
        compiler_params=_cparams(("parallel", "arbitrary")),
        name="inproj",
    )(x2, norm_w, w_main, w_dt)


def _pool_kernel(prev_ref, p_ref, g_ref, band_ref, wp_ref, ps_ref, o_ref, *, tm, seq):
    row0 = (pl.program_id(0) * tm) % seq
    pos = row0 + lax.broadcasted_iota(jnp.int32, (tm, 1), 0)
    for g, w in enumerate(POOL_WINDOWS):
        cols = slice(g * POOL_GW, (g + 1) * POOL_GW)
        cur = p_ref[:, cols]
        prev = prev_ref[:, cols]
        prev = jnp.where(row0 == 0, jnp.zeros_like(prev), prev)
        acc = jnp.dot(band_ref[g], jnp.concatenate([prev, cur], axis=0), preferred_element_type=F32)
        cnt = jnp.minimum(pos + 1, w).astype(F32)
        pooled = acc / cnt - cur.astype(F32)
        y = jnp.dot(pooled.astype(BF16), wp_ref[g], preferred_element_type=F32)
        y = y * ps_ref[:, cols] * jax.nn.sigmoid(g_ref[:, cols].astype(F32))
        o_ref[:, cols] = y.astype(o_ref.dtype)


def _pool(proj, w_pool, pool_scale, tm, seq):
    T = proj.shape[0]
    t_out = jnp.arange(tm, dtype=jnp.int32)[None, :, None]
    src = jnp.arange(2 * tm, dtype=jnp.int32)[None, None, :]
    win = jnp.asarray(POOL_WINDOWS, jnp.int32)[:, None, None]
    band = jnp.logical_and(src > tm + t_out - win, src <= tm + t_out).astype(BF16)
    return pl.pallas_call(
        functools.partial(_pool_kernel, tm=tm, seq=seq),
        grid=(T // tm,),
        in_specs=[
            pl.BlockSpec((tm, D_MODEL), lambda i: (jnp.maximum(i - 1, 0), COL_P // D_MODEL)),
            pl.BlockSpec((tm, D_MODEL), lambda i: (i, COL_P // D_MODEL)),
            pl.BlockSpec((tm, D_MODEL), lambda i: (i, COL_GP // D_MODEL)),
            pl.BlockSpec((POOL_GROUPS, tm, 2 * tm), lambda i: (0, 0, 0)),
            pl.BlockSpec((POOL_GROUPS, POOL_GW, POOL_GW), lambda i: (0, 0, 0)),
            pl.BlockSpec((1, D_MODEL), lambda i: (0, 0)),
        ],
        out_specs=pl.BlockSpec((tm, D_MODEL), lambda i: (i, 0)),
        out_shape=jax.ShapeDtypeStruct((T, D_MODEL), BF16),
        compiler_params=_cparams(("parallel",)),
        name="pool",
    )(proj, proj, proj, band, w_pool, pool_scale)


def _split3(v):
    hi = v.astype(BF16)
    r1 = v - hi.astype(F32)
    mid = r1.astype(BF16)
    lo = (r1 - mid.astype(F32)).astype(BF16)
    return hi, mid, lo


def _ssd_kernel(z_ref, xs_ref, bc_ref, dt_ref, cwx_ref, cbx_ref, cwbc_ref, cbbc_ref,
                dtb_ref, alog_ref, dsk_ref, nw_ref, qm_ref, sh_ref, o_ref,
                prev_x, prev_bc, xs_scr, b_scr, c_scr, a_scr, qt_scr, wt_scr, state):
    L = SSD_CHUNK
    c = pl.program_id(1)

    @pl.when(c == 0)
    def _():
        prev_x[...] = jnp.zeros_like(prev_x)
        prev_bc[...] = jnp.zeros_like(prev_bc)
        state[...] = jnp.zeros_like(state)

    def conv_slab(src_ref, prev_ref, cw_ref, cb_ref, col):
        cols = slice(col, col + CONV_SLAB)
        raw = src_ref[:, cols]
        both = jnp.concatenate([prev_ref[:, cols], raw], axis=0)
        prev_ref[:, cols] = raw
        acc = cb_ref[:, cols] + cw_ref[SSD_CONV - 1:SSD_CONV, cols] * raw.astype(F32)
        for k in range(SSD_CONV - 1):
            shifted = jnp.dot(sh_ref[k * L:(k + 1) * L, :], both, preferred_element_type=F32)
            acc = acc + cw_ref[k:k + 1, cols] * shifted
        return acc * jax.nn.sigmoid(acc)

    for s in range(SSD_INNER // CONV_SLAB):
        xs_scr[:, s * CONV_SLAB:(s + 1) * CONV_SLAB] = conv_slab(xs_ref, prev_x, cwx_ref, cbx_ref, s * CONV_SLAB)
    for s in range(SSD_BC // CONV_SLAB):
        b_scr[:, s * CONV_SLAB:(s + 1) * CONV_SLAB] = conv_slab(bc_ref, prev_bc, cwbc_ref, cbbc_ref, s * CONV_SLAB)
    for s in range(SSD_BC // CONV_SLAB):
        c_scr[:, s * CONV_SLAB:(s + 1) * CONV_SLAB] = conv_slab(
            bc_ref, prev_bc, cwbc_ref, cbbc_ref, SSD_BC + s * CONV_SLAB).astype(BF16)

    dtv = jax.nn.softplus(dt_ref[...] + dtb_ref[...])
    d_a = dtv * (-jnp.exp(alog_ref[...]))
    rr = lax.broadcasted_iota(jnp.int32, (L, L), 0)
    cc = lax.broadcasted_iota(jnp.int32, (L, L), 1)
    causal = rr >= cc
    low_half = cc < SSD_HEADDIM
    tri = jnp.where(causal, 1.0, 0.0).astype(BF16)
    a_cum = sum(jnp.dot(tri, part, preferred_element_type=F32) for part in _split3(d_a))
    a_t = a_cum.T
    dt_t = dtv.T
    a_scr[...] = a_cum
    qt_scr[...] = a_t - jnp.log(dt_t)
    wt_scr[...] = dt_t * jnp.exp(a_t[:, L - 1:L] - a_t)

    def group_body(g, carry):
        off = pl.multiple_of(g * SSD_GW, SSD_GW)
        offn = pl.multiple_of(g * SSD_STATE, SSD_STATE)
        xg = xs_scr[:, pl.ds(off, SSD_GW)]
        xg_b = xg.astype(BF16)
        b_f = b_scr[:, pl.ds(offn, SSD_STATE)]
        c_b = c_scr[:, pl.ds(offn, SSD_STATE)]
        b_t = b_f.T
        cbm = lax.dot_general(c_b, b_f.astype(BF16), (((1,), (1,)), ((), ())),
                              preferred_element_type=F32)
        s_g = state[g]
        y_off = jnp.dot(c_b, s_g.astype(BF16), preferred_element_type=F32)
        a_g = pltpu.roll(a_scr[...], (LANES - SSD_HPG * g) % LANES, 1)
        ys = []
        for quad in range(SSD_HPG // QUAD):
            qs = slice(quad * QUAD_W, (quad + 1) * QUAD_W)
            m_parts, b_parts, a_cols = [], [], []
            for jj in range(QUAD):
                j = quad * QUAD + jj
                h = g * SSD_HPG + j
                a_col = jnp.broadcast_to(a_g[:, j:j + 1], (L, L))
                seg = a_col - qt_scr[pl.ds(h, 1), :]
                m_parts.append((cbm * jnp.exp(jnp.where(causal, seg, NEG_BIG))).astype(BF16))
                b_parts.append((b_t * wt_scr[pl.ds(h, 1), :]).astype(BF16))
                a_cols.append(a_col)
            lhs = jnp.concatenate([jnp.concatenate(m_parts, axis=1), jnp.concatenate(b_parts, axis=1)], axis=0)
            xq = xg_b[:, qs]
            rhs = jnp.concatenate([xq * qm_ref[jj] for jj in range(QUAD)], axis=0)
            res = jnp.dot(lhs, rhs, preferred_element_type=F32)
            ea_q = jnp.concatenate(
                [jnp.exp(jnp.where(low_half, a_cols[2 * p], a_cols[2 * p + 1])) for p in range(QUAD // 2)], axis=1)
            ys.append(res[:L] + y_off[:, qs] * ea_q)
            state[g, :, qs] = s_g[:, qs] * ea_q[L - 1:L, :] + res[L:]
        y = jnp.concatenate(ys, axis=1) + dsk_ref[:, pl.ds(off, SSD_GW)] * xg
        zg = z_ref[:, pl.ds(off, SSD_GW)].astype(F32)
        yg = y * (zg * jax.nn.sigmoid(zg))
        ms = jnp.mean(yg * yg, axis=-1, keepdims=True)
        yn = yg * lax.rsqrt(ms + EPS) * nw_ref[:, pl.ds(off, SSD_GW)]
        o_ref[:, pl.ds(off, SSD_GW)] = yn.astype(o_ref.dtype)
        return carry

    def group_pair(gp, carry):
        group_body(2 * gp, carry)
        return group_body(2 * gp + 1, carry)

    lax.fori_loop(0, SSD_GROUPS // 2, group_pair, 0)


def _ssd(proj, dt_raw, cwx, cbx, cwbc, cbbc, dtb, alog, dsk, nw, batch, seq):
    L = SSD_CHUNK
    nc = seq // L
    T = proj.shape[0]
    row = lambda b, c: b * nc + c
    const = lambda b, c: (0, 0)
    head_of_lane = jnp.arange(QUAD_W, dtype=jnp.int32) // SSD_HEADDIM
    qmask = (head_of_lane[None, None, :] == jnp.arange(QUAD, dtype=jnp.int32)[:, None, None])
    qmask = jnp.broadcast_to(qmask, (QUAD, L, QUAD_W)).astype(BF16)
    tap = jnp.arange(SSD_CONV - 1, dtype=jnp.int32)[:, None, None]
    t_out = jnp.arange(L, dtype=jnp.int32)[None, :, None]
    src = jnp.arange(2 * L, dtype=jnp.int32)[None, None, :]
    shift = (src == L + t_out - (SSD_CONV - 1) + tap).astype(BF16).reshape((SSD_CONV - 1) * L, 2 * L)
    return pl.pallas_call(
        _ssd_kernel,
        grid=(batch, nc),
        in_specs=[
            pl.BlockSpec((L, SSD_INNER), lambda b, c: (row(b, c), COL_Z // SSD_INNER)),
            pl.BlockSpec((L, SSD_INNER), lambda b, c: (row(b, c), COL_XS // SSD_INNER)),
            pl.BlockSpec((L, 2 * SSD_BC), lambda b, c: (row(b, c), COL_BC // (2 * SSD_BC))),
            pl.BlockSpec((L, LANES), lambda b, c: (row(b, c), 0)),
            pl.BlockSpec((SSD_CONV, SSD_INNER), const),
            pl.BlockSpec((1, SSD_INNER), const),
            pl.BlockSpec((SSD_CONV, 2 * SSD_BC), const),
            pl.BlockSpec((1, 2 * SSD_BC), const),
            pl.BlockSpec((1, LANES), const),
            pl.BlockSpec((1, LANES), const),
            pl.BlockSpec((1, SSD_INNER), const),
            pl.BlockSpec((1, SSD_INNER), const),
            pl.BlockSpec((QUAD, L, QUAD_W), lambda b, c: (0, 0, 0)),
            pl.BlockSpec(((SSD_CONV - 1) * L, 2 * L), const),
        ],
        out_specs=pl.BlockSpec((L, SSD_INNER), lambda b, c: (row(b, c), 0)),
        out_shape=jax.ShapeDtypeStruct((T, SSD_INNER), BF16),
        scratch_shapes=[
            pltpu.VMEM((L, SSD_INNER), BF16),
            pltpu.VMEM((L, 2 * SSD_BC), BF16),
            pltpu.VMEM((L, SSD_INNER), F32),
            pltpu.VMEM((L, SSD_BC), F32),
            pltpu.VMEM((L, SSD_BC), BF16),
            pltpu.VMEM((L, LANES), F32),
            pltpu.VMEM((LANES, L), F32),
            pltpu.VMEM((LANES, L), F32),
            pltpu.VMEM((SSD_GROUPS, SSD_STATE, SSD_GW), F32),
        ],
        compiler_params=_cparams(("arbitrary", "arbitrary")),
        name="ssd",
    )(proj, proj, proj, dt_raw, cwx, cbx, cwbc, cbbc, dtb, alog, dsk, nw, qmask, shift)


def _pack_bf16_pairs(v):
    bits = pltpu.bitcast(v.astype(BF16).astype(F32), jnp.uint32)
    return (bits[:, :HALF] >> 16) | (bits[:, HALF:] & jnp.uint32(0xFFFF0000))


def _unpack_lo(w):
    return pltpu.bitcast(w << 16, F32)


def _unpack_hi(w):
    return pltpu.bitcast(w & jnp.uint32(0xFFFF0000), F32)


def _mixout_kernel(pg_ref, yn_ref, gs_ref, x_ref, wso_ref, wo_ref, nw_ref, rw_ref, rb_ref,
                   h_ref, u_ref, lg_ref):
    ssd = jnp.dot(yn_ref[...], wso_ref[...], preferred_element_type=F32)
    merged = pg_ref[...].astype(F32) + jax.nn.sigmoid(gs_ref[...].astype(F32)) * ssd
    h = x_ref[...] + jnp.dot(merged.astype(BF16), wo_ref[...], preferred_element_type=F32)
    h_ref[...] = h
    ms = jnp.mean(h * h, axis=-1, keepdims=True)
    u = h * lax.rsqrt(ms + EPS) * nw_ref[...]
    u_hi = u.astype(BF16)
    u_ref[...] = _pack_bf16_pairs(u)
    u_lo = (u - u_hi.astype(F32)).astype(BF16)
    both = jnp.dot(u_hi, rw_ref[...], preferred_element_type=F32)
    lg = both[:, :LANES] + both[:, LANES:] + jnp.dot(u_lo, rw_ref[:, :LANES], preferred_element_type=F32)
    lg_ref[...] = lg + rb_ref[...]


def _mixout(pool_g, yn, proj, x2, wso, wo, nw, rw, rb, tm):
    T = x2.shape[0]
    const = lambda i: (0, 0)
    return pl.pallas_call(
        _mixout_kernel,
        grid=(T // tm,),
        in_specs=[
            pl.BlockSpec((tm, D_MODEL), lambda i: (i, 0)),
            pl.BlockSpec((tm, SSD_INNER), lambda i: (i, 0)),
            pl.BlockSpec((tm, D_MODEL), lambda i: (i, COL_GS // D_MODEL)),
            pl.BlockSpec((tm, D_MODEL), lambda i: (i, 0)),
            pl.BlockSpec((SSD_INNER, D_MODEL), const),
            pl.BlockSpec((D_MODEL, D_MODEL), const),
            pl.BlockSpec((1, D_MODEL), const),
            pl.BlockSpec((D_MODEL, 2 * LANES), const),
            pl.BlockSpec((1, LANES), const),
        ],
        out_specs=[
            pl.BlockSpec((tm, D_MODEL), lambda i: (i, 0)),
            pl.BlockSpec((tm, HALF), lambda i: (i, 0)),
            pl.BlockSpec((tm, LANES), lambda i: (i, 0)),
        ],
        out_shape=[
            jax.ShapeDtypeStruct((T, D_MODEL), F32),
            jax.ShapeDtypeStruct((T, HALF), jnp.uint32),
            jax.ShapeDtypeStruct((T, LANES), F32),
        ],
        compiler_params=_cparams(("parallel",)),
        name="mixout",
    )(pool_g, yn, proj, x2, wso, wo, nw, rw, rb)


def _router_kernel(lg_ref, er_ref, gt_ref, cnt_ref, run_scr, *, tm):
    @pl.when(pl.program_id(0) == 0)
    def _():
        run_scr[...] = jnp.zeros_like(run_scr)

    lane = lax.broadcasted_iota(jnp.int32, (tm, LANES), 1)
    lg = jnp.where(lane < N_EXPERTS, lg_ref[...], -jnp.inf)
    idxs, vals, hots = [], [], []
    for _ in range(TOP_K):
        m = jnp.max(lg, axis=-1, keepdims=True)
        idx = jnp.min(jnp.where(lg == m, lane, LANES), axis=-1, keepdims=True)
        hot = lane == idx
        lg = jnp.where(hot, -jnp.inf, lg)
        idxs.append(idx)
        vals.append(m)
        hots.append(hot)
    exps = [jnp.exp(v - vals[0]) for v in vals]
    den = exps[0] + exps[1] + exps[2] + exps[3]
    cnt = sum(jnp.where(hot, 1.0, 0.0) for hot in hots)
    rr = lax.broadcasted_iota(jnp.int32, (tm, tm), 0)
    cc = lax.broadcasted_iota(jnp.int32, (tm, tm), 1)
    before = jnp.where(rr > cc, 1.0, 0.0).astype(BF16)
    base = jnp.dot(before, cnt.astype(BF16), preferred_element_type=F32) + run_scr[...]
    er = jnp.zeros((tm, LANES), jnp.int32)
    gt = jnp.zeros((tm, LANES), F32)
    for k in range(TOP_K):
        rank = jnp.sum(jnp.where(hots[k], base, 0.0), axis=-1, keepdims=True)
        er = jnp.where(lane == k, idxs[k], er)
        er = jnp.where(lane == TOP_K + k, rank.astype(jnp.int32), er)
        gt = jnp.where(lane == k, exps[k] / den, gt)
    er_ref[...] = er
    gt_ref[...] = gt
    run_scr[...] += jnp.sum(cnt, axis=0, keepdims=True)
    cnt_ref[...] = run_scr[...]


def _router(logits, tm):
    T = logits.shape[0]
    return pl.pallas_call(
        functools.partial(_router_kernel, tm=tm),
        grid=(T // tm,),
        in_specs=[pl.BlockSpec((tm, LANES), lambda i: (i, 0))],
        out_specs=[
            pl.BlockSpec((tm, LANES), lambda i: (i, 0)),
            pl.BlockSpec((tm, LANES), lambda i: (i, 0)),
            pl.BlockSpec((1, LANES), lambda i: (0, 0)),
        ],
        out_shape=[
            jax.ShapeDtypeStruct((T, LANES), jnp.int32),
            jax.ShapeDtypeStruct((T, LANES), F32),
            jax.ShapeDtypeStruct((1, LANES), F32),
        ],
        scratch_shapes=[pltpu.VMEM((1, LANES), F32)],
        compiler_params=_cparams(("arbitrary",)),
        name="router",
    )(logits)


def _dispatch_kernel(ps_ref, e_ref, r_ref, u_ref, xb_ref, dest_ref, sem, *, tm):
    def body(g, carry):
        r0 = pl.multiple_of(g * ROW_GROUP, ROW_GROUP)
        for i in range(ROW_GROUP):
            for k in range(TOP_K):
                j = (r0 + i) * TOP_K + k
                d = ps_ref[e_ref[j]] + r_ref[j]
                dest_ref[j] = d
                pltpu.make_async_copy(u_ref.at[pl.ds(r0 + i, 1)], xb_ref.at[pl.ds(d, 1)], sem).start()
        return carry

    lax.fori_loop(0, tm // ROW_GROUP, body, 0)
    for _ in range(TOP_K):
        pltpu.make_async_copy(u_ref, xb_ref.at[pl.ds(0, tm)], sem).wait()


def _dispatch(pstart, e_flat, r_flat, u_pk, n_slots, tm):
    T = u_pk.shape[0]
    grid_spec = pltpu.PrefetchScalarGridSpec(
        num_scalar_prefetch=1,
        grid=(T // tm,),
        in_specs=[
            pl.BlockSpec((tm * TOP_K,), lambda i, ps: (i,), memory_space=pltpu.SMEM),
            pl.BlockSpec((tm * TOP_K,), lambda i, ps: (i,), memory_space=pltpu.SMEM),
            pl.BlockSpec((tm, HALF), lambda i, ps: (i, 0)),
        ],
        out_specs=[
            pl.BlockSpec(memory_space=pl.ANY),
            pl.BlockSpec((tm * TOP_K,), lambda i, ps: (i,), memory_space=pltpu.SMEM),
        ],
        scratch_shapes=[pltpu.SemaphoreType.DMA(())],
    )
    return pl.pallas_call(
        functools.partial(_dispatch_kernel, tm=tm),
        grid_spec=grid_spec,
        out_shape=[
            jax.ShapeDtypeStruct((n_slots, HALF), jnp.uint32),
            jax.ShapeDtypeStruct((T * TOP_K,), jnp.int32),
        ],
        compiler_params=_cparams(("arbitrary",)),
        name="dispatch",
    )(pstart, e_flat, r_flat, u_pk)


def _row_blocks(nrows, tm, sub, weights, block):
    nblk = (nrows + sub - 1) // sub
    per_big = min(FFN_BIG_BLOCK, tm) // sub
    nbig = tm // (per_big * sub)
    for b in range(nbig):
        @pl.when(nblk >= (b + 1) * per_big)
        def _():
            block(b * per_big * sub, per_big * sub, weights())

    def rest(sb, carry):
        block(pl.multiple_of(sb * sub, sub), sub, weights())
        return carry

    lax.fori_loop(jnp.minimum(nblk // per_big, nbig) * per_big, nblk, rest, 0)


def _ffn_up_kernel(te_ref, nu_ref, tr_ref, x_ref, wg_ref, wl_ref, bg_ref, bl_ref, o_ref, *, tm, sub):
    nrows = tr_ref[pl.program_id(0)]

    def weights():
        return wg_ref[0].astype(BF16), wl_ref[0].astype(BF16)

    def block(r0, nr, w):
        packed = x_ref[pl.ds(r0, nr), :]
        x = jnp.concatenate([_unpack_lo(packed).astype(BF16), _unpack_hi(packed).astype(BF16)], axis=1)
        glu = jnp.minimum(jnp.dot(x, w[0], preferred_element_type=F32) + bg_ref[0], SWIGLU_LIMIT)
        lin = jnp.clip(jnp.dot(x, w[1], preferred_element_type=F32) + bl_ref[0], -SWIGLU_LIMIT, SWIGLU_LIMIT)
        act = glu * jax.nn.sigmoid(SWIGLU_ALPHA * glu) * (lin + 1.0)
        o_ref[pl.ds(r0, nr), :] = act.astype(o_ref.dtype)

    _row_blocks(nrows, tm, sub, weights, block)


def _ffn_down_kernel(te_ref, nu_ref, tr_ref, a_ref, wa_ref, wb_ref, ba_ref, bb_ref, o_ref, *, tm, sub):
    nrows = tr_ref[pl.program_id(0)]

    def weights():
        return wa_ref[0].astype(BF16), wb_ref[0].astype(BF16)

    def block(r0, nr, w):
        a = a_ref[pl.ds(r0, nr), :]
        ya = jnp.dot(a, w[0], preferred_element_type=F32) + ba_ref[0]
        yb = jnp.dot(a, w[1], preferred_element_type=F32) + bb_ref[0]
        lo = pltpu.bitcast(ya.astype(BF16).astype(F32), jnp.uint32) >> 16
        hi = pltpu.bitcast(yb.astype(BF16).astype(F32), jnp.uint32) & jnp.uint32(0xFFFF0000)
        o_ref[pl.ds(r0, nr), :] = lo | hi

    _row_blocks(nrows, tm, sub, weights, block)


def _ffn(tile_expert, n_used, tile_rows, xb, w_up, b_up3, w_down, b_down3, tm, sub, fc, nc):
    n_slots = xb.shape[0]
    n_tiles = n_slots // tm
    nf = D_FF // fc
    nj = HALF // nc

    def live(i, nu):
        return jnp.minimum(i, nu[0] - 1)

    def frozen(i, j, nu, last):
        return jnp.where(i < nu[0], j, last)

    up_spec = pltpu.PrefetchScalarGridSpec(
        num_scalar_prefetch=3,
        grid=(n_tiles, nf),
        in_specs=[
            pl.BlockSpec((tm, HALF), lambda i, f, te, nu, tr: (live(i, nu), 0)),
            pl.BlockSpec((1, D_MODEL, fc), lambda i, f, te, nu, tr: (te[i], 0, frozen(i, f, nu, nf - 1))),
            pl.BlockSpec((1, D_MODEL, fc), lambda i, f, te, nu, tr: (te[i], 0, nf + frozen(i, f, nu, nf - 1))),
            pl.BlockSpec((1, 1, fc), lambda i, f, te, nu, tr: (te[i], 0, frozen(i, f, nu, nf - 1))),
            pl.BlockSpec((1, 1, fc), lambda i, f, te, nu, tr: (te[i], 0, nf + frozen(i, f, nu, nf - 1))),
        ],
        out_specs=pl.BlockSpec((tm, fc), lambda i, f, te, nu, tr: (live(i, nu), frozen(i, f, nu, nf - 1))),
    )
    act = pl.pallas_call(
        functools.partial(_ffn_up_kernel, tm=tm, sub=sub),
        grid_spec=up_spec,
        out_shape=jax.ShapeDtypeStruct((n_slots, D_FF), BF16),
        compiler_params=_cparams(("arbitrary", "arbitrary")),
        name="ffn_up",
    )(tile_expert, n_used, tile_rows, xb, w_up, w_up, b_up3, b_up3)

    down_spec = pltpu.PrefetchScalarGridSpec(
        num_scalar_prefetch=3,
        grid=(n_tiles, nj),
        in_specs=[
            pl.BlockSpec((tm, D_FF), lambda i, j, te, nu, tr: (live(i, nu), 0)),
            pl.BlockSpec((1, D_FF, nc), lambda i, j, te, nu, tr: (te[i], 0, frozen(i, j, nu, nj - 1))),
            pl.BlockSpec((1, D_FF, nc), lambda i, j, te, nu, tr: (te[i], 0, nj + frozen(i, j, nu, nj - 1))),
            pl.BlockSpec((1, 1, nc), lambda i, j, te, nu, tr: (te[i], 0, frozen(i, j, nu, nj - 1))),
            pl.BlockSpec((1, 1, nc), lambda i, j, te, nu, tr: (te[i], 0, nj + frozen(i, j, nu, nj - 1))),
        ],
        out_specs=pl.BlockSpec((tm, nc), lambda i, j, te, nu, tr: (live(i, nu), frozen(i, j, nu, nj - 1))),
    )
    return pl.pallas_call(
        functools.partial(_ffn_down_kernel, tm=tm, sub=sub),
        grid_spec=down_spec,
        out_shape=jax.ShapeDtypeStruct((n_slots, HALF), jnp.uint32),
        compiler_params=_cparams(("arbitrary", "arbitrary")),
        name="ffn_down",
    )(tile_expert, n_used, tile_rows, act, w_down, w_down, b_down3, b_down3)


def _combine_kernel(d_ref, h_ref, g_ref, nw_ref, yb_ref, o_ref, ybuf, sem, *, tm):
    def body(g, carry):
        r0 = pl.multiple_of(g * ROW_GROUP, ROW_GROUP)
        for i in range(ROW_GROUP):
            for k in range(TOP_K):
                d = d_ref[(r0 + i) * TOP_K + k]
                pltpu.make_async_copy(yb_ref.at[pl.ds(d, 1)], ybuf.at[k, pl.ds(r0 + i, 1)], sem).start()
        return carry

    lax.fori_loop(0, tm // ROW_GROUP, body, 0)
    for k in range(TOP_K):
        pltpu.make_async_copy(yb_ref.at[pl.ds(0, tm)], ybuf.at[k], sem).wait()
    lo = h_ref[:, :HALF]
    hi = h_ref[:, HALF:]
    for k in range(TOP_K):
        w = ybuf[k]
        g = g_ref[:, k:k + 1]
        lo = lo + g * _unpack_lo(w)
        hi = hi + g * _unpack_hi(w)
    ms = (jnp.sum(lo * lo, axis=-1, keepdims=True) + jnp.sum(hi * hi, axis=-1, keepdims=True)) / D_MODEL
    scale = lax.rsqrt(ms + EPS)
    o_ref[:, :HALF] = lo * scale * nw_ref[:, :HALF]
    o_ref[:, HALF:] = hi * scale * nw_ref[:, HALF:]


def _combine(dest, h, gates, nw, yb, tm):
    T = h.shape[0]
    return pl.pallas_call(
        functools.partial(_combine_kernel, tm=tm),
        grid=(T // tm,),
        in_specs=[
            pl.BlockSpec((tm * TOP_K,), lambda i: (i,), memory_space=pltpu.SMEM),
            pl.BlockSpec((tm, D_MODEL), lambda i: (i, 0)),
            pl.BlockSpec((tm, LANES), lambda i: (i, 0)),
            pl.BlockSpec((1, D_MODEL), lambda i: (0, 0)),
            pl.BlockSpec(memory_space=pl.ANY),
        ],
        out_specs=pl.BlockSpec((tm, D_MODEL), lambda i: (i, 0)),
        out_shape=jax.ShapeDtypeStruct((T, D_MODEL), F32),
        scratch_shapes=[pltpu.VMEM((TOP_K, tm, HALF), jnp.uint32), pltpu.SemaphoreType.DMA(())],
        compiler_params=_cparams(("arbitrary",)),
        name="combine",
    )(dest, h, gates, nw, yb)


def _row(v, width=None):
    v = v.astype(F32).reshape(1, -1)
    if width is not None and v.shape[1] < width:
        v = jnp.pad(v, ((0, 0), (0, width - v.shape[1])))
    return v


def _tile(n, pref):
    t = pref
    while n % t:
        t //= 2
    return t


def kernel(x, norm_mix_w, w_in, w_pool, pool_scale, conv_w, conv_b, dt_bias, a_log, d_skip,
           ssd_norm_w, w_ssd_out, w_out, norm_ffn_w, router_w, router_b, w_up, b_up,
           w_down, b_down, norm_final_w):
    B, S, D = x.shape
    T = B * S
    x2 = x.reshape(T, D)

    o_p, o_z, o_xbc = 0, D_MODEL, D_MODEL + SSD_INNER
    o_dt = o_xbc + SSD_INNER + 2 * SSD_BC
    o_gp = o_dt + SSD_HEADS
    o_gs = o_gp + D_MODEL
    w_main = jnp.concatenate(
        [w_in[:, o_z:o_z + SSD_INNER], w_in[:, o_xbc:o_dt], w_in[:, o_p:o_p + D_MODEL],
         w_in[:, o_gp:o_gp + D_MODEL], w_in[:, o_gs:o_gs + D_MODEL]], axis=1).astype(BF16)
    w_dt = jnp.pad(w_in[:, o_dt:o_dt + SSD_HEADS], ((0, 0), (0, LANES - SSD_HEADS))).astype(BF16)

    proj, dt_raw = _inproj(x2, _row(norm_mix_w), w_main, w_dt, _tile(T, 1024), 2048)

    pool_g = _pool(proj, w_pool.astype(BF16), _row(pool_scale), _tile(S, 256), S)

    yn = _ssd(proj, dt_raw,
              conv_w[:, :SSD_INNER].astype(F32), _row(conv_b[:SSD_INNER]),
              conv_w[:, SSD_INNER:].astype(F32), _row(conv_b[SSD_INNER:]),
              _row(dt_bias, LANES), _row(a_log, LANES),
              _row(jnp.repeat(d_skip, SSD_HEADDIM)), _row(ssd_norm_w), B, S)

    rw = jnp.pad(router_w.astype(F32), ((0, 0), (0, LANES - N_EXPERTS)))
    rw_hi = rw.astype(BF16)
    rw_lo = (rw - rw_hi.astype(F32)).astype(BF16)
    h, u_pk, logits = _mixout(pool_g, yn, proj, x2, w_ssd_out.astype(BF16), w_out.astype(BF16),
                              _row(norm_ffn_w), jnp.concatenate([rw_hi, rw_lo], axis=1),
                              _row(router_b, LANES), _tile(T, 256))

    er, gates, cnt = _router(logits, _tile(T, 256))

    TK = T * TOP_K
    tm_ffn = 2304 if TK >= 32768 else 256
    sub = 256
    n_tiles = -(-TK // tm_ffn) + N_EXPERTS
    counts = cnt[0, :N_EXPERTS].astype(jnp.int32)
    tiles_e = (counts + tm_ffn - 1) // tm_ffn
    tend = jnp.cumsum(tiles_e)
    tstart = tend - tiles_e
    pstart = (tstart * tm_ffn).astype(jnp.int32)
    tile_id = jnp.arange(n_tiles, dtype=jnp.int32)
    tile_expert = jnp.minimum(jnp.searchsorted(tend, tile_id, side='right'), N_EXPERTS - 1).astype(jnp.int32)
    tile_rows = jnp.clip(counts[tile_expert] - (tile_id - tstart[tile_expert]) * tm_ffn, 0, tm_ffn).astype(jnp.int32)
    n_used = tend[-1].astype(jnp.int32).reshape(1)
    e_flat = er[:, :TOP_K].reshape(TK)
    r_flat = er[:, TOP_K:2 * TOP_K].reshape(TK)

    xb, dest = _dispatch(pstart, e_flat, r_flat, u_pk, n_tiles * tm_ffn, _tile(T, 512))
    yb = _ffn(tile_expert, n_used, tile_rows, xb, w_up, b_up.reshape(N_EXPERTS, 1, 2 * D_FF),
              w_down, b_down.reshape(N_EXPERTS, 1, D_MODEL), tm_ffn, sub, 512, 512)
    out = _combine(dest, h, gates, _row(norm_final_w), yb, _tile(T, 256))
    return out.reshape(B, S, D)
```

```python
import functools

import jax
import jax.numpy as jnp
from jax import lax
from jax.experimental import pallas as pl
from jax.experimental.pallas import tpu as pltpu

F32 = jnp.float32
BF16 = jnp.bfloat16

D_MODEL = 2048
EPS = 1e-5
POOL_GROUPS = 4
POOL_WINDOWS = (2, 4, 8, 16)
POOL_GW = D_MODEL // POOL_GROUPS
SSD_INNER = 2 * D_MODEL
SSD_HEADDIM = 64
SSD_HEADS = SSD_INNER // SSD_HEADDIM
SSD_GROUPS = 8
SSD_HPG = SSD_HEADS // SSD_GROUPS
SSD_STATE = 128
SSD_CONV = 4
SSD_CHUNK = 128
SSD_GW = SSD_HPG * SSD_HEADDIM
SSD_BC = SSD_GROUPS * SSD_STATE
N_EXPERTS = 32
TOP_K = 4
D_FF = D_MODEL
SWIGLU_ALPHA = 1.702
SWIGLU_LIMIT = 7.0

LANES = 128
HALF = D_MODEL // 2
CONV_SLAB = 256
QUAD = 4
QUAD_W = QUAD * SSD_HEADDIM
FFN_BIG_BLOCK = 1024
ROW_GROUP = 8
VMEM_LIMIT = 56 * 1024 * 1024

COL_Z = 0
COL_XS = SSD_INNER
COL_BC = 2 * SSD_INNER
COL_P = 2 * SSD_INNER + 2 * SSD_BC
COL_GP = COL_P + D_MODEL
COL_GS = COL_GP + D_MODEL
PROJ_COLS = COL_GS + D_MODEL

NEG_BIG = -1e30


def _cparams(sem):
    return pltpu.CompilerParams(dimension_semantics=sem, vmem_limit_bytes=VMEM_LIMIT)


def _inproj_kernel(x_ref, nw_ref, w_ref, wdt_ref, o_ref, dt_ref, u_scr):
    @pl.when(pl.program_id(1) == 0)
    def _():
        x = x_ref[...]
        ms = jnp.mean(x * x, axis=-1, keepdims=True)
        u = (x * lax.rsqrt(ms + EPS) * nw_ref[...]).astype(BF16)
        u_scr[...] = u
        dt_ref[...] = jnp.dot(u, wdt_ref[...], preferred_element_type=F32)

    o_ref[...] = jnp.dot(u_scr[...], w_ref[...], preferred_element_type=F32).astype(o_ref.dtype)


def _inproj(x2, norm_w, w_main, w_dt, tm, tn):
    T = x2.shape[0]
    return pl.pallas_call(
        _inproj_kernel,
        grid=(T // tm, PROJ_COLS // tn),
        in_specs=[
            pl.BlockSpec((tm, D_MODEL), lambda i, j: (i, 0)),
            pl.BlockSpec((1, D_MODEL), lambda i, j: (0, 0)),
            pl.BlockSpec((D_MODEL, tn), lambda i, j: (0, j)),
            pl.BlockSpec((D_MODEL, LANES), lambda i, j: (0, 0)),
        ],
        out_specs=[
            pl.BlockSpec((tm, tn), lambda i, j: (i, j)),
            pl.BlockSpec((tm, LANES), lambda i, j: (i, 0)),
        ],
        out_shape=[
            jax.ShapeDtypeStruct((T, PROJ_COLS), BF16),
            jax.ShapeDtypeStruct((T, LANES), F32),
        ],
        scratch_shapes=[pltpu.VMEM((tm, D_MODEL), BF16)],
        compiler_params=_cparams(("parallel", "arbitrary")),
        name="inproj",
    )(x2, norm_w, w_main, w_dt)


def _pool_kernel(prev_ref, p_ref, g_ref, band_ref, hband_ref, wp_ref, ps_ref, o_ref, *, tm, seq):
    row0 = (pl.program_id(0) * tm) % seq
    pos = row0 + lax.broadcasted_iota(jnp.int32, (tm, 1), 0)
    for g, w in enumerate(POOL_WINDOWS):
        cols = slice(g * POOL_GW, (g + 1) * POOL_GW)
        cur = p_ref[:, cols]
        prev = prev_ref[:, cols]
        prev = jnp.where(row0 == 0, jnp.zeros_like(prev), prev)
        acc = (jnp.dot(band_ref[g], cur, preferred_element_type=F32)
               + jnp.dot(hband_ref[g], prev, preferred_element_type=F32))
        cnt = jnp.minimum(pos + 1, w).astype(F32)
        pooled = acc / cnt - cur.astype(F32)
        y = jnp.dot(pooled.astype(BF16), wp_ref[g], preferred_element_type=F32)
        y = y * ps_ref[:, cols] * jax.nn.sigmoid(g_ref[:, cols].astype(F32))
        o_ref[:, cols] = y.astype(o_ref.dtype)


def _pool(proj, w_pool, pool_scale, tm, seq):
    T = proj.shape[0]
    hist = max(POOL_WINDOWS)
    t_out = jnp.arange(tm, dtype=jnp.int32)[None, :, None]
    src = jnp.arange(hist + tm, dtype=jnp.int32)[None, None, :]
    win = jnp.asarray(POOL_WINDOWS, jnp.int32)[:, None, None]
    band = jnp.logical_and(src > hist + t_out - win, src <= hist + t_out).astype(BF16)
    return pl.pallas_call(
        functools.partial(_pool_kernel, tm=tm, seq=seq),
        grid=(T // tm,),
        in_specs=[
            pl.BlockSpec((hist, D_MODEL), lambda i: (jnp.maximum(i * (tm // hist) - 1, 0), COL_P // D_MODEL)),
            pl.BlockSpec((tm, D_MODEL), lambda i: (i, COL_P // D_MODEL)),
            pl.BlockSpec((tm, D_MODEL), lambda i: (i, COL_GP // D_MODEL)),
            pl.BlockSpec((POOL_GROUPS, tm, tm), lambda i: (0, 0, 0)),
            pl.BlockSpec((POOL_GROUPS, tm, hist), lambda i: (0, 0, 0)),
            pl.BlockSpec((POOL_GROUPS, POOL_GW, POOL_GW), lambda i: (0, 0, 0)),
            pl.BlockSpec((1, D_MODEL), lambda i: (0, 0)),
        ],
        out_specs=pl.BlockSpec((tm, D_MODEL), lambda i: (i, 0)),
        out_shape=jax.ShapeDtypeStruct((T, D_MODEL), BF16),
        compiler_params=_cparams(("parallel",)),
        name="pool",
    )(proj, proj, proj, band[:, :, hist:], band[:, :, :hist], w_pool, pool_scale)


def _split3(v):
    hi = v.astype(BF16)
    r1 = v - hi.astype(F32)
    mid = r1.astype(BF16)
    lo = (r1 - mid.astype(F32)).astype(BF16)
    return hi, mid, lo


def _ssd_kernel(z_ref, xs_ref, bc_ref, dt_ref, cwx_ref, cbx_ref, cwbc_ref, cbbc_ref,
                dtb_ref, alog_ref, dsk_ref, nw_ref, qm_ref, sh_ref, o_ref,
                prev_x, prev_bc, xs_scr, b_scr, c_scr, a_scr, qt_scr, wt_scr, state):
    L = SSD_CHUNK
    c = pl.program_id(1)

    @pl.when(c == 0)
    def _():
        prev_x[...] = jnp.zeros_like(prev_x)
        prev_bc[...] = jnp.zeros_like(prev_bc)
        state[...] = jnp.zeros_like(state)

    def conv_slab(src_ref, prev_ref, cw_ref, cb_ref, col):
        cols = slice(col, col + CONV_SLAB)
        raw = src_ref[:, cols]
        both = jnp.concatenate([prev_ref[:, cols], raw], axis=0)
        prev_ref[:, cols] = raw
        acc = cb_ref[:, cols] + cw_ref[SSD_CONV - 1:SSD_CONV, cols] * raw.astype(F32)
        for k in range(SSD_CONV - 1):
            shifted = jnp.dot(sh_ref[k * L:(k + 1) * L, :], both, preferred_element_type=F32)
            acc = acc + cw_ref[k:k + 1, cols] * shifted
        return acc * jax.nn.sigmoid(acc)

    for s in range(SSD_INNER // CONV_SLAB):
        xs_scr[:, s * CONV_SLAB:(s + 1) * CONV_SLAB] = conv_slab(xs_ref, prev_x, cwx_ref, cbx_ref, s * CONV_SLAB)
    for s in range(SSD_BC // CONV_SLAB):
        b_scr[:, s * CONV_SLAB:(s + 1) * CONV_SLAB] = conv_slab(bc_ref, prev_bc, cwbc_ref, cbbc_ref, s * CONV_SLAB)
    for s in range(SSD_BC // CONV_SLAB):
        c_scr[:, s * CONV_SLAB:(s + 1) * CONV_SLAB] = conv_slab(
            bc_ref, prev_bc, cwbc_ref, cbbc_ref, SSD_BC + s * CONV_SLAB).astype(BF16)

    dtv = jax.nn.softplus(dt_ref[...] + dtb_ref[...])
    d_a = dtv * (-jnp.exp(alog_ref[...]))
    rr = lax.broadcasted_iota(jnp.int32, (L, L), 0)
    cc = lax.broadcasted_iota(jnp.int32, (L, L), 1)
    causal = rr >= cc
    low_half = cc < SSD_HEADDIM
    tri = jnp.where(causal, 1.0, 0.0).astype(BF16)
    a_cum = sum(jnp.dot(tri, part, preferred_element_type=F32) for part in _split3(d_a))
    a_t = a_cum.T
    dt_t = dtv.T
    a_scr[...] = a_cum
    qt_scr[...] = a_t - jnp.log(dt_t)
    wt_scr[...] = dt_t * jnp.exp(a_t[:, L - 1:L] - a_t)

    def group_body(g, carry):
        off = pl.multiple_of(g * SSD_GW, SSD_GW)
        offn = pl.multiple_of(g * SSD_STATE, SSD_STATE)
        xg = xs_scr[:, pl.ds(off, SSD_GW)]
        xg_b = xg.astype(BF16)
        b_f = b_scr[:, pl.ds(offn, SSD_STATE)]
        c_b = c_scr[:, pl.ds(offn, SSD_STATE)]
        b_t = b_f.T
        cbm = lax.dot_general(c_b, b_f.astype(BF16), (((1,), (1,)), ((), ())),
                              preferred_element_type=F32)
        s_g = state[g]
        y_off = jnp.dot(c_b, s_g.astype(BF16), preferred_element_type=F32)
        a_g = pltpu.roll(a_scr[...], (LANES - SSD_HPG * g) % LANES, 1)
        ys = []
        for quad in range(SSD_HPG // QUAD):
            qs = slice(quad * QUAD_W, (quad + 1) * QUAD_W)
            m_parts, b_parts, a_cols = [], [], []
            for jj in range(QUAD):
                j = quad * QUAD + jj
                h = g * SSD_HPG + j
                a_col = jnp.broadcast_to(a_g[:, j:j + 1], (L, L))
                seg = a_col - qt_scr[pl.ds(h, 1), :]
                m_parts.append((cbm * jnp.exp(jnp.where(causal, seg, NEG_BIG))).astype(BF16))
                b_parts.append((b_t * wt_scr[pl.ds(h, 1), :]).astype(BF16))
                a_cols.append(a_col)
            lhs = jnp.concatenate([jnp.concatenate(m_parts, axis=1), jnp.concatenate(b_parts, axis=1)], axis=0)
            xq = xg_b[:, qs]
            rhs = jnp.concatenate([xq * qm_ref[jj] for jj in range(QUAD)], axis=0)
            res = jnp.dot(lhs, rhs, preferred_element_type=F32)
            ea_q = jnp.concatenate(
                [jnp.exp(jnp.where(low_half, a_cols[2 * p], a_cols[2 * p + 1])) for p in range(QUAD // 2)], axis=1)
            ys.append(res[:L] + y_off[:, qs] * ea_q)
            state[g, :, qs] = s_g[:, qs] * ea_q[L - 1:L, :] + res[L:]
        y = jnp.concatenate(ys, axis=1) + dsk_ref[:, pl.ds(off, SSD_GW)] * xg
        zg = z_ref[:, pl.ds(off, SSD_GW)].astype(F32)
        yg = y * (zg * jax.nn.sigmoid(zg))
        ms = jnp.mean(yg * yg, axis=-1, keepdims=True)
        yn = yg * lax.rsqrt(ms + EPS) * nw_ref[:, pl.ds(off, SSD_GW)]
        o_ref[:, pl.ds(off, SSD_GW)] = yn.astype(o_ref.dtype)
        return carry

    def group_pair(gp, carry):
        group_body(2 * gp, carry)
        return group_body(2 * gp + 1, carry)

    lax.fori_loop(0, SSD_GROUPS // 2, group_pair, 0)


def _ssd(proj, dt_raw, cwx, cbx, cwbc, cbbc, dtb, alog, dsk, nw, batch, seq):
    L = SSD_CHUNK
    nc = seq // L
    T = proj.shape[0]
    row = lambda b, c: b * nc + c
    const = lambda b, c: (0, 0)
    head_of_lane = jnp.arange(QUAD_W, dtype=jnp.int32) // SSD_HEADDIM
    qmask = (head_of_lane[None, None, :] == jnp.arange(QUAD, dtype=jnp.int32)[:, None, None])
    qmask = jnp.broadcast_to(qmask, (QUAD, L, QUAD_W)).astype(BF16)
    tap = jnp.arange(SSD_CONV - 1, dtype=jnp.int32)[:, None, None]
    t_out = jnp.arange(L, dtype=jnp.int32)[None, :, None]
    src = jnp.arange(2 * L, dtype=jnp.int32)[None, None, :]
    shift = (src == L + t_out - (SSD_CONV - 1) + tap).astype(BF16).reshape((SSD_CONV - 1) * L, 2 * L)
    return pl.pallas_call(
        _ssd_kernel,
        grid=(batch, nc),
        in_specs=[
            pl.BlockSpec((L, SSD_INNER), lambda b, c: (row(b, c), COL_Z // SSD_INNER)),
            pl.BlockSpec((L, SSD_INNER), lambda b, c: (row(b, c), COL_XS // SSD_INNER)),
            pl.BlockSpec((L, 2 * SSD_BC), lambda b, c: (row(b, c), COL_BC // (2 * SSD_BC))),
            pl.BlockSpec((L, LANES), lambda b, c: (row(b, c), 0)),
            pl.BlockSpec((SSD_CONV, SSD_INNER), const),
            pl.BlockSpec((1, SSD_INNER), const),
            pl.BlockSpec((SSD_CONV, 2 * SSD_BC), const),
            pl.BlockSpec((1, 2 * SSD_BC), const),
            pl.BlockSpec((1, LANES), const),
            pl.BlockSpec((1, LANES), const),
            pl.BlockSpec((1, SSD_INNER), const),
            pl.BlockSpec((1, SSD_INNER), const),
            pl.BlockSpec((QUAD, L, QUAD_W), lambda b, c: (0, 0, 0)),
            pl.BlockSpec(((SSD_CONV - 1) * L, 2 * L), const),
        ],
        out_specs=pl.BlockSpec((L, SSD_INNER), lambda b, c: (row(b, c), 0)),
        out_shape=jax.ShapeDtypeStruct((T, SSD_INNER), BF16),
        scratch_shapes=[
            pltpu.VMEM((L, SSD_INNER), BF16),
            pltpu.VMEM((L, 2 * SSD_BC), BF16),
            pltpu.VMEM((L, SSD_INNER), F32),
            pltpu.VMEM((L, SSD_BC), F32),
            pltpu.VMEM((L, SSD_BC), BF16),
            pltpu.VMEM((L, LANES), F32),
            pltpu.VMEM((LANES, L), F32),
            pltpu.VMEM((LANES, L), F32),
            pltpu.VMEM((SSD_GROUPS, SSD_STATE, SSD_GW), F32),
        ],
        compiler_params=_cparams(("arbitrary", "arbitrary")),
        name="ssd",
    )(proj, proj, proj, dt_raw, cwx, cbx, cwbc, cbbc, dtb, alog, dsk, nw, qmask, shift)


def _pack_bf16_pairs(v):
    bits = pltpu.bitcast(v.astype(BF16).astype(F32), jnp.uint32)
    return (bits[:, :HALF] >> 16) | (bits[:, HALF:] & jnp.uint32(0xFFFF0000))


def _unpack_lo(w):
    return pltpu.bitcast(w << 16, F32)


def _unpack_hi(w):
    return pltpu.bitcast(w & jnp.uint32(0xFFFF0000), F32)


def _mixout_kernel(pg_ref, yn_ref, gs_ref, x_ref, wso_ref, wo_ref, nw_ref, rw_ref, rb_ref,
                   h_ref, u_ref, er_ref, gt_ref, cnt_ref, run_scr, *, tm):
    @pl.when(pl.program_id(0) == 0)
    def _():
        run_scr[...] = jnp.zeros_like(run_scr)

    ssd = jnp.dot(yn_ref[...], wso_ref[...], preferred_element_type=F32)
    merged = pg_ref[...].astype(F32) + jax.nn.sigmoid(gs_ref[...].astype(F32)) * ssd
    h = x_ref[...] + jnp.dot(merged.astype(BF16), wo_ref[...], preferred_element_type=F32)
    h_ref[...] = h
    ms = jnp.mean(h * h, axis=-1, keepdims=True)
    u = h * lax.rsqrt(ms + EPS) * nw_ref[...]
    u_hi = u.astype(BF16)
    u_ref[...] = _pack_bf16_pairs(u)
    u_lo = (u - u_hi.astype(F32)).astype(BF16)
    both = jnp.dot(u_hi, rw_ref[...], preferred_element_type=F32)
    lg = both[:, :LANES] + both[:, LANES:] + jnp.dot(u_lo, rw_ref[:, :LANES], preferred_element_type=F32)
    er, gt = _route(lg + rb_ref[...], run_scr, tm)
    er_ref[...] = er
    gt_ref[...] = gt
    cnt_ref[...] = run_scr[...]


def _mixout(pool_g, yn, proj, x2, wso, wo, nw, rw, rb, tm):
    T = x2.shape[0]
    const = lambda i: (0, 0)
    return pl.pallas_call(
        functools.partial(_mixout_kernel, tm=tm),
        grid=(T // tm,),
        in_specs=[
            pl.BlockSpec((tm, D_MODEL), lambda i: (i, 0)),
            pl.BlockSpec((tm, SSD_INNER), lambda i: (i, 0)),
            pl.BlockSpec((tm, D_MODEL), lambda i: (i, COL_GS // D_MODEL)),
            pl.BlockSpec((tm, D_MODEL), lambda i: (i, 0)),
            pl.BlockSpec((SSD_INNER, D_MODEL), const),
            pl.BlockSpec((D_MODEL, D_MODEL), const),
            pl.BlockSpec((1, D_MODEL), const),
            pl.BlockSpec((D_MODEL, 2 * LANES), const),
            pl.BlockSpec((1, LANES), const),
        ],
        out_specs=[
            pl.BlockSpec((tm, D_MODEL), lambda i: (i, 0)),
            pl.BlockSpec((tm, HALF), lambda i: (i, 0)),
            pl.BlockSpec((tm, LANES), lambda i: (i, 0)),
            pl.BlockSpec((tm, LANES), lambda i: (i, 0)),
            pl.BlockSpec((1, LANES), const),
        ],
        out_shape=[
            jax.ShapeDtypeStruct((T, D_MODEL), F32),
            jax.ShapeDtypeStruct((T, HALF), jnp.uint32),
            jax.ShapeDtypeStruct((T, LANES), jnp.int32),
            jax.ShapeDtypeStruct((T, LANES), F32),
            jax.ShapeDtypeStruct((1, LANES), F32),
        ],
        scratch_shapes=[pltpu.VMEM((1, LANES), F32)],
        compiler_params=_cparams(("arbitrary",)),
        name="mixout",
    )(pool_g, yn, proj, x2, wso, wo, nw, rw, rb)


def _route(lg, run_scr, tm):
    lane = lax.broadcasted_iota(jnp.int32, (tm, LANES), 1)
    lg = jnp.where(lane < N_EXPERTS, lg, -jnp.inf)
    idxs, vals, hots = [], [], []
    for _ in range(TOP_K):
        m = jnp.max(lg, axis=-1, keepdims=True)
        idx = jnp.min(jnp.where(lg == m, lane, LANES), axis=-1, keepdims=True)
        hot = lane == idx
        lg = jnp.where(hot, -jnp.inf, lg)
        idxs.append(idx)
        vals.append(m)
        hots.append(hot)
    exps = [jnp.exp(v - vals[0]) for v in vals]
    den = exps[0] + exps[1] + exps[2] + exps[3]
    cnt = sum(jnp.where(hot, 1.0, 0.0) for hot in hots)
    rr = lax.broadcasted_iota(jnp.int32, (tm, tm), 0)
    cc = lax.broadcasted_iota(jnp.int32, (tm, tm), 1)
    before = jnp.where(rr > cc, 1.0, 0.0).astype(BF16)
    base = jnp.dot(before, cnt.astype(BF16), preferred_element_type=F32) + run_scr[...]
    er = jnp.zeros((tm, LANES), jnp.int32)
    gt = jnp.zeros((tm, LANES), F32)
    for k in range(TOP_K):
        rank = jnp.sum(jnp.where(hots[k], base, 0.0), axis=-1, keepdims=True)
        er = jnp.where(lane == k, idxs[k], er)
        er = jnp.where(lane == TOP_K + k, rank.astype(jnp.int32), er)
        gt = jnp.where(lane == k, exps[k] / den, gt)
    run_scr[...] += jnp.sum(cnt, axis=0, keepdims=True)
    return er, gt


def _dispatch_kernel(ps_ref, e_ref, r_ref, u_ref, xb_ref, dest_ref, sem, *, tm):
    def body(g, carry):
        r0 = pl.multiple_of(g * ROW_GROUP, ROW_GROUP)
        for i in range(ROW_GROUP):
            for k in range(TOP_K):
                j = (r0 + i) * TOP_K + k
                d = ps_ref[e_ref[j]] + r_ref[j]
                dest_ref[j] = d
                pltpu.make_async_copy(u_ref.at[pl.ds(r0 + i, 1)], xb_ref.at[pl.ds(d, 1)], sem).start()
        return carry

    lax.fori_loop(0, tm // ROW_GROUP, body, 0)
    for _ in range(TOP_K):
        pltpu.make_async_copy(u_ref, xb_ref.at[pl.ds(0, tm)], sem).wait()


def _dispatch(pstart, e_flat, r_flat, u_pk, n_slots, tm):
    T = u_pk.shape[0]
    grid_spec = pltpu.PrefetchScalarGridSpec(
        num_scalar_prefetch=1,
        grid=(T // tm,),
        in_specs=[
            pl.BlockSpec((tm * TOP_K,), lambda i, ps: (i,), memory_space=pltpu.SMEM),
            pl.BlockSpec((tm * TOP_K,), lambda i, ps: (i,), memory_space=pltpu.SMEM),
            pl.BlockSpec((tm, HALF), lambda i, ps: (i, 0)),
        ],
        out_specs=[
            pl.BlockSpec(memory_space=pl.ANY),
            pl.BlockSpec((tm * TOP_K,), lambda i, ps: (i,), memory_space=pltpu.SMEM),
        ],
        scratch_shapes=[pltpu.SemaphoreType.DMA(())],
    )
    return pl.pallas_call(
        functools.partial(_dispatch_kernel, tm=tm),
        grid_spec=grid_spec,
        out_shape=[
            jax.ShapeDtypeStruct((n_slots, HALF), jnp.uint32),
            jax.ShapeDtypeStruct((T * TOP_K,), jnp.int32),
        ],
        compiler_params=_cparams(("arbitrary",)),
        name="dispatch",
    )(pstart, e_flat, r_flat, u_pk)


def _row_blocks(nrows, tm, sub, weights, block):
    nblk = (nrows + sub - 1) // sub
    per_big = min(FFN_BIG_BLOCK, tm) // sub
    nbig = tm // (per_big * sub)
    for b in range(nbig):
        @pl.when(nblk >= (b + 1) * per_big)
        def _():
            block(b * per_big * sub, per_big * sub, weights())

    def rest(sb, carry):
        block(pl.multiple_of(sb * sub, sub), sub, weights())
        return carry

    lax.fori_loop(jnp.minimum(nblk // per_big, nbig) * per_big, nblk, rest, 0)


def _ffn_up_kernel(te_ref, nu_ref, tr_ref, x_ref, wg_ref, wl_ref, bg_ref, bl_ref, o_ref, *, tm, sub):
    nrows = tr_ref[pl.program_id(0)]

    def weights():
        return wg_ref[0].astype(BF16), wl_ref[0].astype(BF16)

    def block(r0, nr, w):
        packed = x_ref[pl.ds(r0, nr), :]
        x = jnp.concatenate([_unpack_lo(packed).astype(BF16), _unpack_hi(packed).astype(BF16)], axis=1)
        glu = jnp.minimum(jnp.dot(x, w[0], preferred_element_type=F32) + bg_ref[0], SWIGLU_LIMIT)
        lin = jnp.clip(jnp.dot(x, w[1], preferred_element_type=F32) + bl_ref[0], -SWIGLU_LIMIT, SWIGLU_LIMIT)
        act = glu * jax.nn.sigmoid(SWIGLU_ALPHA * glu) * (lin + 1.0)
        o_ref[pl.ds(r0, nr), :] = act.astype(o_ref.dtype)

    _row_blocks(nrows, tm, sub, weights, block)


def _ffn_down_kernel(te_ref, nu_ref, tr_ref, a_ref, wa_ref, wb_ref, ba_ref, bb_ref, o_ref, *, tm, sub):
    nrows = tr_ref[pl.program_id(0)]

    def weights():
        return wa_ref[0].astype(BF16), wb_ref[0].astype(BF16)

    def block(r0, nr, w):
        a = a_ref[pl.ds(r0, nr), :]
        ya = jnp.dot(a, w[0], preferred_element_type=F32) + ba_ref[0]
        yb = jnp.dot(a, w[1], preferred_element_type=F32) + bb_ref[0]
        lo = pltpu.bitcast(ya.astype(BF16).astype(F32), jnp.uint32) >> 16
        hi = pltpu.bitcast(yb.astype(BF16).astype(F32), jnp.uint32) & jnp.uint32(0xFFFF0000)
        o_ref[pl.ds(r0, nr), :] = lo | hi

    _row_blocks(nrows, tm, sub, weights, block)


def _ffn(tile_expert, n_used, tile_rows, xb, w_up, b_up3, w_down, b_down3, tm, sub, fc, nc):
    n_slots = xb.shape[0]
    n_tiles = n_slots // tm
    nf = D_FF // fc
    nj = HALF // nc

    def live(i, nu):
        return jnp.minimum(i, nu[0] - 1)

    def frozen(i, j, nu, last):
        return jnp.where(i < nu[0], j, last)

    up_spec = pltpu.PrefetchScalarGridSpec(
        num_scalar_prefetch=3,
        grid=(n_tiles, nf),
        in_specs=[
            pl.BlockSpec((tm, HALF), lambda i, f, te, nu, tr: (live(i, nu), 0)),
            pl.BlockSpec((1, D_MODEL, fc), lambda i, f, te, nu, tr: (te[i], 0, frozen(i, f, nu, nf - 1))),
            pl.BlockSpec((1, D_MODEL, fc), lambda i, f, te, nu, tr: (te[i], 0, nf + frozen(i, f, nu, nf - 1))),
            pl.BlockSpec((1, 1, fc), lambda i, f, te, nu, tr: (te[i], 0, frozen(i, f, nu, nf - 1))),
            pl.BlockSpec((1, 1, fc), lambda i, f, te, nu, tr: (te[i], 0, nf + frozen(i, f, nu, nf - 1))),
        ],
        out_specs=pl.BlockSpec((tm, fc), lambda i, f, te, nu, tr: (live(i, nu), frozen(i, f, nu, nf - 1))),
    )
    act = pl.pallas_call(
        functools.partial(_ffn_up_kernel, tm=tm, sub=sub),
        grid_spec=up_spec,
        out_shape=jax.ShapeDtypeStruct((n_slots, D_FF), BF16),
        compiler_params=_cparams(("arbitrary", "arbitrary")),
        name="ffn_up",
    )(tile_expert, n_used, tile_rows, xb, w_up, w_up, b_up3, b_up3)

    down_spec = pltpu.PrefetchScalarGridSpec(
        num_scalar_prefetch=3,
        grid=(n_tiles, nj),
        in_specs=[
            pl.BlockSpec((tm, D_FF), lambda i, j, te, nu, tr: (live(i, nu), 0)),
            pl.BlockSpec((1, D_FF, nc), lambda i, j, te, nu, tr: (te[i], 0, frozen(i, j, nu, nj - 1))),
            pl.BlockSpec((1, D_FF, nc), lambda i, j, te, nu, tr: (te[i], 0, nj + frozen(i, j, nu, nj - 1))),
            pl.BlockSpec((1, 1, nc), lambda i, j, te, nu, tr: (te[i], 0, frozen(i, j, nu, nj - 1))),
            pl.BlockSpec((1, 1, nc), lambda i, j, te, nu, tr: (te[i], 0, nj + frozen(i, j, nu, nj - 1))),
        ],
        out_specs=pl.BlockSpec((tm, nc), lambda i, j, te, nu, tr: (live(i, nu), frozen(i, j, nu, nj - 1))),
    )
    return pl.pallas_call(
        functools.partial(_ffn_down_kernel, tm=tm, sub=sub),
        grid_spec=down_spec,
        out_shape=jax.ShapeDtypeStruct((n_slots, HALF), jnp.uint32),
        compiler_params=_cparams(("arbitrary", "arbitrary")),
        name="ffn_down",
    )(tile_expert, n_used, tile_rows, act, w_down, w_down, b_down3, b_down3)


def _combine_kernel(d_ref, h_ref, g_ref, nw_ref, yb_ref, o_ref, ybuf, sem, *, tm):
    def body(g, carry):
        r0 = pl.multiple_of(g * ROW_GROUP, ROW_GROUP)
        for i in range(ROW_GROUP):
            for k in range(TOP_K):
                d = d_ref[(r0 + i) * TOP_K + k]
                pltpu.make_async_copy(yb_ref.at[pl.ds(d, 1)], ybuf.at[k, pl.ds(r0 + i, 1)], sem).start()
        return carry

    lax.fori_loop(0, tm // ROW_GROUP, body, 0)
    for k in range(TOP_K):
        pltpu.make_async_copy(yb_ref.at[pl.ds(0, tm)], ybuf.at[k], sem).wait()
    lo = h_ref[:, :HALF]
    hi = h_ref[:, HALF:]
    for k in range(TOP_K):
        w = ybuf[k]
        g = g_ref[:, k:k + 1]
        lo = lo + g * _unpack_lo(w)
        hi = hi + g * _unpack_hi(w)
    ms = (jnp.sum(lo * lo, axis=-1, keepdims=True) + jnp.sum(hi * hi, axis=-1, keepdims=True)) / D_MODEL
    scale = lax.rsqrt(ms + EPS)
    o_ref[:, :HALF] = lo * scale * nw_ref[:, :HALF]
    o_ref[:, HALF:] = hi * scale * nw_ref[:, HALF:]


def _combine(dest, h, gates, nw, yb, tm):
    T = h.shape[0]
    return pl.pallas_call(
        functools.partial(_combine_kernel, tm=tm),
        grid=(T // tm,),
        in_specs=[
            pl.BlockSpec((tm * TOP_K,), lambda i: (i,), memory_space=pltpu.SMEM),
            pl.BlockSpec((tm, D_MODEL), lambda i: (i, 0)),
            pl.BlockSpec((tm, LANES), lambda i: (i, 0)),
            pl.BlockSpec((1, D_MODEL), lambda i: (0, 0)),
            pl.BlockSpec(memory_space=pl.ANY),
        ],
        out_specs=pl.BlockSpec((tm, D_MODEL), lambda i: (i, 0)),
        out_shape=jax.ShapeDtypeStruct((T, D_MODEL), F32),
        scratch_shapes=[pltpu.VMEM((TOP_K, tm, HALF), jnp.uint32), pltpu.SemaphoreType.DMA(())],
        compiler_params=_cparams(("arbitrary",)),
        name="combine",
    )(dest, h, gates, nw, yb)


def _row(v, width=None):
    v = v.astype(F32).reshape(1, -1)
    if width is not None and v.shape[1] < width:
        v = jnp.pad(v, ((0, 0), (0, width - v.shape[1])))
    return v


def _tile(n, pref):
    t = pref
    while n % t:
        t //= 2
    return t


def kernel(x, norm_mix_w, w_in, w_pool, pool_scale, conv_w, conv_b, dt_bias, a_log, d_skip,
           ssd_norm_w, w_ssd_out, w_out, norm_ffn_w, router_w, router_b, w_up, b_up,
           w_down, b_down, norm_final_w):
    B, S, D = x.shape
    T = B * S
    x2 = x.reshape(T, D)

    o_p, o_z, o_xbc = 0, D_MODEL, D_MODEL + SSD_INNER
    o_dt = o_xbc + SSD_INNER + 2 * SSD_BC
    o_gp = o_dt + SSD_HEADS
    o_gs = o_gp + D_MODEL
    w_main = jnp.concatenate(
        [w_in[:, o_z:o_z + SSD_INNER], w_in[:, o_xbc:o_dt], w_in[:, o_p:o_p + D_MODEL],
         w_in[:, o_gp:o_gp + D_MODEL], w_in[:, o_gs:o_gs + D_MODEL]], axis=1).astype(BF16)
    w_dt = jnp.pad(w_in[:, o_dt:o_dt + SSD_HEADS], ((0, 0), (0, LANES - SSD_HEADS))).astype(BF16)

    proj, dt_raw = _inproj(x2, _row(norm_mix_w), w_main, w_dt, _tile(T, 1024), 2048)

    pool_g = _pool(proj, w_pool.astype(BF16), _row(pool_scale), _tile(S, 256), S)

    yn = _ssd(proj, dt_raw,
              conv_w[:, :SSD_INNER].astype(F32), _row(conv_b[:SSD_INNER]),
              conv_w[:, SSD_INNER:].astype(F32), _row(conv_b[SSD_INNER:]),
              _row(dt_bias, LANES), _row(a_log, LANES),
              _row(jnp.repeat(d_skip, SSD_HEADDIM)), _row(ssd_norm_w), B, S)

    rw = jnp.pad(router_w.astype(F32), ((0, 0), (0, LANES - N_EXPERTS)))
    rw_hi = rw.astype(BF16)
    rw_lo = (rw - rw_hi.astype(F32)).astype(BF16)
    h, u_pk, er, gates, cnt = _mixout(pool_g, yn, proj, x2, w_ssd_out.astype(BF16), w_out.astype(BF16),
                                      _row(norm_ffn_w), jnp.concatenate([rw_hi, rw_lo], axis=1),
                                      _row(router_b, LANES), _tile(T, 256))

    TK = T * TOP_K
    tm_ffn = 2304 if TK >= 32768 else 256
    sub = 256
    n_tiles = -(-TK // tm_ffn) + N_EXPERTS
    counts = cnt[0, :N_EXPERTS].astype(jnp.int32)
    tiles_e = (counts + tm_ffn - 1) // tm_ffn
    tend = jnp.cumsum(tiles_e)
    tstart = tend - tiles_e
    pstart = (tstart * tm_ffn).astype(jnp.int32)
    tile_id = jnp.arange(n_tiles, dtype=jnp.int32)
    tile_expert = jnp.minimum(jnp.searchsorted(tend, tile_id, side='right'), N_EXPERTS - 1).astype(jnp.int32)
    tile_rows = jnp.clip(counts[tile_expert] - (tile_id - tstart[tile_expert]) * tm_ffn, 0, tm_ffn).astype(jnp.int32)
    n_used = tend[-1].astype(jnp.int32).reshape(1)
    e_flat = er[:, :TOP_K].reshape(TK)
    r_flat = er[:, TOP_K:2 * TOP_K].reshape(TK)

    xb, dest = _dispatch(pstart, e_flat, r_flat, u_pk, n_tiles * tm_ffn, _tile(T, 512))
    yb = _ffn(tile_expert, n_used, tile_rows, xb, w_up, b_up.reshape(N_EXPERTS, 1, 2 * D_FF),
              w_down, b_down.reshape(N_EXPERTS, 1, D_MODEL), tm_ffn, sub, 512, 512)
    out = _combine(dest, h, gates, _row(norm_final_w), yb, _tile(T, 256))
    return out.reshape(B, S, D)
```

```python
import functools

import jax
import jax.numpy as jnp
from jax import lax
from jax.experimental import pallas as pl
from jax.experimental.pallas import tpu as pltpu

F32 = jnp.float32
BF16 = jnp.bfloat16

D_MODEL = 2048
EPS = 1e-5
POOL_GROUPS = 4
POOL_WINDOWS = (2, 4, 8, 16)
POOL_GW = D_MODEL // POOL_GROUPS
SSD_INNER = 2 * D_MODEL
SSD_HEADDIM = 64
SSD_HEADS = SSD_INNER // SSD_HEADDIM
SSD_GROUPS = 8
SSD_HPG = SSD_HEADS // SSD_GROUPS
SSD_STATE = 128
SSD_CONV = 4
SSD_CHUNK = 128
SSD_GW = SSD_HPG * SSD_HEADDIM
SSD_BC = SSD_GROUPS * SSD_STATE
N_EXPERTS = 32
TOP_K = 4
D_FF = D_MODEL
SWIGLU_ALPHA = 1.702
SWIGLU_LIMIT = 7.0

LANES = 128
HALF = D_MODEL // 2
CONV_SLAB = 256
QUAD = 4
QUAD_W = QUAD * SSD_HEADDIM
FFN_BIG_BLOCK = 1024
ROW_GROUP = 8
VMEM_LIMIT = 56 * 1024 * 1024

COL_Z = 0
COL_XS = SSD_INNER
COL_BC = 2 * SSD_INNER
COL_P = 2 * SSD_INNER + 2 * SSD_BC
COL_GP = COL_P + D_MODEL
COL_GS = COL_GP + D_MODEL
PROJ_COLS = COL_GS + D_MODEL

NEG_BIG = -1e30


def _cparams(sem):
    return pltpu.CompilerParams(dimension_semantics=sem, vmem_limit_bytes=VMEM_LIMIT)


def _inproj_kernel(x_ref, nw_ref, w_ref, wdt_ref, o_ref, dt_ref, u_scr):
    @pl.when(pl.program_id(1) == 0)
    def _():
        x = x_ref[...]
        ms = jnp.mean(x * x, axis=-1, keepdims=True)
        u = (x * lax.rsqrt(ms + EPS) * nw_ref[...]).astype(BF16)
        u_scr[...] = u
        dt_ref[...] = jnp.dot(u, wdt_ref[...], preferred_element_type=F32)

    o_ref[...] = jnp.dot(u_scr[...], w_ref[...], preferred_element_type=F32).astype(o_ref.dtype)


def _inproj(x2, norm_w, w_main, w_dt, tm, tn):
    T = x2.shape[0]
    return pl.pallas_call(
        _inproj_kernel,
        grid=(T // tm, PROJ_COLS // tn),
        in_specs=[
            pl.BlockSpec((tm, D_MODEL), lambda i, j: (i, 0)),
            pl.BlockSpec((1, D_MODEL), lambda i, j: (0, 0)),
            pl.BlockSpec((D_MODEL, tn), lambda i, j: (0, j)),
            pl.BlockSpec((D_MODEL, LANES), lambda i, j: (0, 0)),
        ],
        out_specs=[
            pl.BlockSpec((tm, tn), lambda i, j: (i, j)),
            pl.BlockSpec((tm, LANES), lambda i, j: (i, 0)),
        ],
        out_shape=[
            jax.ShapeDtypeStruct((T, PROJ_COLS), BF16),
            jax.ShapeDtypeStruct((T, LANES), F32),
        ],
        scratch_shapes=[pltpu.VMEM((tm, D_MODEL), BF16)],
        compiler_params=_cparams(("parallel", "arbitrary")),
        name="inproj",
    )(x2, norm_w, w_main, w_dt)


def _pool_kernel(prev_ref, p_ref, g_ref, band_ref, hband_ref, wp_ref, ps_ref, o_ref, *, tm, seq):
    row0 = (pl.program_id(0) * tm) % seq
    pos = row0 + lax.broadcasted_iota(jnp.int32, (tm, 1), 0)
    for g, w in enumerate(POOL_WINDOWS):
        cols = slice(g * POOL_GW, (g + 1) * POOL_GW)
        cur = p_ref[:, cols]
        prev = prev_ref[:, cols]
        prev = jnp.where(row0 == 0, jnp.zeros_like(prev), prev)
        acc = (jnp.dot(band_ref[g], cur, preferred_element_type=F32)
               + jnp.dot(hband_ref[g], prev, preferred_element_type=F32))
        cnt = jnp.minimum(pos + 1, w).astype(F32)
        pooled = acc / cnt - cur.astype(F32)
        y = jnp.dot(pooled.astype(BF16), wp_ref[g], preferred_element_type=F32)
        y = y * ps_ref[:, cols] * jax.nn.sigmoid(g_ref[:, cols].astype(F32))
        o_ref[:, cols] = y.astype(o_ref.dtype)


def _pool(proj, w_pool, pool_scale, tm, seq):
    T = proj.shape[0]
    hist = max(POOL_WINDOWS)
    t_out = jnp.arange(tm, dtype=jnp.int32)[None, :, None]
    src = jnp.arange(hist + tm, dtype=jnp.int32)[None, None, :]
    win = jnp.asarray(POOL_WINDOWS, jnp.int32)[:, None, None]
    band = jnp.logical_and(src > hist + t_out - win, src <= hist + t_out).astype(BF16)
    return pl.pallas_call(
        functools.partial(_pool_kernel, tm=tm, seq=seq),
        grid=(T // tm,),
        in_specs=[
            pl.BlockSpec((hist, D_MODEL), lambda i: (jnp.maximum(i * (tm // hist) - 1, 0), COL_P // D_MODEL)),
            pl.BlockSpec((tm, D_MODEL), lambda i: (i, COL_P // D_MODEL)),
            pl.BlockSpec((tm, D_MODEL), lambda i: (i, COL_GP // D_MODEL)),
            pl.BlockSpec((POOL_GROUPS, tm, tm), lambda i: (0, 0, 0)),
            pl.BlockSpec((POOL_GROUPS, tm, hist), lambda i: (0, 0, 0)),
            pl.BlockSpec((POOL_GROUPS, POOL_GW, POOL_GW), lambda i: (0, 0, 0)),
            pl.BlockSpec((1, D_MODEL), lambda i: (0, 0)),
        ],
        out_specs=pl.BlockSpec((tm, D_MODEL), lambda i: (i, 0)),
        out_shape=jax.ShapeDtypeStruct((T, D_MODEL), BF16),
        compiler_params=_cparams(("parallel",)),
        name="pool",
    )(proj, proj, proj, band[:, :, hist:], band[:, :, :hist], w_pool, pool_scale)


def _split3(v):
    hi = v.astype(BF16)
    r1 = v - hi.astype(F32)
    mid = r1.astype(BF16)
    lo = (r1 - mid.astype(F32)).astype(BF16)
    return hi, mid, lo


def _ssd_kernel(z_ref, xs_ref, bc_ref, dt_ref, cwx_ref, cbx_ref, cwbc_ref, cbbc_ref,
                dtb_ref, alog_ref, dsk_ref, nw_ref, qm_ref, sh_ref, o_ref,
                prev_x, prev_bc, xs_scr, b_scr, c_scr, a_scr, qt_scr, wt_scr, state):
    L = SSD_CHUNK
    c = pl.program_id(1)

    @pl.when(c == 0)
    def _():
        prev_x[...] = jnp.zeros_like(prev_x)
        prev_bc[...] = jnp.zeros_like(prev_bc)
        state[...] = jnp.zeros_like(state)

    def conv_slab(src_ref, prev_ref, cw_ref, cb_ref, col):
        cols = slice(col, col + CONV_SLAB)
        raw = src_ref[:, cols]
        both = jnp.concatenate([prev_ref[:, cols], raw], axis=0)
        prev_ref[:, cols] = raw
        acc = cb_ref[:, cols] + cw_ref[SSD_CONV - 1:SSD_CONV, cols] * raw.astype(F32)
        for k in range(SSD_CONV - 1):
            shifted = jnp.dot(sh_ref[k * L:(k + 1) * L, :], both, preferred_element_type=F32)
            acc = acc + cw_ref[k:k + 1, cols] * shifted
        return acc * jax.nn.sigmoid(acc)

    for s in range(SSD_INNER // CONV_SLAB):
        xs_scr[:, s * CONV_SLAB:(s + 1) * CONV_SLAB] = conv_slab(xs_ref, prev_x, cwx_ref, cbx_ref, s * CONV_SLAB)
    for s in range(SSD_BC // CONV_SLAB):
        b_scr[:, s * CONV_SLAB:(s + 1) * CONV_SLAB] = conv_slab(bc_ref, prev_bc, cwbc_ref, cbbc_ref, s * CONV_SLAB)
    for s in range(SSD_BC // CONV_SLAB):
        c_scr[:, s * CONV_SLAB:(s + 1) * CONV_SLAB] = conv_slab(
            bc_ref, prev_bc, cwbc_ref, cbbc_ref, SSD_BC + s * CONV_SLAB).astype(BF16)

    dtv = jax.nn.softplus(dt_ref[...] + dtb_ref[...])
    d_a = dtv * (-jnp.exp(alog_ref[...]))
    rr = lax.broadcasted_iota(jnp.int32, (L, L), 0)
    cc = lax.broadcasted_iota(jnp.int32, (L, L), 1)
    causal = rr >= cc
    low_half = cc < SSD_HEADDIM
    tri = jnp.where(causal, 1.0, 0.0).astype(BF16)
    a_cum = sum(jnp.dot(tri, part, preferred_element_type=F32) for part in _split3(d_a))
    a_t = a_cum.T
    dt_t = dtv.T
    a_scr[...] = a_cum
    qt_scr[...] = a_t - jnp.log(dt_t)
    wt_scr[...] = dt_t * jnp.exp(a_t[:, L - 1:L] - a_t)

    def group_body(g, carry):
        off = pl.multiple_of(g * SSD_GW, SSD_GW)
        offn = pl.multiple_of(g * SSD_STATE, SSD_STATE)
        xg = xs_scr[:, pl.ds(off, SSD_GW)]
        xg_b = xg.astype(BF16)
        b_f = b_scr[:, pl.ds(offn, SSD_STATE)]
        c_b = c_scr[:, pl.ds(offn, SSD_STATE)]
        b_t = b_f.T
        cbm = lax.dot_general(c_b, b_f.astype(BF16), (((1,), (1,)), ((), ())),
                              preferred_element_type=F32)
        s_g = state[g]
        y_off = jnp.dot(c_b, s_g.astype(BF16), preferred_element_type=F32)
        a_g = pltpu.roll(a_scr[...], (LANES - SSD_HPG * g) % LANES, 1)
        ys = []
        for quad in range(SSD_HPG // QUAD):
            qs = slice(quad * QUAD_W, (quad + 1) * QUAD_W)
            m_parts, b_parts, a_cols = [], [], []
            for jj in range(QUAD):
                j = quad * QUAD + jj
                h = g * SSD_HPG + j
                a_col = jnp.broadcast_to(a_g[:, j:j + 1], (L, L))
                seg = a_col - qt_scr[pl.ds(h, 1), :]
                m_parts.append((cbm * jnp.exp(jnp.where(causal, seg, NEG_BIG))).astype(BF16))
                b_parts.append((b_t * wt_scr[pl.ds(h, 1), :]).astype(BF16))
                a_cols.append(a_col)
            lhs = jnp.concatenate([jnp.concatenate(m_parts, axis=1), jnp.concatenate(b_parts, axis=1)], axis=0)
            xq = xg_b[:, qs]
            rhs = jnp.concatenate([xq * qm_ref[jj] for jj in range(QUAD)], axis=0)
            res = jnp.dot(lhs, rhs, preferred_element_type=F32)
            ea_q = jnp.concatenate(
                [jnp.exp(jnp.where(low_half, a_cols[2 * p], a_cols[2 * p + 1])) for p in range(QUAD // 2)], axis=1)
            ys.append(res[:L] + y_off[:, qs] * ea_q)
            state[g, :, qs] = s_g[:, qs] * ea_q[L - 1:L, :] + res[L:]
        y = jnp.concatenate(ys, axis=1) + dsk_ref[:, pl.ds(off, SSD_GW)] * xg
        zg = z_ref[:, pl.ds(off, SSD_GW)].astype(F32)
        yg = y * (zg * jax.nn.sigmoid(zg))
        ms = jnp.mean(yg * yg, axis=-1, keepdims=True)
        yn = yg * lax.rsqrt(ms + EPS) * nw_ref[:, pl.ds(off, SSD_GW)]
        o_ref[:, pl.ds(off, SSD_GW)] = yn.astype(o_ref.dtype)
        return carry

    def group_pair(gp, carry):
        group_body(2 * gp, carry)
        return group_body(2 * gp + 1, carry)

    lax.fori_loop(0, SSD_GROUPS // 2, group_pair, 0)


def _ssd(proj, dt_raw, cwx, cbx, cwbc, cbbc, dtb, alog, dsk, nw, batch, seq):
    L = SSD_CHUNK
    nc = seq // L
    T = proj.shape[0]
    row = lambda b, c: b * nc + c
    const = lambda b, c: (0, 0)
    head_of_lane = jnp.arange(QUAD_W, dtype=jnp.int32) // SSD_HEADDIM
    qmask = (head_of_lane[None, None, :] == jnp.arange(QUAD, dtype=jnp.int32)[:, None, None])
    qmask = jnp.broadcast_to(qmask, (QUAD, L, QUAD_W)).astype(BF16)
    tap = jnp.arange(SSD_CONV - 1, dtype=jnp.int32)[:, None, None]
    t_out = jnp.arange(L, dtype=jnp.int32)[None, :, None]
    src = jnp.arange(2 * L, dtype=jnp.int32)[None, None, :]
    shift = (src == L + t_out - (SSD_CONV - 1) + tap).astype(BF16).reshape((SSD_CONV - 1) * L, 2 * L)
    return pl.pallas_call(
        _ssd_kernel,
        grid=(batch, nc),
        in_specs=[
            pl.BlockSpec((L, SSD_INNER), lambda b, c: (row(b, c), COL_Z // SSD_INNER)),
            pl.BlockSpec((L, SSD_INNER), lambda b, c: (row(b, c), COL_XS // SSD_INNER)),
            pl.BlockSpec((L, 2 * SSD_BC), lambda b, c: (row(b, c), COL_BC // (2 * SSD_BC))),
            pl.BlockSpec((L, LANES), lambda b, c: (row(b, c), 0)),
            pl.BlockSpec((SSD_CONV, SSD_INNER), const),
            pl.BlockSpec((1, SSD_INNER), const),
            pl.BlockSpec((SSD_CONV, 2 * SSD_BC), const),
            pl.BlockSpec((1, 2 * SSD_BC), const),
            pl.BlockSpec((1, LANES), const),
            pl.BlockSpec((1, LANES), const),
            pl.BlockSpec((1, SSD_INNER), const),
            pl.BlockSpec((1, SSD_INNER), const),
            pl.BlockSpec((QUAD, L, QUAD_W), lambda b, c: (0, 0, 0)),
            pl.BlockSpec(((SSD_CONV - 1) * L, 2 * L), const),
        ],
        out_specs=pl.BlockSpec((L, SSD_INNER), lambda b, c: (row(b, c), 0)),
        out_shape=jax.ShapeDtypeStruct((T, SSD_INNER), BF16),
        scratch_shapes=[
            pltpu.VMEM((L, SSD_INNER), BF16),
            pltpu.VMEM((L, 2 * SSD_BC), BF16),
            pltpu.VMEM((L, SSD_INNER), F32),
            pltpu.VMEM((L, SSD_BC), F32),
            pltpu.VMEM((L, SSD_BC), BF16),
            pltpu.VMEM((L, LANES), F32),
            pltpu.VMEM((LANES, L), F32),
            pltpu.VMEM((LANES, L), F32),
            pltpu.VMEM((SSD_GROUPS, SSD_STATE, SSD_GW), F32),
        ],
        compiler_params=_cparams(("arbitrary", "arbitrary")),
        name="ssd",
    )(proj, proj, proj, dt_raw, cwx, cbx, cwbc, cbbc, dtb, alog, dsk, nw, qmask, shift)


def _pack_bf16_pairs(v):
    bits = pltpu.bitcast(v.astype(BF16).astype(F32), jnp.uint32)
    return (bits[:, :HALF] >> 16) | (bits[:, HALF:] & jnp.uint32(0xFFFF0000))


def _unpack_lo(w):
    return pltpu.bitcast(w << 16, F32)


def _unpack_hi(w):
    return pltpu.bitcast(w & jnp.uint32(0xFFFF0000), F32)


def _mixout_kernel(pg_ref, yn_ref, gs_ref, x_ref, wso_ref, wo_ref, nw_ref, rw_ref, rb_ref,
                   h_ref, u_ref, er_ref, gt_ref, cnt_ref, run_scr, lg_scr, *, tm):
    i = pl.program_id(0)
    last = pl.num_programs(0) - 1

    @pl.when(i == 0)
    def _():
        run_scr[...] = jnp.zeros_like(run_scr)
        lg_scr[...] = jnp.zeros_like(lg_scr)

    def route_previous_tile(lg_prev):
        er, gt = _route(lg_prev, run_scr, tm, i > 0)
        er_ref[...] = er
        gt_ref[...] = gt
        cnt_ref[...] = run_scr[...]

    @pl.when(i < last)
    def _():
        lg_prev = lg_scr[...]
        ssd = jnp.dot(yn_ref[...], wso_ref[...], preferred_element_type=F32)
        merged = pg_ref[...].astype(F32) + jax.nn.sigmoid(gs_ref[...].astype(F32)) * ssd
        h = x_ref[...] + jnp.dot(merged.astype(BF16), wo_ref[...], preferred_element_type=F32)
        h_ref[...] = h
        ms = jnp.mean(h * h, axis=-1, keepdims=True)
        u = h * lax.rsqrt(ms + EPS) * nw_ref[...]
        u_hi = u.astype(BF16)
        u_ref[...] = _pack_bf16_pairs(u)
        u_lo = (u - u_hi.astype(F32)).astype(BF16)
        both = jnp.dot(u_hi, rw_ref[...], preferred_element_type=F32)
        lg = both[:, :LANES] + both[:, LANES:] + jnp.dot(u_lo, rw_ref[:, :LANES], preferred_element_type=F32)
        lg_scr[...] = lg + rb_ref[...]
        route_previous_tile(lg_prev)

    @pl.when(i == last)
    def _():
        route_previous_tile(lg_scr[...])


def _mixout(pool_g, yn, proj, x2, wso, wo, nw, rw, rb, tm):
    T = x2.shape[0]
    n = T // tm
    const = lambda i: (0, 0)
    cur = lambda i: jnp.minimum(i, n - 1)
    prev = lambda i: jnp.maximum(i - 1, 0)
    return pl.pallas_call(
        functools.partial(_mixout_kernel, tm=tm),
        grid=(n + 1,),
        in_specs=[
            pl.BlockSpec((tm, D_MODEL), lambda i: (cur(i), 0)),
            pl.BlockSpec((tm, SSD_INNER), lambda i: (cur(i), 0)),
            pl.BlockSpec((tm, D_MODEL), lambda i: (cur(i), COL_GS // D_MODEL)),
            pl.BlockSpec((tm, D_MODEL), lambda i: (cur(i), 0)),
            pl.BlockSpec((SSD_INNER, D_MODEL), const),
            pl.BlockSpec((D_MODEL, D_MODEL), const),
            pl.BlockSpec((1, D_MODEL), const),
            pl.BlockSpec((D_MODEL, 2 * LANES), const),
            pl.BlockSpec((1, LANES), const),
        ],
        out_specs=[
            pl.BlockSpec((tm, D_MODEL), lambda i: (cur(i), 0)),
            pl.BlockSpec((tm, HALF), lambda i: (cur(i), 0)),
            pl.BlockSpec((tm, LANES), lambda i: (prev(i), 0)),
            pl.BlockSpec((tm, LANES), lambda i: (prev(i), 0)),
            pl.BlockSpec((1, LANES), const),
        ],
        out_shape=[
            jax.ShapeDtypeStruct((T, D_MODEL), F32),
            jax.ShapeDtypeStruct((T, HALF), jnp.uint32),
            jax.ShapeDtypeStruct((T, LANES), jnp.int32),
            jax.ShapeDtypeStruct((T, LANES), F32),
            jax.ShapeDtypeStruct((1, LANES), F32),
        ],
        scratch_shapes=[pltpu.VMEM((1, LANES), F32), pltpu.VMEM((tm, LANES), F32)],
        compiler_params=_cparams(("arbitrary",)),
        name="mixout",
    )(pool_g, yn, proj, x2, wso, wo, nw, rw, rb)


def _route(lg, run_scr, tm, valid):
    lane = lax.broadcasted_iota(jnp.int32, (tm, LANES), 1)
    lg = jnp.where(lane < N_EXPERTS, lg, -jnp.inf)
    idxs, vals, hots = [], [], []
    for _ in range(TOP_K):
        m = jnp.max(lg, axis=-1, keepdims=True)
        idx = jnp.min(jnp.where(lg == m, lane, LANES), axis=-1, keepdims=True)
        hot = lane == idx
        lg = jnp.where(hot, -jnp.inf, lg)
        idxs.append(idx)
        vals.append(m)
        hots.append(hot)
    exps = [jnp.exp(v - vals[0]) for v in vals]
    den = exps[0] + exps[1] + exps[2] + exps[3]
    cnt = sum(jnp.where(hot, 1.0, 0.0) for hot in hots)
    rr = lax.broadcasted_iota(jnp.int32, (tm, tm), 0)
    cc = lax.broadcasted_iota(jnp.int32, (tm, tm), 1)
    before = jnp.where(rr > cc, 1.0, 0.0).astype(BF16)
    base = jnp.dot(before, cnt.astype(BF16), preferred_element_type=F32) + run_scr[...]
    er = jnp.zeros((tm, LANES), jnp.int32)
    gt = jnp.zeros((tm, LANES), F32)
    for k in range(TOP_K):
        rank = jnp.sum(jnp.where(hots[k], base, 0.0), axis=-1, keepdims=True)
        er = jnp.where(lane == k, idxs[k], er)
        er = jnp.where(lane == TOP_K + k, rank.astype(jnp.int32), er)
        gt = jnp.where(lane == k, exps[k] / den, gt)
    run_scr[...] += jnp.where(valid, jnp.sum(cnt, axis=0, keepdims=True), 0.0)
    return er, gt


def _dispatch_kernel(ps_ref, e_ref, r_ref, u_ref, xb_ref, dest_ref, sem, *, tm):
    def body(g, carry):
        r0 = pl.multiple_of(g * ROW_GROUP, ROW_GROUP)
        for i in range(ROW_GROUP):
            for k in range(TOP_K):
                j = (r0 + i) * TOP_K + k
                d = ps_ref[e_ref[j]] + r_ref[j]
                dest_ref[j] = d
                pltpu.make_async_copy(u_ref.at[pl.ds(r0 + i, 1)], xb_ref.at[pl.ds(d, 1)], sem).start()
        return carry

    lax.fori_loop(0, tm // ROW_GROUP, body, 0)
    for _ in range(TOP_K):
        pltpu.make_async_copy(u_ref, xb_ref.at[pl.ds(0, tm)], sem).wait()


def _dispatch(pstart, e_flat, r_flat, u_pk, n_slots, tm):
    T = u_pk.shape[0]
    grid_spec = pltpu.PrefetchScalarGridSpec(
        num_scalar_prefetch=1,
        grid=(T // tm,),
        in_specs=[
            pl.BlockSpec((tm * TOP_K,), lambda i, ps: (i,), memory_space=pltpu.SMEM),
            pl.BlockSpec((tm * TOP_K,), lambda i, ps: (i,), memory_space=pltpu.SMEM),
            pl.BlockSpec((tm, HALF), lambda i, ps: (i, 0)),
        ],
        out_specs=[
            pl.BlockSpec(memory_space=pl.ANY),
            pl.BlockSpec((tm * TOP_K,), lambda i, ps: (i,), memory_space=pltpu.SMEM),
        ],
        scratch_shapes=[pltpu.SemaphoreType.DMA(())],
    )
    return pl.pallas_call(
        functools.partial(_dispatch_kernel, tm=tm),
        grid_spec=grid_spec,
        out_shape=[
            jax.ShapeDtypeStruct((n_slots, HALF), jnp.uint32),
            jax.ShapeDtypeStruct((T * TOP_K,), jnp.int32),
        ],
        compiler_params=_cparams(("arbitrary",)),
        name="dispatch",
    )(pstart, e_flat, r_flat, u_pk)


def _row_blocks(nrows, tm, sub, weights, block):
    nblk = (nrows + sub - 1) // sub
    per_big = min(FFN_BIG_BLOCK, tm) // sub
    nbig = tm // (per_big * sub)
    for b in range(nbig):
        @pl.when(nblk >= (b + 1) * per_big)
        def _():
            block(b * per_big * sub, per_big * sub, weights())

    def rest(sb, carry):
        block(pl.multiple_of(sb * sub, sub), sub, weights())
        return carry

    lax.fori_loop(jnp.minimum(nblk // per_big, nbig) * per_big, nblk, rest, 0)


def _ffn_up_kernel(te_ref, nu_ref, tr_ref, x_ref, wg_ref, wl_ref, bg_ref, bl_ref, o_ref, *, tm, sub):
    nrows = tr_ref[pl.program_id(0)]

    def weights():
        return wg_ref[0].astype(BF16), wl_ref[0].astype(BF16)

    def block(r0, nr, w):
        packed = x_ref[pl.ds(r0, nr), :]
        x = jnp.concatenate([_unpack_lo(packed).astype(BF16), _unpack_hi(packed).astype(BF16)], axis=1)
        glu = jnp.minimum(jnp.dot(x, w[0], preferred_element_type=F32) + bg_ref[0], SWIGLU_LIMIT)
        lin = jnp.clip(jnp.dot(x, w[1], preferred_element_type=F32) + bl_ref[0], -SWIGLU_LIMIT, SWIGLU_LIMIT)
        act = glu * jax.nn.sigmoid(SWIGLU_ALPHA * glu) * (lin + 1.0)
        o_ref[pl.ds(r0, nr), :] = act.astype(o_ref.dtype)

    _row_blocks(nrows, tm, sub, weights, block)


def _ffn_down_kernel(te_ref, nu_ref, tr_ref, a_ref, wa_ref, wb_ref, ba_ref, bb_ref, o_ref, *, tm, sub):
    nrows = tr_ref[pl.program_id(0)]

    def weights():
        return wa_ref[0].astype(BF16), wb_ref[0].astype(BF16)

    def block(r0, nr, w):
        a = a_ref[pl.ds(r0, nr), :]
        ya = jnp.dot(a, w[0], preferred_element_type=F32) + ba_ref[0]
        yb = jnp.dot(a, w[1], preferred_element_type=F32) + bb_ref[0]
        lo = pltpu.bitcast(ya.astype(BF16).astype(F32), jnp.uint32) >> 16
        hi = pltpu.bitcast(yb.astype(BF16).astype(F32), jnp.uint32) & jnp.uint32(0xFFFF0000)
        o_ref[pl.ds(r0, nr), :] = lo | hi

    _row_blocks(nrows, tm, sub, weights, block)


def _ffn(tile_expert, n_used, tile_rows, xb, w_up, b_up3, w_down, b_down3, tm, sub, fc, nc):
    n_slots = xb.shape[0]
    n_tiles = n_slots // tm
    nf = D_FF // fc
    nj = HALF // nc

    def live(i, nu):
        return jnp.minimum(i, nu[0] - 1)

    def frozen(i, j, nu, last):
        return jnp.where(i < nu[0], j, last)

    up_spec = pltpu.PrefetchScalarGridSpec(
        num_scalar_prefetch=3,
        grid=(n_tiles, nf),
        in_specs=[
            pl.BlockSpec((tm, HALF), lambda i, f, te, nu, tr: (live(i, nu), 0)),
            pl.BlockSpec((1, D_MODEL, fc), lambda i, f, te, nu, tr: (te[i], 0, frozen(i, f, nu, nf - 1))),
            pl.BlockSpec((1, D_MODEL, fc), lambda i, f, te, nu, tr: (te[i], 0, nf + frozen(i, f, nu, nf - 1))),
            pl.BlockSpec((1, 1, fc), lambda i, f, te, nu, tr: (te[i], 0, frozen(i, f, nu, nf - 1))),
            pl.BlockSpec((1, 1, fc), lambda i, f, te, nu, tr: (te[i], 0, nf + frozen(i, f, nu, nf - 1))),
        ],
        out_specs=pl.BlockSpec((tm, fc), lambda i, f, te, nu, tr: (live(i, nu), frozen(i, f, nu, nf - 1))),
    )
    act = pl.pallas_call(
        functools.partial(_ffn_up_kernel, tm=tm, sub=sub),
        grid_spec=up_spec,
        out_shape=jax.ShapeDtypeStruct((n_slots, D_FF), BF16),
        compiler_params=_cparams(("arbitrary", "arbitrary")),
        name="ffn_up",
    )(tile_expert, n_used, tile_rows, xb, w_up, w_up, b_up3, b_up3)

    down_spec = pltpu.PrefetchScalarGridSpec(
        num_scalar_prefetch=3,
        grid=(n_tiles, nj),
        in_specs=[
            pl.BlockSpec((tm, D_FF), lambda i, j, te, nu, tr: (live(i, nu), 0)),
            pl.BlockSpec((1, D_FF, nc), lambda i, j, te, nu, tr: (te[i], 0, frozen(i, j, nu, nj - 1))),
            pl.BlockSpec((1, D_FF, nc), lambda i, j, te, nu, tr: (te[i], 0, nj + frozen(i, j, nu, nj - 1))),
            pl.BlockSpec((1, 1, nc), lambda i, j, te, nu, tr: (te[i], 0, frozen(i, j, nu, nj - 1))),
            pl.BlockSpec((1, 1, nc), lambda i, j, te, nu, tr: (te[i], 0, nj + frozen(i, j, nu, nj - 1))),
        ],
        out_specs=pl.BlockSpec((tm, nc), lambda i, j, te, nu, tr: (live(i, nu), frozen(i, j, nu, nj - 1))),
    )
    return pl.pallas_call(
        functools.partial(_ffn_down_kernel, tm=tm, sub=sub),
        grid_spec=down_spec,
        out_shape=jax.ShapeDtypeStruct((n_slots, HALF), jnp.uint32),
        compiler_params=_cparams(("arbitrary", "arbitrary")),
        name="ffn_down",
    )(tile_expert, n_used, tile_rows, act, w_down, w_down, b_down3, b_down3)


def _combine_kernel(d_ref, h_ref, g_ref, nw_ref, yb_ref, o_ref, ybuf, sem, *, tm):
    def body(g, carry):
        r0 = pl.multiple_of(g * ROW_GROUP, ROW_GROUP)
        for i in range(ROW_GROUP):
            for k in range(TOP_K):
                d = d_ref[(r0 + i) * TOP_K + k]
                pltpu.make_async_copy(yb_ref.at[pl.ds(d, 1)], ybuf.at[k, pl.ds(r0 + i, 1)], sem).start()
        return carry

    lax.fori_loop(0, tm // ROW_GROUP, body, 0)
    for k in range(TOP_K):
        pltpu.make_async_copy(yb_ref.at[pl.ds(0, tm)], ybuf.at[k], sem).wait()
    lo = h_ref[:, :HALF]
    hi = h_ref[:, HALF:]
    for k in range(TOP_K):
        w = ybuf[k]
        g = g_ref[:, k:k + 1]
        lo = lo + g * _unpack_lo(w)
        hi = hi + g * _unpack_hi(w)
    ms = (jnp.sum(lo * lo, axis=-1, keepdims=True) + jnp.sum(hi * hi, axis=-1, keepdims=True)) / D_MODEL
    scale = lax.rsqrt(ms + EPS)
    o_ref[:, :HALF] = lo * scale * nw_ref[:, :HALF]
    o_ref[:, HALF:] = hi * scale * nw_ref[:, HALF:]


def _combine(dest, h, gates, nw, yb, tm):
    T = h.shape[0]
    return pl.pallas_call(
        functools.partial(_combine_kernel, tm=tm),
        grid=(T // tm,),
        in_specs=[
            pl.BlockSpec((tm * TOP_K,), lambda i: (i,), memory_space=pltpu.SMEM),
            pl.BlockSpec((tm, D_MODEL), lambda i: (i, 0)),
            pl.BlockSpec((tm, LANES), lambda i: (i, 0)),
            pl.BlockSpec((1, D_MODEL), lambda i: (0, 0)),
            pl.BlockSpec(memory_space=pl.ANY),
        ],
        out_specs=pl.BlockSpec((tm, D_MODEL), lambda i: (i, 0)),
        out_shape=jax.ShapeDtypeStruct((T, D_MODEL), F32),
        scratch_shapes=[pltpu.VMEM((TOP_K, tm, HALF), jnp.uint32), pltpu.SemaphoreType.DMA(())],
        compiler_params=_cparams(("arbitrary",)),
        name="combine",
    )(dest, h, gates, nw, yb)


def _row(v, width=None):
    v = v.astype(F32).reshape(1, -1)
    if width is not None and v.shape[1] < width:
        v = jnp.pad(v, ((0, 0), (0, width - v.shape[1])))
    return v


def _tile(n, pref):
    t = pref
    while n % t:
        t //= 2
    return t


def kernel(x, norm_mix_w, w_in, w_pool, pool_scale, conv_w, conv_b, dt_bias, a_log, d_skip,
           ssd_norm_w, w_ssd_out, w_out, norm_ffn_w, router_w, router_b, w_up, b_up,
           w_down, b_down, norm_final_w):
    B, S, D = x.shape
    T = B * S
    x2 = x.reshape(T, D)

    o_p, o_z, o_xbc = 0, D_MODEL, D_MODEL + SSD_INNER
    o_dt = o_xbc + SSD_INNER + 2 * SSD_BC
    o_gp = o_dt + SSD_HEADS
    o_gs = o_gp + D_MODEL
    w_main = jnp.concatenate(
        [w_in[:, o_z:o_z + SSD_INNER], w_in[:, o_xbc:o_dt], w_in[:, o_p:o_p + D_MODEL],
         w_in[:, o_gp:o_gp + D_MODEL], w_in[:, o_gs:o_gs + D_MODEL]], axis=1).astype(BF16)
    w_dt = jnp.pad(w_in[:, o_dt:o_dt + SSD_HEADS], ((0, 0), (0, LANES - SSD_HEADS))).astype(BF16)

    proj, dt_raw = _inproj(x2, _row(norm_mix_w), w_main, w_dt, _tile(T, 1024), 2048)

    pool_g = _pool(proj, w_pool.astype(BF16), _row(pool_scale), _tile(S, 256), S)

    yn = _ssd(proj, dt_raw,
              conv_w[:, :SSD_INNER].astype(F32), _row(conv_b[:SSD_INNER]),
              conv_w[:, SSD_INNER:].astype(F32), _row(conv_b[SSD_INNER:]),
              _row(dt_bias, LANES), _row(a_log, LANES),
              _row(jnp.repeat(d_skip, SSD_HEADDIM)), _row(ssd_norm_w), B, S)

    rw = jnp.pad(router_w.astype(F32), ((0, 0), (0, LANES - N_EXPERTS)))
    rw_hi = rw.astype(BF16)
    rw_lo = (rw - rw_hi.astype(F32)).astype(BF16)
    h, u_pk, er, gates, cnt = _mixout(pool_g, yn, proj, x2, w_ssd_out.astype(BF16), w_out.astype(BF16),
                                      _row(norm_ffn_w), jnp.concatenate([rw_hi, rw_lo], axis=1),
                                      _row(router_b, LANES), _tile(T, 256))

    TK = T * TOP_K
    tm_ffn = 2304 if TK >= 32768 else 256
    sub = 256
    n_tiles = -(-TK // tm_ffn) + N_EXPERTS
    counts = cnt[0, :N_EXPERTS].astype(jnp.int32)
    tiles_e = (counts + tm_ffn - 1) // tm_ffn
    tend = jnp.cumsum(tiles_e)
    tstart = tend - tiles_e
    pstart = (tstart * tm_ffn).astype(jnp.int32)
    tile_id = jnp.arange(n_tiles, dtype=jnp.int32)
    tile_expert = jnp.minimum(jnp.searchsorted(tend, tile_id, side='right'), N_EXPERTS - 1).astype(jnp.int32)
    tile_rows = jnp.clip(counts[tile_expert] - (tile_id - tstart[tile_expert]) * tm_ffn, 0, tm_ffn).astype(jnp.int32)
    n_used = tend[-1].astype(jnp.int32).reshape(1)
    e_flat = er[:, :TOP_K].reshape(TK)
    r_flat = er[:, TOP_K:2 * TOP_K].reshape(TK)

    xb, dest = _dispatch(pstart, e_flat, r_flat, u_pk, n_tiles * tm_ffn, _tile(T, 512))
    yb = _ffn(tile_expert, n_used, tile_rows, xb, w_up, b_up.reshape(N_EXPERTS, 1, 2 * D_FF),
              w_down, b_down.reshape(N_EXPERTS, 1, D_MODEL), tm_ffn, sub, 512, 512)
    out = _combine(dest, h, gates, _row(norm_final_w), yb, _tile(T, 256))
    return out.reshape(B, S, D)
```

```python
import functools

import jax
import jax.numpy as jnp
from jax import lax
from jax.experimental import pallas as pl
from jax.experimental.pallas import tpu as pltpu

F32 = jnp.float32
BF16 = jnp.bfloat16

D_MODEL = 2048
EPS = 1e-5
POOL_GROUPS = 4
POOL_WINDOWS = (2, 4, 8, 16)
POOL_GW = D_MODEL // POOL_GROUPS
SSD_INNER = 2 * D_MODEL
SSD_HEADDIM = 64
SSD_HEADS = SSD_INNER // SSD_HEADDIM
SSD_GROUPS = 8
SSD_HPG = SSD_HEADS // SSD_GROUPS
SSD_STATE = 128
SSD_CONV = 4
SSD_CHUNK = 128
SSD_GW = SSD_HPG * SSD_HEADDIM
SSD_BC = SSD_GROUPS * SSD_STATE
N_EXPERTS = 32
TOP_K = 4
D_FF = D_MODEL
SWIGLU_ALPHA = 1.702
SWIGLU_LIMIT = 7.0

LANES = 128
HALF = D_MODEL // 2
CONV_SLAB = 256
QUAD = 4
QUAD_W = QUAD * SSD_HEADDIM
FFN_BIG_BLOCK = 1024
ROW_GROUP = 8
VMEM_LIMIT = 56 * 1024 * 1024

COL_Z = 0
COL_XS = SSD_INNER
COL_BC = 2 * SSD_INNER
COL_P = 2 * SSD_INNER + 2 * SSD_BC
COL_GP = COL_P + D_MODEL
COL_GS = COL_GP + D_MODEL
PROJ_COLS = COL_GS + D_MODEL

NEG_BIG = -1e30


def _cparams(sem):
    return pltpu.CompilerParams(dimension_semantics=sem, vmem_limit_bytes=VMEM_LIMIT)


def _inproj_kernel(x_ref, nw_ref, w_ref, wdt_ref, o_ref, dt_ref, u_scr):
    @pl.when(pl.program_id(1) == 0)
    def _():
        x = x_ref[...]
        ms = jnp.mean(x * x, axis=-1, keepdims=True)
        u = (x * lax.rsqrt(ms + EPS) * nw_ref[...]).astype(BF16)
        u_scr[...] = u
        dt_ref[...] = jnp.dot(u, wdt_ref[...], preferred_element_type=F32)

    o_ref[...] = jnp.dot(u_scr[...], w_ref[...], preferred_element_type=F32).astype(o_ref.dtype)


def _inproj(x2, norm_w, w_main, w_dt, tm, tn):
    T = x2.shape[0]
    return pl.pallas_call(
        _inproj_kernel,
        grid=(T // tm, PROJ_COLS // tn),
        in_specs=[
            pl.BlockSpec((tm, D_MODEL), lambda i, j: (i, 0)),
            pl.BlockSpec((1, D_MODEL), lambda i, j: (0, 0)),
            pl.BlockSpec((D_MODEL, tn), lambda i, j: (0, j)),
            pl.BlockSpec((D_MODEL, LANES), lambda i, j: (0, 0)),
        ],
        out_specs=[
            pl.BlockSpec((tm, tn), lambda i, j: (i, j)),
            pl.BlockSpec((tm, LANES), lambda i, j: (i, 0)),
        ],
        out_shape=[
            jax.ShapeDtypeStruct((T, PROJ_COLS), BF16),
            jax.ShapeDtypeStruct((T, LANES), F32),
        ],
        scratch_shapes=[pltpu.VMEM((tm, D_MODEL), BF16)],
        compiler_params=_cparams(("parallel", "arbitrary")),
        name="inproj",
    )(x2, norm_w, w_main, w_dt)


def _pool_kernel(prev_ref, p_ref, g_ref, band_ref, hband_ref, wp_ref, ps_ref, o_ref, *, tm, seq):
    row0 = (pl.program_id(0) * tm) % seq
    pos = row0 + lax.broadcasted_iota(jnp.int32, (tm, 1), 0)
    for g, w in enumerate(POOL_WINDOWS):
        cols = slice(g * POOL_GW, (g + 1) * POOL_GW)
        cur = p_ref[:, cols]
        prev = prev_ref[:, cols]
        prev = jnp.where(row0 == 0, jnp.zeros_like(prev), prev)
        acc = (jnp.dot(band_ref[g], cur, preferred_element_type=F32)
               + jnp.dot(hband_ref[g], prev, preferred_element_type=F32))
        cnt = jnp.minimum(pos + 1, w).astype(F32)
        pooled = acc / cnt - cur.astype(F32)
        y = jnp.dot(pooled.astype(BF16), wp_ref[g], preferred_element_type=F32)
        y = y * ps_ref[:, cols] * jax.nn.sigmoid(g_ref[:, cols].astype(F32))
        o_ref[:, cols] = y.astype(o_ref.dtype)


def _pool(proj, w_pool, pool_scale, tm, seq):
    T = proj.shape[0]
    hist = max(POOL_WINDOWS)
    t_out = jnp.arange(tm, dtype=jnp.int32)[None, :, None]
    src = jnp.arange(hist + tm, dtype=jnp.int32)[None, None, :]
    win = jnp.asarray(POOL_WINDOWS, jnp.int32)[:, None, None]
    band = jnp.logical_and(src > hist + t_out - win, src <= hist + t_out).astype(BF16)
    return pl.pallas_call(
        functools.partial(_pool_kernel, tm=tm, seq=seq),
        grid=(T // tm,),
        in_specs=[
            pl.BlockSpec((hist, D_MODEL), lambda i: (jnp.maximum(i * (tm // hist) - 1, 0), COL_P // D_MODEL)),
            pl.BlockSpec((tm, D_MODEL), lambda i: (i, COL_P // D_MODEL)),
            pl.BlockSpec((tm, D_MODEL), lambda i: (i, COL_GP // D_MODEL)),
            pl.BlockSpec((POOL_GROUPS, tm, tm), lambda i: (0, 0, 0)),
            pl.BlockSpec((POOL_GROUPS, tm, hist), lambda i: (0, 0, 0)),
            pl.BlockSpec((POOL_GROUPS, POOL_GW, POOL_GW), lambda i: (0, 0, 0)),
            pl.BlockSpec((1, D_MODEL), lambda i: (0, 0)),
        ],
        out_specs=pl.BlockSpec((tm, D_MODEL), lambda i: (i, 0)),
        out_shape=jax.ShapeDtypeStruct((T, D_MODEL), BF16),
        compiler_params=_cparams(("parallel",)),
        name="pool",
    )(proj, proj, proj, band[:, :, hist:], band[:, :, :hist], w_pool, pool_scale)


def _split3(v):
    hi = v.astype(BF16)
    r1 = v - hi.astype(F32)
    mid = r1.astype(BF16)
    lo = (r1 - mid.astype(F32)).astype(BF16)
    return hi, mid, lo


def _ssd_kernel(z_ref, xs_ref, bc_ref, dt_ref, cwx_ref, cbx_ref, cwbc_ref, cbbc_ref,
                dtb_ref, alog_ref, dsk_ref, nw_ref, qm_ref, sh_ref, o_ref,
                prev_x, prev_bc, xs_scr, b_scr, c_scr, a_scr, qt_scr, wt_scr, state):
    L = SSD_CHUNK
    c = pl.program_id(1)

    @pl.when(c == 0)
    def _():
        prev_x[...] = jnp.zeros_like(prev_x)
        prev_bc[...] = jnp.zeros_like(prev_bc)
        state[...] = jnp.zeros_like(state)

    def conv_slab(src_ref, prev_ref, cw_ref, cb_ref, col):
        cols = slice(col, col + CONV_SLAB)
        raw = src_ref[:, cols]
        both = jnp.concatenate([prev_ref[:, cols], raw], axis=0)
        prev_ref[:, cols] = raw
        acc = cb_ref[:, cols] + cw_ref[SSD_CONV - 1:SSD_CONV, cols] * raw.astype(F32)
        for k in range(SSD_CONV - 1):
            shifted = jnp.dot(sh_ref[k * L:(k + 1) * L, :], both, preferred_element_type=F32)
            acc = acc + cw_ref[k:k + 1, cols] * shifted
        return acc * jax.nn.sigmoid(acc)

    for s in range(SSD_INNER // CONV_SLAB):
        xs_scr[:, s * CONV_SLAB:(s + 1) * CONV_SLAB] = conv_slab(xs_ref, prev_x, cwx_ref, cbx_ref, s * CONV_SLAB)
    for s in range(SSD_BC // CONV_SLAB):
        b_scr[:, s * CONV_SLAB:(s + 1) * CONV_SLAB] = conv_slab(bc_ref, prev_bc, cwbc_ref, cbbc_ref, s * CONV_SLAB)
    for s in range(SSD_BC // CONV_SLAB):
        c_scr[:, s * CONV_SLAB:(s + 1) * CONV_SLAB] = conv_slab(
            bc_ref, prev_bc, cwbc_ref, cbbc_ref, SSD_BC + s * CONV_SLAB).astype(BF16)

    dtv = jax.nn.softplus(dt_ref[...] + dtb_ref[...])
    d_a = dtv * (-jnp.exp(alog_ref[...]))
    rr = lax.broadcasted_iota(jnp.int32, (L, L), 0)
    cc = lax.broadcasted_iota(jnp.int32, (L, L), 1)
    causal = rr >= cc
    low_half = cc < SSD_HEADDIM
    tri = jnp.where(causal, 1.0, 0.0).astype(BF16)
    a_cum = sum(jnp.dot(tri, part, preferred_element_type=F32) for part in _split3(d_a))
    a_t = a_cum.T
    dt_t = dtv.T
    a_scr[...] = a_cum
    qt_scr[...] = a_t - jnp.log(dt_t)
    wt_scr[...] = dt_t * jnp.exp(a_t[:, L - 1:L] - a_t)

    def group_body(g, carry):
        off = pl.multiple_of(g * SSD_GW, SSD_GW)
        offn = pl.multiple_of(g * SSD_STATE, SSD_STATE)
        xg = xs_scr[:, pl.ds(off, SSD_GW)]
        xg_b = xg.astype(BF16)
        b_f = b_scr[:, pl.ds(offn, SSD_STATE)]
        c_b = c_scr[:, pl.ds(offn, SSD_STATE)]
        b_t = b_f.T
        cbm = lax.dot_general(c_b, b_f.astype(BF16), (((1,), (1,)), ((), ())),
                              preferred_element_type=F32)
        s_g = state[g]
        y_off = jnp.dot(c_b, s_g.astype(BF16), preferred_element_type=F32)
        a_g = pltpu.roll(a_scr[...], (LANES - SSD_HPG * g) % LANES, 1)
        ys = []
        for quad in range(SSD_HPG // QUAD):
            qs = slice(quad * QUAD_W, (quad + 1) * QUAD_W)
            m_parts, b_parts, a_cols = [], [], []
            for jj in range(QUAD):
                j = quad * QUAD + jj
                h = g * SSD_HPG + j
                a_col = jnp.broadcast_to(a_g[:, j:j + 1], (L, L))
                seg = a_col - qt_scr[pl.ds(h, 1), :]
                m_parts.append((cbm * jnp.exp(jnp.where(causal, seg, NEG_BIG))).astype(BF16))
                b_parts.append((b_t * wt_scr[pl.ds(h, 1), :]).astype(BF16))
                a_cols.append(a_col)
            lhs = jnp.concatenate([jnp.concatenate(m_parts, axis=1), jnp.concatenate(b_parts, axis=1)], axis=0)
            xq = xg_b[:, qs]
            rhs = jnp.concatenate([xq * qm_ref[jj] for jj in range(QUAD)], axis=0)
            res = jnp.dot(lhs, rhs, preferred_element_type=F32)
            ea_q = jnp.concatenate(
                [jnp.exp(jnp.where(low_half, a_cols[2 * p], a_cols[2 * p + 1])) for p in range(QUAD // 2)], axis=1)
            ys.append(res[:L] + y_off[:, qs] * ea_q)
            state[g, :, qs] = s_g[:, qs] * ea_q[L - 1:L, :] + res[L:]
        y = jnp.concatenate(ys, axis=1) + dsk_ref[:, pl.ds(off, SSD_GW)] * xg
        zg = z_ref[:, pl.ds(off, SSD_GW)].astype(F32)
        yg = y * (zg * jax.nn.sigmoid(zg))
        ms = jnp.mean(yg * yg, axis=-1, keepdims=True)
        yn = yg * lax.rsqrt(ms + EPS) * nw_ref[:, pl.ds(off, SSD_GW)]
        o_ref[:, pl.ds(off, SSD_GW)] = yn.astype(o_ref.dtype)
        return carry

    def group_pair(gp, carry):
        group_body(2 * gp, carry)
        return group_body(2 * gp + 1, carry)

    lax.fori_loop(0, SSD_GROUPS // 2, group_pair, 0)


def _ssd(proj, dt_raw, cwx, cbx, cwbc, cbbc, dtb, alog, dsk, nw, batch, seq):
    L = SSD_CHUNK
    nc = seq // L
    T = proj.shape[0]
    row = lambda b, c: b * nc + c
    const = lambda b, c: (0, 0)
    head_of_lane = jnp.arange(QUAD_W, dtype=jnp.int32) // SSD_HEADDIM
    qmask = (head_of_lane[None, None, :] == jnp.arange(QUAD, dtype=jnp.int32)[:, None, None])
    qmask = jnp.broadcast_to(qmask, (QUAD, L, QUAD_W)).astype(BF16)
    tap = jnp.arange(SSD_CONV - 1, dtype=jnp.int32)[:, None, None]
    t_out = jnp.arange(L, dtype=jnp.int32)[None, :, None]
    src = jnp.arange(2 * L, dtype=jnp.int32)[None, None, :]
    shift = (src == L + t_out - (SSD_CONV - 1) + tap).astype(BF16).reshape((SSD_CONV - 1) * L, 2 * L)
    return pl.pallas_call(
        _ssd_kernel,
        grid=(batch, nc),
        in_specs=[
            pl.BlockSpec((L, SSD_INNER), lambda b, c: (row(b, c), COL_Z // SSD_INNER)),
            pl.BlockSpec((L, SSD_INNER), lambda b, c: (row(b, c), COL_XS // SSD_INNER)),
            pl.BlockSpec((L, 2 * SSD_BC), lambda b, c: (row(b, c), COL_BC // (2 * SSD_BC))),
            pl.BlockSpec((L, LANES), lambda b, c: (row(b, c), 0)),
            pl.BlockSpec((SSD_CONV, SSD_INNER), const),
            pl.BlockSpec((1, SSD_INNER), const),
            pl.BlockSpec((SSD_CONV, 2 * SSD_BC), const),
            pl.BlockSpec((1, 2 * SSD_BC), const),
            pl.BlockSpec((1, LANES), const),
            pl.BlockSpec((1, LANES), const),
            pl.BlockSpec((1, SSD_INNER), const),
            pl.BlockSpec((1, SSD_INNER), const),
            pl.BlockSpec((QUAD, L, QUAD_W), lambda b, c: (0, 0, 0)),
            pl.BlockSpec(((SSD_CONV - 1) * L, 2 * L), const),
        ],
        out_specs=pl.BlockSpec((L, SSD_INNER), lambda b, c: (row(b, c), 0)),
        out_shape=jax.ShapeDtypeStruct((T, SSD_INNER), BF16),
        scratch_shapes=[
            pltpu.VMEM((L, SSD_INNER), BF16),
            pltpu.VMEM((L, 2 * SSD_BC), BF16),
            pltpu.VMEM((L, SSD_INNER), F32),
            pltpu.VMEM((L, SSD_BC), F32),
            pltpu.VMEM((L, SSD_BC), BF16),
            pltpu.VMEM((L, LANES), F32),
            pltpu.VMEM((LANES, L), F32),
            pltpu.VMEM((LANES, L), F32),
            pltpu.VMEM((SSD_GROUPS, SSD_STATE, SSD_GW), F32),
        ],
        compiler_params=_cparams(("arbitrary", "arbitrary")),
        name="ssd",
    )(proj, proj, proj, dt_raw, cwx, cbx, cwbc, cbbc, dtb, alog, dsk, nw, qmask, shift)


def _pack_bf16_pairs(v):
    bits = pltpu.bitcast(v.astype(BF16).astype(F32), jnp.uint32)
    return (bits[:, :HALF] >> 16) | (bits[:, HALF:] & jnp.uint32(0xFFFF0000))


def _unpack_lo(w):
    return pltpu.bitcast(w << 16, F32)


def _unpack_hi(w):
    return pltpu.bitcast(w & jnp.uint32(0xFFFF0000), F32)


def _mixout_kernel(pg_ref, yn_ref, gs_ref, x_ref, wso_ref, wo_ref, nw_ref, rw_ref, rb_ref,
                   h_ref, u_ref, er_ref, gt_ref, cnt_ref, run_scr, lg_scr, *, tm):
    i = pl.program_id(0)
    last = pl.num_programs(0) - 1

    @pl.when(i == 0)
    def _():
        run_scr[...] = jnp.zeros_like(run_scr)
        lg_scr[...] = jnp.zeros_like(lg_scr)

    def route_previous_tile(lg_prev):
        er, gt = _route(lg_prev, run_scr, tm, i > 0)
        er_ref[...] = er
        gt_ref[...] = gt
        cnt_ref[...] = run_scr[...]

    @pl.when(i < last)
    def _():
        lg_prev = lg_scr[...]
        ssd = jnp.dot(yn_ref[...], wso_ref[...], preferred_element_type=F32)
        merged = pg_ref[...].astype(F32) + jax.nn.sigmoid(gs_ref[...].astype(F32)) * ssd
        h = x_ref[...] + jnp.dot(merged.astype(BF16), wo_ref[...], preferred_element_type=F32)
        h_ref[...] = h
        ms = jnp.mean(h * h, axis=-1, keepdims=True)
        u = h * lax.rsqrt(ms + EPS) * nw_ref[...]
        u_hi = u.astype(BF16)
        u_ref[...] = _pack_bf16_pairs(u)
        u_lo = (u - u_hi.astype(F32)).astype(BF16)
        both = jnp.dot(u_hi, rw_ref[...], preferred_element_type=F32)
        lg = both[:, :LANES] + both[:, LANES:] + jnp.dot(u_lo, rw_ref[:, :LANES], preferred_element_type=F32)
        lg_scr[...] = lg + rb_ref[...]
        route_previous_tile(lg_prev)

    @pl.when(i == last)
    def _():
        route_previous_tile(lg_scr[...])


def _mixout(pool_g, yn, proj, x2, wso, wo, nw, rw, rb, tm):
    T = x2.shape[0]
    n = T // tm
    const = lambda i: (0, 0)
    cur = lambda i: jnp.minimum(i, n - 1)
    prev = lambda i: jnp.maximum(i - 1, 0)
    return pl.pallas_call(
        functools.partial(_mixout_kernel, tm=tm),
        grid=(n + 1,),
        in_specs=[
            pl.BlockSpec((tm, D_MODEL), lambda i: (cur(i), 0)),
            pl.BlockSpec((tm, SSD_INNER), lambda i: (cur(i), 0)),
            pl.BlockSpec((tm, D_MODEL), lambda i: (cur(i), COL_GS // D_MODEL)),
            pl.BlockSpec((tm, D_MODEL), lambda i: (cur(i), 0)),
            pl.BlockSpec((SSD_INNER, D_MODEL), const),
            pl.BlockSpec((D_MODEL, D_MODEL), const),
            pl.BlockSpec((1, D_MODEL), const),
            pl.BlockSpec((D_MODEL, 2 * LANES), const),
            pl.BlockSpec((1, LANES), const),
        ],
        out_specs=[
            pl.BlockSpec((tm, D_MODEL), lambda i: (cur(i), 0)),
            pl.BlockSpec((tm, HALF), lambda i: (cur(i), 0)),
            pl.BlockSpec((tm, LANES), lambda i: (prev(i), 0)),
            pl.BlockSpec((tm, LANES), lambda i: (prev(i), 0)),
            pl.BlockSpec((1, LANES), const),
        ],
        out_shape=[
            jax.ShapeDtypeStruct((T, D_MODEL), F32),
            jax.ShapeDtypeStruct((T, HALF), jnp.uint32),
            jax.ShapeDtypeStruct((T, LANES), jnp.int32),
            jax.ShapeDtypeStruct((T, LANES), F32),
            jax.ShapeDtypeStruct((1, LANES), F32),
        ],
        scratch_shapes=[pltpu.VMEM((1, LANES), F32), pltpu.VMEM((tm, LANES), F32)],
        compiler_params=_cparams(("arbitrary",)),
        name="mixout",
    )(pool_g, yn, proj, x2, wso, wo, nw, rw, rb)


def _route(lg, run_scr, tm, valid):
    lane = lax.broadcasted_iota(jnp.int32, (tm, LANES), 1)
    lg = jnp.where(lane < N_EXPERTS, lg, -jnp.inf)
    idxs, vals, hots = [], [], []
    for _ in range(TOP_K):
        m = jnp.max(lg, axis=-1, keepdims=True)
        idx = jnp.min(jnp.where(lg == m, lane, LANES), axis=-1, keepdims=True)
        hot = lane == idx
        lg = jnp.where(hot, -jnp.inf, lg)
        idxs.append(idx)
        vals.append(m)
        hots.append(hot)
    exps = [jnp.exp(v - vals[0]) for v in vals]
    den = exps[0] + exps[1] + exps[2] + exps[3]
    cnt = sum(jnp.where(hot, 1.0, 0.0) for hot in hots)
    rr = lax.broadcasted_iota(jnp.int32, (tm, tm), 0)
    cc = lax.broadcasted_iota(jnp.int32, (tm, tm), 1)
    before = jnp.where(rr > cc, 1.0, 0.0).astype(BF16)
    base = jnp.dot(before, cnt.astype(BF16), preferred_element_type=F32) + run_scr[...]
    er = jnp.zeros((tm, LANES), jnp.int32)
    gt = jnp.zeros((tm, LANES), F32)
    for k in range(TOP_K):
        rank = jnp.sum(jnp.where(hots[k], base, 0.0), axis=-1, keepdims=True)
        er = jnp.where(lane == k, idxs[k], er)
        er = jnp.where(lane == TOP_K + k, rank.astype(jnp.int32), er)
        gt = jnp.where(lane == k, exps[k] / den, gt)
    run_scr[...] += jnp.where(valid, jnp.sum(cnt, axis=0, keepdims=True), 0.0)
    return er, gt


def _dispatch_kernel(ps_ref, e_ref, r_ref, u_ref, xb_ref, dest_ref, sem, *, tm):
    def body(g, carry):
        r0 = pl.multiple_of(g * ROW_GROUP, ROW_GROUP)
        for i in range(ROW_GROUP):
            for k in range(TOP_K):
                j = (r0 + i) * TOP_K + k
                d = ps_ref[e_ref[j]] + r_ref[j]
                dest_ref[j] = d
                pltpu.make_async_copy(u_ref.at[pl.ds(r0 + i, 1)], xb_ref.at[pl.ds(d, 1)], sem).start()
        return carry

    lax.fori_loop(0, tm // ROW_GROUP, body, 0)
    for _ in range(TOP_K):
        pltpu.make_async_copy(u_ref, xb_ref.at[pl.ds(0, tm)], sem).wait()


def _dispatch(pstart, e_flat, r_flat, u_pk, n_slots, tm):
    T = u_pk.shape[0]
    grid_spec = pltpu.PrefetchScalarGridSpec(
        num_scalar_prefetch=1,
        grid=(T // tm,),
        in_specs=[
            pl.BlockSpec((tm * TOP_K,), lambda i, ps: (i,), memory_space=pltpu.SMEM),
            pl.BlockSpec((tm * TOP_K,), lambda i, ps: (i,), memory_space=pltpu.SMEM),
            pl.BlockSpec((tm, HALF), lambda i, ps: (i, 0)),
        ],
        out_specs=[
            pl.BlockSpec(memory_space=pl.ANY),
            pl.BlockSpec((tm * TOP_K,), lambda i, ps: (i,), memory_space=pltpu.SMEM),
        ],
        scratch_shapes=[pltpu.SemaphoreType.DMA(())],
    )
    return pl.pallas_call(
        functools.partial(_dispatch_kernel, tm=tm),
        grid_spec=grid_spec,
        out_shape=[
            jax.ShapeDtypeStruct((n_slots, HALF), jnp.uint32),
            jax.ShapeDtypeStruct((T * TOP_K,), jnp.int32),
        ],
        compiler_params=_cparams(("arbitrary",)),
        name="dispatch",
    )(pstart, e_flat, r_flat, u_pk)


def _row_blocks(nrows, tm, sub, weights, block):
    half = sub // 2
    nhalf = (nrows + half - 1) // half
    nblk = nhalf // 2
    per_big = min(FFN_BIG_BLOCK, tm) // sub
    nbig = tm // (per_big * sub)
    for b in range(nbig):
        @pl.when(nblk >= (b + 1) * per_big)
        def _():
            block(b * per_big * sub, per_big * sub, weights())

    def rest(sb, carry):
        block(pl.multiple_of(sb * sub, sub), sub, weights())
        return carry

    lax.fori_loop(jnp.minimum(nblk // per_big, nbig) * per_big, nblk, rest, 0)

    @pl.when(nhalf % 2 == 1)
    def _():
        block(pl.multiple_of(nblk * sub, sub), half, weights())


def _ffn_up_kernel(te_ref, nu_ref, tr_ref, x_ref, wg_ref, wl_ref, bg_ref, bl_ref, o_ref, *, tm, sub):
    nrows = tr_ref[pl.program_id(0)]

    def weights():
        return wg_ref[0].astype(BF16), wl_ref[0].astype(BF16)

    def block(r0, nr, w):
        packed = x_ref[pl.ds(r0, nr), :]
        x = jnp.concatenate([_unpack_lo(packed).astype(BF16), _unpack_hi(packed).astype(BF16)], axis=1)
        glu = jnp.minimum(jnp.dot(x, w[0], preferred_element_type=F32) + bg_ref[0], SWIGLU_LIMIT)
        lin = jnp.clip(jnp.dot(x, w[1], preferred_element_type=F32) + bl_ref[0], -SWIGLU_LIMIT, SWIGLU_LIMIT)
        act = glu * jax.nn.sigmoid(SWIGLU_ALPHA * glu) * (lin + 1.0)
        o_ref[pl.ds(r0, nr), :] = act.astype(o_ref.dtype)

    _row_blocks(nrows, tm, sub, weights, block)


def _ffn_down_kernel(te_ref, nu_ref, tr_ref, a_ref, wa_ref, wb_ref, ba_ref, bb_ref, o_ref, *, tm, sub):
    nrows = tr_ref[pl.program_id(0)]

    def weights():
        return wa_ref[0].astype(BF16), wb_ref[0].astype(BF16)

    def block(r0, nr, w):
        a = a_ref[pl.ds(r0, nr), :]
        ya = jnp.dot(a, w[0], preferred_element_type=F32) + ba_ref[0]
        yb = jnp.dot(a, w[1], preferred_element_type=F32) + bb_ref[0]
        lo = pltpu.bitcast(ya.astype(BF16).astype(F32), jnp.uint32) >> 16
        hi = pltpu.bitcast(yb.astype(BF16).astype(F32), jnp.uint32) & jnp.uint32(0xFFFF0000)
        o_ref[pl.ds(r0, nr), :] = lo | hi

    _row_blocks(nrows, tm, sub, weights, block)


def _ffn(tile_expert, n_used, tile_rows, xb, w_up, b_up3, w_down, b_down3, tm, sub, fc, nc):
    n_slots = xb.shape[0]
    n_tiles = n_slots // tm
    nf = D_FF // fc
    nj = HALF // nc

    def live(i, nu):
        return jnp.minimum(i, nu[0] - 1)

    def frozen(i, j, nu, last):
        return jnp.where(i < nu[0], j, last)

    up_spec = pltpu.PrefetchScalarGridSpec(
        num_scalar_prefetch=3,
        grid=(n_tiles, nf),
        in_specs=[
            pl.BlockSpec((tm, HALF), lambda i, f, te, nu, tr: (live(i, nu), 0)),
            pl.BlockSpec((1, D_MODEL, fc), lambda i, f, te, nu, tr: (te[i], 0, frozen(i, f, nu, nf - 1))),
            pl.BlockSpec((1, D_MODEL, fc), lambda i, f, te, nu, tr: (te[i], 0, nf + frozen(i, f, nu, nf - 1))),
            pl.BlockSpec((1, 1, fc), lambda i, f, te, nu, tr: (te[i], 0, frozen(i, f, nu, nf - 1))),
            pl.BlockSpec((1, 1, fc), lambda i, f, te, nu, tr: (te[i], 0, nf + frozen(i, f, nu, nf - 1))),
        ],
        out_specs=pl.BlockSpec((tm, fc), lambda i, f, te, nu, tr: (live(i, nu), frozen(i, f, nu, nf - 1))),
    )
    act = pl.pallas_call(
        functools.partial(_ffn_up_kernel, tm=tm, sub=sub),
        grid_spec=up_spec,
        out_shape=jax.ShapeDtypeStruct((n_slots, D_FF), BF16),
        compiler_params=_cparams(("arbitrary", "arbitrary")),
        name="ffn_up",
    )(tile_expert, n_used, tile_rows, xb, w_up, w_up, b_up3, b_up3)

    down_spec = pltpu.PrefetchScalarGridSpec(
        num_scalar_prefetch=3,
        grid=(n_tiles, nj),
        in_specs=[
            pl.BlockSpec((tm, D_FF), lambda i, j, te, nu, tr: (live(i, nu), 0)),
            pl.BlockSpec((1, D_FF, nc), lambda i, j, te, nu, tr: (te[i], 0, frozen(i, j, nu, nj - 1))),
            pl.BlockSpec((1, D_FF, nc), lambda i, j, te, nu, tr: (te[i], 0, nj + frozen(i, j, nu, nj - 1))),
            pl.BlockSpec((1, 1, nc), lambda i, j, te, nu, tr: (te[i], 0, frozen(i, j, nu, nj - 1))),
            pl.BlockSpec((1, 1, nc), lambda i, j, te, nu, tr: (te[i], 0, nj + frozen(i, j, nu, nj - 1))),
        ],
        out_specs=pl.BlockSpec((tm, nc), lambda i, j, te, nu, tr: (live(i, nu), frozen(i, j, nu, nj - 1))),
    )
    return pl.pallas_call(
        functools.partial(_ffn_down_kernel, tm=tm, sub=sub),
        grid_spec=down_spec,
        out_shape=jax.ShapeDtypeStruct((n_slots, HALF), jnp.uint32),
        compiler_params=_cparams(("arbitrary", "arbitrary")),
        name="ffn_down",
    )(tile_expert, n_used, tile_rows, act, w_down, w_down, b_down3, b_down3)


def _combine_kernel(d_ref, dn_ref, h_ref, g_ref, nw_ref, yb_ref, o_ref, ybuf, sem, *, tm):
    step = pl.program_id(0)
    slot = step % 2

    def gather_rows(idx_ref, s):
        def body(g, carry):
            r0 = pl.multiple_of(g * ROW_GROUP, ROW_GROUP)
            for i in range(ROW_GROUP):
                for k in range(TOP_K):
                    d = idx_ref[(r0 + i) * TOP_K + k]
                    pltpu.make_async_copy(yb_ref.at[pl.ds(d, 1)], ybuf.at[s, k, pl.ds(r0 + i, 1)],
                                          sem.at[s]).start()
            return carry

        lax.fori_loop(0, tm // ROW_GROUP, body, 0)

    @pl.when(step == 0)
    def _():
        gather_rows(d_ref, 0)

    @pl.when(step + 1 < pl.num_programs(0))
    def _():
        gather_rows(dn_ref, 1 - slot)

    for k in range(TOP_K):
        pltpu.make_async_copy(yb_ref.at[pl.ds(0, tm)], ybuf.at[slot, k], sem.at[slot]).wait()
    lo = h_ref[:, :HALF]
    hi = h_ref[:, HALF:]
    for k in range(TOP_K):
        w = ybuf[slot, k]
        g = g_ref[:, k:k + 1]
        lo = lo + g * _unpack_lo(w)
        hi = hi + g * _unpack_hi(w)
    ms = (jnp.sum(lo * lo, axis=-1, keepdims=True) + jnp.sum(hi * hi, axis=-1, keepdims=True)) / D_MODEL
    scale = lax.rsqrt(ms + EPS)
    o_ref[:, :HALF] = lo * scale * nw_ref[:, :HALF]
    o_ref[:, HALF:] = hi * scale * nw_ref[:, HALF:]


def _combine(dest, h, gates, nw, yb, tm):
    T = h.shape[0]
    n = T // tm
    return pl.pallas_call(
        functools.partial(_combine_kernel, tm=tm),
        grid=(n,),
        in_specs=[
            pl.BlockSpec((tm * TOP_K,), lambda i: (i,), memory_space=pltpu.SMEM),
            pl.BlockSpec((tm * TOP_K,), lambda i: (jnp.minimum(i + 1, n - 1),), memory_space=pltpu.SMEM),
            pl.BlockSpec((tm, D_MODEL), lambda i: (i, 0)),
            pl.BlockSpec((tm, LANES), lambda i: (i, 0)),
            pl.BlockSpec((1, D_MODEL), lambda i: (0, 0)),
            pl.BlockSpec(memory_space=pl.ANY),
        ],
        out_specs=pl.BlockSpec((tm, D_MODEL), lambda i: (i, 0)),
        out_shape=jax.ShapeDtypeStruct((T, D_MODEL), F32),
        scratch_shapes=[pltpu.VMEM((2, TOP_K, tm, HALF), jnp.uint32), pltpu.SemaphoreType.DMA((2,))],
        compiler_params=_cparams(("arbitrary",)),
        name="combine",
    )(dest, dest, h, gates, nw, yb)


def _row(v, width=None):
    v = v.astype(F32).reshape(1, -1)
    if width is not None and v.shape[1] < width:
        v = jnp.pad(v, ((0, 0), (0, width - v.shape[1])))
    return v


def _tile(n, pref):
    t = pref
    while n % t:
        t //= 2
    return t


def kernel(x, norm_mix_w, w_in, w_pool, pool_scale, conv_w, conv_b, dt_bias, a_log, d_skip,
           ssd_norm_w, w_ssd_out, w_out, norm_ffn_w, router_w, router_b, w_up, b_up,
           w_down, b_down, norm_final_w):
    B, S, D = x.shape
    T = B * S
    x2 = x.reshape(T, D)

    o_p, o_z, o_xbc = 0, D_MODEL, D_MODEL + SSD_INNER
    o_dt = o_xbc + SSD_INNER + 2 * SSD_BC
    o_gp = o_dt + SSD_HEADS
    o_gs = o_gp + D_MODEL
    w_main = jnp.concatenate(
        [w_in[:, o_z:o_z + SSD_INNER], w_in[:, o_xbc:o_dt], w_in[:, o_p:o_p + D_MODEL],
         w_in[:, o_gp:o_gp + D_MODEL], w_in[:, o_gs:o_gs + D_MODEL]], axis=1).astype(BF16)
    w_dt = jnp.pad(w_in[:, o_dt:o_dt + SSD_HEADS], ((0, 0), (0, LANES - SSD_HEADS))).astype(BF16)

    proj, dt_raw = _inproj(x2, _row(norm_mix_w), w_main, w_dt, _tile(T, 1024), 2048)

    pool_g = _pool(proj, w_pool.astype(BF16), _row(pool_scale), _tile(S, 256), S)

    yn = _ssd(proj, dt_raw,
              conv_w[:, :SSD_INNER].astype(F32), _row(conv_b[:SSD_INNER]),
              conv_w[:, SSD_INNER:].astype(F32), _row(conv_b[SSD_INNER:]),
              _row(dt_bias, LANES), _row(a_log, LANES),
              _row(jnp.repeat(d_skip, SSD_HEADDIM)), _row(ssd_norm_w), B, S)

    rw = jnp.pad(router_w.astype(F32), ((0, 0), (0, LANES - N_EXPERTS)))
    rw_hi = rw.astype(BF16)
    rw_lo = (rw - rw_hi.astype(F32)).astype(BF16)
    h, u_pk, er, gates, cnt = _mixout(pool_g, yn, proj, x2, w_ssd_out.astype(BF16), w_out.astype(BF16),
                                      _row(norm_ffn_w), jnp.concatenate([rw_hi, rw_lo], axis=1),
                                      _row(router_b, LANES), _tile(T, 256))

    TK = T * TOP_K
    tm_ffn = 2304 if TK >= 32768 else 256
    sub = 256
    n_tiles = -(-TK // tm_ffn) + N_EXPERTS
    counts = cnt[0, :N_EXPERTS].astype(jnp.int32)
    tiles_e = (counts + tm_ffn - 1) // tm_ffn
    tend = jnp.cumsum(tiles_e)
    tstart = tend - tiles_e
    pstart = (tstart * tm_ffn).astype(jnp.int32)
    tile_id = jnp.arange(n_tiles, dtype=jnp.int32)
    tile_expert = jnp.minimum(jnp.searchsorted(tend, tile_id, side='right'), N_EXPERTS - 1).astype(jnp.int32)
    tile_rows = jnp.clip(counts[tile_expert] - (tile_id - tstart[tile_expert]) * tm_ffn, 0, tm_ffn).astype(jnp.int32)
    n_used = tend[-1].astype(jnp.int32).reshape(1)
    e_flat = er[:, :TOP_K].reshape(TK)
    r_flat = er[:, TOP_K:2 * TOP_K].reshape(TK)

    xb, dest = _dispatch(pstart, e_flat, r_flat, u_pk, n_tiles * tm_ffn, _tile(T, 512))
    yb = _ffn(tile_expert, n_used, tile_rows, xb, w_up, b_up.reshape(N_EXPERTS, 1, 2 * D_FF),
              w_down, b_down.reshape(N_EXPERTS, 1, D_MODEL), tm_ffn, sub, 512, 512)
    out = _combine(dest, h, gates, _row(norm_final_w), yb, _tile(T, 256))
    return out.reshape(B, S, D)
```

```python
import functools

import jax
import jax.numpy as jnp
from jax import lax
from jax.experimental import pallas as pl
from jax.experimental.pallas import tpu as pltpu

F32 = jnp.float32
BF16 = jnp.bfloat16

D_MODEL = 2048
EPS = 1e-5
POOL_GROUPS = 4
POOL_WINDOWS = (2, 4, 8, 16)
POOL_GW = D_MODEL // POOL_GROUPS
SSD_INNER = 2 * D_MODEL
SSD_HEADDIM = 64
SSD_HEADS = SSD_INNER // SSD_HEADDIM
SSD_GROUPS = 8
SSD_HPG = SSD_HEADS // SSD_GROUPS
SSD_STATE = 128
SSD_CONV = 4
SSD_CHUNK = 128
SSD_GW = SSD_HPG * SSD_HEADDIM
SSD_BC = SSD_GROUPS * SSD_STATE
N_EXPERTS = 32
TOP_K = 4
D_FF = D_MODEL
SWIGLU_ALPHA = 1.702
SWIGLU_LIMIT = 7.0

LANES = 128
HALF = D_MODEL // 2
CONV_SLAB = 256
QUAD = 4
QUAD_W = QUAD * SSD_HEADDIM
WPREP_TN = 1024
FFN_BIG_BLOCK = 1024
ROW_GROUP = 8
VMEM_LIMIT = 56 * 1024 * 1024

COL_Z = 0
COL_XS = SSD_INNER
COL_BC = 2 * SSD_INNER
COL_P = 2 * SSD_INNER + 2 * SSD_BC
COL_GP = COL_P + D_MODEL
COL_GS = COL_GP + D_MODEL
PROJ_COLS = COL_GS + D_MODEL

NEG_BIG = -1e30


def _cparams(sem):
    return pltpu.CompilerParams(dimension_semantics=sem, vmem_limit_bytes=VMEM_LIMIT)


def _wprep_kernel(a_ref, b_ref, o_ref, *, n_aligned, lane_off):
    j = pl.program_id(0)

    @pl.when(j < n_aligned)
    def _():
        o_ref[...] = a_ref[...].astype(o_ref.dtype)

    @pl.when(j >= n_aligned)
    def _():
        o_ref[...] = jnp.concatenate([a_ref[:, lane_off:], b_ref[:, :lane_off]], axis=1).astype(o_ref.dtype)


def _wprep(w_in, o_z, o_xbc, o_p, o_gp):
    tn = WPREP_TN
    z_end, xbc_end = SSD_INNER, COL_P
    lane_off = o_gp % LANES
    gp_base = o_gp - lane_off

    def src_col(j):
        c = j * tn
        return jnp.where(c < z_end, o_z + c,
                         jnp.where(c < xbc_end, o_xbc + (c - z_end),
                                   jnp.where(c < COL_GP, o_p + (c - xbc_end), gp_base + (c - COL_GP))))

    n_aligned = COL_GP // tn
    spill = lambda j: jnp.where(j >= n_aligned, (src_col(j) + tn) // LANES, (gp_base + tn) // LANES)
    return pl.pallas_call(
        functools.partial(_wprep_kernel, n_aligned=n_aligned, lane_off=lane_off),
        grid=(PROJ_COLS // tn,),
        in_specs=[
            pl.BlockSpec((D_MODEL, tn), lambda j: (0, src_col(j) // tn)),
            pl.BlockSpec((D_MODEL, LANES), lambda j: (0, spill(j))),
        ],
        out_specs=pl.BlockSpec((D_MODEL, tn), lambda j: (0, j)),
        out_shape=jax.ShapeDtypeStruct((D_MODEL, PROJ_COLS), BF16),
        compiler_params=_cparams(("parallel",)),
        name="wprep",
    )(w_in, w_in)


def _inproj_kernel(x_ref, nw_ref, w_ref, wdt_ref, o_ref, dt_ref, u_scr):
    @pl.when(pl.program_id(1) == 0)
    def _():
        x = x_ref[...]
        ms = jnp.mean(x * x, axis=-1, keepdims=True)
        u = (x * lax.rsqrt(ms + EPS) * nw_ref[...]).astype(BF16)
        u_scr[...] = u
        dt_ref[...] = jnp.dot(u, wdt_ref[...], preferred_element_type=F32)

    o_ref[...] = jnp.dot(u_scr[...], w_ref[...], preferred_element_type=F32).astype(o_ref.dtype)


def _inproj(x2, norm_w, w_main, w_dt, tm, tn):
    T = x2.shape[0]
    return pl.pallas_call(
        _inproj_kernel,
        grid=(T // tm, PROJ_COLS // tn),
        in_specs=[
            pl.BlockSpec((tm, D_MODEL), lambda i, j: (i, 0)),
            pl.BlockSpec((1, D_MODEL), lambda i, j: (0, 0)),
            pl.BlockSpec((D_MODEL, tn), lambda i, j: (0, j)),
            pl.BlockSpec((D_MODEL, LANES), lambda i, j: (0, 0)),
        ],
        out_specs=[
            pl.BlockSpec((tm, tn), lambda i, j: (i, j)),
            pl.BlockSpec((tm, LANES), lambda i, j: (i, 0)),
        ],
        out_shape=[
            jax.ShapeDtypeStruct((T, PROJ_COLS), BF16),
            jax.ShapeDtypeStruct((T, LANES), F32),
        ],
        scratch_shapes=[pltpu.VMEM((tm, D_MODEL), BF16)],
        compiler_params=_cparams(("parallel", "arbitrary")),
        name="inproj",
    )(x2, norm_w, w_main, w_dt)


def _pool_kernel(prev_ref, p_ref, g_ref, band_ref, hband_ref, wp_ref, ps_ref, o_ref, *, tm, seq):
    row0 = (pl.program_id(0) * tm) % seq
    pos = row0 + lax.broadcasted_iota(jnp.int32, (tm, 1), 0)
    for g, w in enumerate(POOL_WINDOWS):
        cols = slice(g * POOL_GW, (g + 1) * POOL_GW)
        cur = p_ref[:, cols]
        prev = prev_ref[:, cols]
        prev = jnp.where(row0 == 0, jnp.zeros_like(prev), prev)
        acc = (jnp.dot(band_ref[g], cur, preferred_element_type=F32)
               + jnp.dot(hband_ref[g], prev, preferred_element_type=F32))
        cnt = jnp.minimum(pos + 1, w).astype(F32)
        pooled = acc / cnt - cur.astype(F32)
        y = jnp.dot(pooled.astype(BF16), wp_ref[g], preferred_element_type=F32)
        y = y * ps_ref[:, cols] * jax.nn.sigmoid(g_ref[:, cols].astype(F32))
        o_ref[:, cols] = y.astype(o_ref.dtype)


def _pool(proj, w_pool, pool_scale, tm, seq):
    T = proj.shape[0]
    hist = max(POOL_WINDOWS)
    t_out = jnp.arange(tm, dtype=jnp.int32)[None, :, None]
    src = jnp.arange(hist + tm, dtype=jnp.int32)[None, None, :]
    win = jnp.asarray(POOL_WINDOWS, jnp.int32)[:, None, None]
    band = jnp.logical_and(src > hist + t_out - win, src <= hist + t_out).astype(BF16)
    return pl.pallas_call(
        functools.partial(_pool_kernel, tm=tm, seq=seq),
        grid=(T // tm,),
        in_specs=[
            pl.BlockSpec((hist, D_MODEL), lambda i: (jnp.maximum(i * (tm // hist) - 1, 0), COL_P // D_MODEL)),
            pl.BlockSpec((tm, D_MODEL), lambda i: (i, COL_P // D_MODEL)),
            pl.BlockSpec((tm, D_MODEL), lambda i: (i, COL_GP // D_MODEL)),
            pl.BlockSpec((POOL_GROUPS, tm, tm), lambda i: (0, 0, 0)),
            pl.BlockSpec((POOL_GROUPS, tm, hist), lambda i: (0, 0, 0)),
            pl.BlockSpec((POOL_GROUPS, POOL_GW, POOL_GW), lambda i: (0, 0, 0)),
            pl.BlockSpec((1, D_MODEL), lambda i: (0, 0)),
        ],
        out_specs=pl.BlockSpec((tm, D_MODEL), lambda i: (i, 0)),
        out_shape=jax.ShapeDtypeStruct((T, D_MODEL), BF16),
        compiler_params=_cparams(("parallel",)),
        name="pool",
    )(proj, proj, proj, band[:, :, hist:], band[:, :, :hist], w_pool, pool_scale)


def _split3(v):
    hi = v.astype(BF16)
    r1 = v - hi.astype(F32)
    mid = r1.astype(BF16)
    lo = (r1 - mid.astype(F32)).astype(BF16)
    return hi, mid, lo


def _ssd_kernel(z_ref, xs_ref, bc_ref, dt_ref, cwx_ref, cbx_ref, cwbc_ref, cbbc_ref,
                dtb_ref, alog_ref, dsk_ref, nw_ref, qm_ref, sh_ref, o_ref,
                prev_x, prev_bc, xs_scr, b_scr, c_scr, a_scr, qt_scr, wt_scr, state):
    L = SSD_CHUNK
    c = pl.program_id(1)

    @pl.when(c == 0)
    def _():
        prev_x[...] = jnp.zeros_like(prev_x)
        prev_bc[...] = jnp.zeros_like(prev_bc)
        state[...] = jnp.zeros_like(state)

    def conv_slab(src_ref, prev_ref, cw_ref, cb_ref, col):
        cols = slice(col, col + CONV_SLAB)
        raw = src_ref[:, cols]
        both = jnp.concatenate([prev_ref[:, cols], raw], axis=0)
        prev_ref[:, cols] = raw
        acc = cb_ref[:, cols] + cw_ref[SSD_CONV - 1:SSD_CONV, cols] * raw.astype(F32)
        for k in range(SSD_CONV - 1):
            shifted = jnp.dot(sh_ref[k * L:(k + 1) * L, :], both, preferred_element_type=F32)
            acc = acc + cw_ref[k:k + 1, cols] * shifted
        return acc * jax.nn.sigmoid(acc)

    for s in range(SSD_INNER // CONV_SLAB):
        xs_scr[:, s * CONV_SLAB:(s + 1) * CONV_SLAB] = conv_slab(xs_ref, prev_x, cwx_ref, cbx_ref, s * CONV_SLAB)
    for s in range(SSD_BC // CONV_SLAB):
        b_scr[:, s * CONV_SLAB:(s + 1) * CONV_SLAB] = conv_slab(bc_ref, prev_bc, cwbc_ref, cbbc_ref, s * CONV_SLAB)
    for s in range(SSD_BC // CONV_SLAB):
        c_scr[:, s * CONV_SLAB:(s + 1) * CONV_SLAB] = conv_slab(
            bc_ref, prev_bc, cwbc_ref, cbbc_ref, SSD_BC + s * CONV_SLAB).astype(BF16)

    dtv = jax.nn.softplus(dt_ref[...] + dtb_ref[...])
    d_a = dtv * (-jnp.exp(alog_ref[...]))
    rr = lax.broadcasted_iota(jnp.int32, (L, L), 0)
    cc = lax.broadcasted_iota(jnp.int32, (L, L), 1)
    causal = rr >= cc
    low_half = cc < SSD_HEADDIM
    tri = jnp.where(causal, 1.0, 0.0).astype(BF16)
    a_cum = sum(jnp.dot(tri, part, preferred_element_type=F32) for part in _split3(d_a))
    a_t = a_cum.T
    dt_t = dtv.T
    a_scr[...] = a_cum
    qt_scr[...] = a_t - jnp.log(dt_t)
    wt_scr[...] = dt_t * jnp.exp(a_t[:, L - 1:L] - a_t)

    def group_body(g, carry):
        off = pl.multiple_of(g * SSD_GW, SSD_GW)
        offn = pl.multiple_of(g * SSD_STATE, SSD_STATE)
        xg = xs_scr[:, pl.ds(off, SSD_GW)]
        xg_b = xg.astype(BF16)
        b_f = b_scr[:, pl.ds(offn, SSD_STATE)]
        c_b = c_scr[:, pl.ds(offn, SSD_STATE)]
        b_t = b_f.T
        cbm = lax.dot_general(c_b, b_f.astype(BF16), (((1,), (1,)), ((), ())),
                              preferred_element_type=F32)
        s_g = state[g]
        y_off = jnp.dot(c_b, s_g.astype(BF16), preferred_element_type=F32)
        a_g = pltpu.roll(a_scr[...], (LANES - SSD_HPG * g) % LANES, 1)
        ys = []
        for quad in range(SSD_HPG // QUAD):
            qs = slice(quad * QUAD_W, (quad + 1) * QUAD_W)
            m_parts, b_parts, a_cols = [], [], []
            for jj in range(QUAD):
                j = quad * QUAD + jj
                h = g * SSD_HPG + j
                a_col = jnp.broadcast_to(a_g[:, j:j + 1], (L, L))
                seg = a_col - qt_scr[pl.ds(h, 1), :]
                m_parts.append((cbm * jnp.exp(jnp.where(causal, seg, NEG_BIG))).astype(BF16))
                b_parts.append((b_t * wt_scr[pl.ds(h, 1), :]).astype(BF16))
                a_cols.append(a_col)
            lhs = jnp.concatenate([jnp.concatenate(m_parts, axis=1), jnp.concatenate(b_parts, axis=1)], axis=0)
            xq = xg_b[:, qs]
            rhs = jnp.concatenate([xq * qm_ref[jj] for jj in range(QUAD)], axis=0)
            res = jnp.dot(lhs, rhs, preferred_element_type=F32)
            ea_q = jnp.concatenate(
                [jnp.exp(jnp.where(low_half, a_cols[2 * p], a_cols[2 * p + 1])) for p in range(QUAD // 2)], axis=1)
            ys.append(res[:L] + y_off[:, qs] * ea_q)
            state[g, :, qs] = s_g[:, qs] * ea_q[L - 1:L, :] + res[L:]
        y = jnp.concatenate(ys, axis=1) + dsk_ref[:, pl.ds(off, SSD_GW)] * xg
        zg = z_ref[:, pl.ds(off, SSD_GW)].astype(F32)
        yg = y * (zg * jax.nn.sigmoid(zg))
        ms = jnp.mean(yg * yg, axis=-1, keepdims=True)
        yn = yg * lax.rsqrt(ms + EPS) * nw_ref[:, pl.ds(off, SSD_GW)]
        o_ref[:, pl.ds(off, SSD_GW)] = yn.astype(o_ref.dtype)
        return carry

    def group_pair(gp, carry):
        group_body(2 * gp, carry)
        return group_body(2 * gp + 1, carry)

    lax.fori_loop(0, SSD_GROUPS // 2, group_pair, 0)


def _ssd(proj, dt_raw, cwx, cbx, cwbc, cbbc, dtb, alog, dsk, nw, batch, seq):
    L = SSD_CHUNK
    nc = seq // L
    T = proj.shape[0]
    row = lambda b, c: b * nc + c
    const = lambda b, c: (0, 0)
    head_of_lane = jnp.arange(QUAD_W, dtype=jnp.int32) // SSD_HEADDIM
    qmask = (head_of_lane[None, None, :] == jnp.arange(QUAD, dtype=jnp.int32)[:, None, None])
    qmask = jnp.broadcast_to(qmask, (QUAD, L, QUAD_W)).astype(BF16)
    tap = jnp.arange(SSD_CONV - 1, dtype=jnp.int32)[:, None, None]
    t_out = jnp.arange(L, dtype=jnp.int32)[None, :, None]
    src = jnp.arange(2 * L, dtype=jnp.int32)[None, None, :]
    shift = (src == L + t_out - (SSD_CONV - 1) + tap).astype(BF16).reshape((SSD_CONV - 1) * L, 2 * L)
    return pl.pallas_call(
        _ssd_kernel,
        grid=(batch, nc),
        in_specs=[
            pl.BlockSpec((L, SSD_INNER), lambda b, c: (row(b, c), COL_Z // SSD_INNER)),
            pl.BlockSpec((L, SSD_INNER), lambda b, c: (row(b, c), COL_XS // SSD_INNER)),
            pl.BlockSpec((L, 2 * SSD_BC), lambda b, c: (row(b, c), COL_BC // (2 * SSD_BC))),
            pl.BlockSpec((L, LANES), lambda b, c: (row(b, c), 0)),
            pl.BlockSpec((SSD_CONV, SSD_INNER), const),
            pl.BlockSpec((1, SSD_INNER), const),
            pl.BlockSpec((SSD_CONV, 2 * SSD_BC), const),
            pl.BlockSpec((1, 2 * SSD_BC), const),
            pl.BlockSpec((1, LANES), const),
            pl.BlockSpec((1, LANES), const),
            pl.BlockSpec((1, SSD_INNER), const),
            pl.BlockSpec((1, SSD_INNER), const),
            pl.BlockSpec((QUAD, L, QUAD_W), lambda b, c: (0, 0, 0)),
            pl.BlockSpec(((SSD_CONV - 1) * L, 2 * L), const),
        ],
        out_specs=pl.BlockSpec((L, SSD_INNER), lambda b, c: (row(b, c), 0)),
        out_shape=jax.ShapeDtypeStruct((T, SSD_INNER), BF16),
        scratch_shapes=[
            pltpu.VMEM((L, SSD_INNER), BF16),
            pltpu.VMEM((L, 2 * SSD_BC), BF16),
            pltpu.VMEM((L, SSD_INNER), F32),
            pltpu.VMEM((L, SSD_BC), F32),
            pltpu.VMEM((L, SSD_BC), BF16),
            pltpu.VMEM((L, LANES), F32),
            pltpu.VMEM((LANES, L), F32),
            pltpu.VMEM((LANES, L), F32),
            pltpu.VMEM((SSD_GROUPS, SSD_STATE, SSD_GW), F32),
        ],
        compiler_params=_cparams(("arbitrary", "arbitrary")),
        name="ssd",
    )(proj, proj, proj, dt_raw, cwx, cbx, cwbc, cbbc, dtb, alog, dsk, nw, qmask, shift)


def _pack_bf16_pairs(v):
    bits = pltpu.bitcast(v.astype(BF16).astype(F32), jnp.uint32)
    return (bits[:, :HALF] >> 16) | (bits[:, HALF:] & jnp.uint32(0xFFFF0000))


def _unpack_lo(w):
    return pltpu.bitcast(w << 16, F32)


def _unpack_hi(w):
    return pltpu.bitcast(w & jnp.uint32(0xFFFF0000), F32)


def _mixout_kernel(pg_ref, yn_ref, gs_ref, x_ref, wso_ref, wo_ref, nw_ref, rw_ref, rb_ref,
                   h_ref, u_ref, er_ref, gt_ref, cnt_ref, run_scr, lg_scr, *, tm):
    i = pl.program_id(0)
    last = pl.num_programs(0) - 1

    @pl.when(i == 0)
    def _():
        run_scr[...] = jnp.zeros_like(run_scr)
        lg_scr[...] = jnp.zeros_like(lg_scr)

    def route_previous_tile(lg_prev):
        er, gt = _route(lg_prev, run_scr, tm, i > 0)
        er_ref[...] = er
        gt_ref[...] = gt
        cnt_ref[...] = run_scr[...]

    @pl.when(i < last)
    def _():
        lg_prev = lg_scr[...]
        ssd = jnp.dot(yn_ref[...], wso_ref[...], preferred_element_type=F32)
        merged = pg_ref[...].astype(F32) + jax.nn.sigmoid(gs_ref[...].astype(F32)) * ssd
        h = x_ref[...] + jnp.dot(merged.astype(BF16), wo_ref[...], preferred_element_type=F32)
        h_ref[...] = h
        ms = jnp.mean(h * h, axis=-1, keepdims=True)
        u = h * lax.rsqrt(ms + EPS) * nw_ref[...]
        u_hi = u.astype(BF16)
        u_ref[...] = _pack_bf16_pairs(u)
        u_lo = (u - u_hi.astype(F32)).astype(BF16)
        both = jnp.dot(u_hi, rw_ref[...], preferred_element_type=F32)
        lg = both[:, :LANES] + both[:, LANES:] + jnp.dot(u_lo, rw_ref[:, :LANES], preferred_element_type=F32)
        lg_scr[...] = lg + rb_ref[...]
        route_previous_tile(lg_prev)

    @pl.when(i == last)
    def _():
        route_previous_tile(lg_scr[...])


def _mixout(pool_g, yn, proj, x2, wso, wo, nw, rw, rb, tm):
    T = x2.shape[0]
    n = T // tm
    const = lambda i: (0, 0)
    cur = lambda i: jnp.minimum(i, n - 1)
    prev = lambda i: jnp.maximum(i - 1, 0)
    return pl.pallas_call(
        functools.partial(_mixout_kernel, tm=tm),
        grid=(n + 1,),
        in_specs=[
            pl.BlockSpec((tm, D_MODEL), lambda i: (cur(i), 0)),
            pl.BlockSpec((tm, SSD_INNER), lambda i: (cur(i), 0)),
            pl.BlockSpec((tm, D_MODEL), lambda i: (cur(i), COL_GS // D_MODEL)),
            pl.BlockSpec((tm, D_MODEL), lambda i: (cur(i), 0)),
            pl.BlockSpec((SSD_INNER, D_MODEL), const),
            pl.BlockSpec((D_MODEL, D_MODEL), const),
            pl.BlockSpec((1, D_MODEL), const),
            pl.BlockSpec((D_MODEL, 2 * LANES), const),
            pl.BlockSpec((1, LANES), const),
        ],
        out_specs=[
            pl.BlockSpec((tm, D_MODEL), lambda i: (cur(i), 0)),
            pl.BlockSpec((tm, HALF), lambda i: (cur(i), 0)),
            pl.BlockSpec((tm, LANES), lambda i: (prev(i), 0)),
            pl.BlockSpec((tm, LANES), lambda i: (prev(i), 0)),
            pl.BlockSpec((1, LANES), const),
        ],
        out_shape=[
            jax.ShapeDtypeStruct((T, D_MODEL), F32),
            jax.ShapeDtypeStruct((T, HALF), jnp.uint32),
            jax.ShapeDtypeStruct((T, LANES), jnp.int32),
            jax.ShapeDtypeStruct((T, LANES), F32),
            jax.ShapeDtypeStruct((1, LANES), F32),
        ],
        scratch_shapes=[pltpu.VMEM((1, LANES), F32), pltpu.VMEM((tm, LANES), F32)],
        compiler_params=_cparams(("arbitrary",)),
        name="mixout",
    )(pool_g, yn, proj, x2, wso, wo, nw, rw, rb)


def _route(lg, run_scr, tm, valid):
    lane = lax.broadcasted_iota(jnp.int32, (tm, LANES), 1)
    lg = jnp.where(lane < N_EXPERTS, lg, -jnp.inf)
    idxs, vals, hots = [], [], []
    for _ in range(TOP_K):
        m = jnp.max(lg, axis=-1, keepdims=True)
        idx = jnp.min(jnp.where(lg == m, lane, LANES), axis=-1, keepdims=True)
        hot = lane == idx
        lg = jnp.where(hot, -jnp.inf, lg)
        idxs.append(idx)
        vals.append(m)
        hots.append(hot)
    exps = [jnp.exp(v - vals[0]) for v in vals]
    den = exps[0] + exps[1] + exps[2] + exps[3]
    cnt = sum(jnp.where(hot, 1.0, 0.0) for hot in hots)
    rr = lax.broadcasted_iota(jnp.int32, (tm, tm), 0)
    cc = lax.broadcasted_iota(jnp.int32, (tm, tm), 1)
    before = jnp.where(rr > cc, 1.0, 0.0).astype(BF16)
    base = jnp.dot(before, cnt.astype(BF16), preferred_element_type=F32) + run_scr[...]
    er = jnp.zeros((tm, LANES), jnp.int32)
    gt = jnp.zeros((tm, LANES), F32)
    for k in range(TOP_K):
        rank = jnp.sum(jnp.where(hots[k], base, 0.0), axis=-1, keepdims=True)
        er = jnp.where(lane == k, idxs[k], er)
        er = jnp.where(lane == TOP_K + k, rank.astype(jnp.int32), er)
        gt = jnp.where(lane == k, exps[k] / den, gt)
    run_scr[...] += jnp.where(valid, jnp.sum(cnt, axis=0, keepdims=True), 0.0)
    return er, gt


def _dispatch_kernel(ps_ref, e_ref, r_ref, u_ref, xb_ref, dest_ref, sem, *, tm):
    def body(g, carry):
        r0 = pl.multiple_of(g * ROW_GROUP, ROW_GROUP)
        for i in range(ROW_GROUP):
            for k in range(TOP_K):
                j = (r0 + i) * TOP_K + k
                d = ps_ref[e_ref[j]] + r_ref[j]
                dest_ref[j] = d
                pltpu.make_async_copy(u_ref.at[pl.ds(r0 + i, 1)], xb_ref.at[pl.ds(d, 1)], sem).start()
        return carry

    lax.fori_loop(0, tm // ROW_GROUP, body, 0)
    for _ in range(TOP_K):
        pltpu.make_async_copy(u_ref, xb_ref.at[pl.ds(0, tm)], sem).wait()


def _dispatch(pstart, e_flat, r_flat, u_pk, n_slots, tm):
    T = u_pk.shape[0]
    grid_spec = pltpu.PrefetchScalarGridSpec(
        num_scalar_prefetch=1,
        grid=(T // tm,),
        in_specs=[
            pl.BlockSpec((tm * TOP_K,), lambda i, ps: (i,), memory_space=pltpu.SMEM),
            pl.BlockSpec((tm * TOP_K,), lambda i, ps: (i,), memory_space=pltpu.SMEM),
            pl.BlockSpec((tm, HALF), lambda i, ps: (i, 0)),
        ],
        out_specs=[
            pl.BlockSpec(memory_space=pl.ANY),
            pl.BlockSpec((tm * TOP_K,), lambda i, ps: (i,), memory_space=pltpu.SMEM),
        ],
        scratch_shapes=[pltpu.SemaphoreType.DMA(())],
    )
    return pl.pallas_call(
        functools.partial(_dispatch_kernel, tm=tm),
        grid_spec=grid_spec,
        out_shape=[
            jax.ShapeDtypeStruct((n_slots, HALF), jnp.uint32),
            jax.ShapeDtypeStruct((T * TOP_K,), jnp.int32),
        ],
        compiler_params=_cparams(("arbitrary",)),
        name="dispatch",
    )(pstart, e_flat, r_flat, u_pk)


def _row_blocks(nrows, tm, sub, weights, block):
    nblk = (nrows + sub - 1) // sub
    per_big = min(FFN_BIG_BLOCK, tm) // sub
    nbig = tm // (per_big * sub)
    for b in range(nbig):
        @pl.when(nblk >= (b + 1) * per_big)
        def _():
            block(b * per_big * sub, per_big * sub, weights())

    def rest(sb, carry):
        block(pl.multiple_of(sb * sub, sub), sub, weights())
        return carry

    lax.fori_loop(jnp.minimum(nblk // per_big, nbig) * per_big, nblk, rest, 0)


def _ffn_up_kernel(te_ref, nu_ref, tr_ref, x_ref, wg_ref, wl_ref, bg_ref, bl_ref, o_ref, *, tm, sub):
    nrows = tr_ref[pl.program_id(0)]

    def weights():
        return wg_ref[0].astype(BF16), wl_ref[0].astype(BF16)

    def block(r0, nr, w):
        packed = x_ref[pl.ds(r0, nr), :]
        x = jnp.concatenate([_unpack_lo(packed).astype(BF16), _unpack_hi(packed).astype(BF16)], axis=1)
        glu = jnp.minimum(jnp.dot(x, w[0], preferred_element_type=F32) + bg_ref[0], SWIGLU_LIMIT)
        lin = jnp.clip(jnp.dot(x, w[1], preferred_element_type=F32) + bl_ref[0], -SWIGLU_LIMIT, SWIGLU_LIMIT)
        act = glu * jax.nn.sigmoid(SWIGLU_ALPHA * glu) * (lin + 1.0)
        o_ref[pl.ds(r0, nr), :] = act.astype(o_ref.dtype)

    _row_blocks(nrows, tm, sub, weights, block)


def _ffn_down_kernel(te_ref, nu_ref, tr_ref, a_ref, wa_ref, wb_ref, ba_ref, bb_ref, o_ref, *, tm, sub):
    nrows = tr_ref[pl.program_id(0)]

    def weights():
        return wa_ref[0].astype(BF16), wb_ref[0].astype(BF16)

    def block(r0, nr, w):
        a = a_ref[pl.ds(r0, nr), :]
        ya = jnp.dot(a, w[0], preferred_element_type=F32) + ba_ref[0]
        yb = jnp.dot(a, w[1], preferred_element_type=F32) + bb_ref[0]
        lo = pltpu.bitcast(ya.astype(BF16).astype(F32), jnp.uint32) >> 16
        hi = pltpu.bitcast(yb.astype(BF16).astype(F32), jnp.uint32) & jnp.uint32(0xFFFF0000)
        o_ref[pl.ds(r0, nr), :] = lo | hi

    _row_blocks(nrows, tm, sub, weights, block)


def _ffn(tile_expert, n_used, tile_rows, xb, w_up, b_up3, w_down, b_down3, tm, sub, fc, nc):
    n_slots = xb.shape[0]
    n_tiles = n_slots // tm
    nf = D_FF // fc
    nj = HALF // nc

    def live(i, nu):
        return jnp.minimum(i, nu[0] - 1)

    def frozen(i, j, nu, last):
        return jnp.where(i < nu[0], j, last)

    up_spec = pltpu.PrefetchScalarGridSpec(
        num_scalar_prefetch=3,
        grid=(n_tiles, nf),
        in_specs=[
            pl.BlockSpec((tm, HALF), lambda i, f, te, nu, tr: (live(i, nu), 0)),
            pl.BlockSpec((1, D_MODEL, fc), lambda i, f, te, nu, tr: (te[i], 0, frozen(i, f, nu, nf - 1))),
            pl.BlockSpec((1, D_MODEL, fc), lambda i, f, te, nu, tr: (te[i], 0, nf + frozen(i, f, nu, nf - 1))),
            pl.BlockSpec((1, 1, fc), lambda i, f, te, nu, tr: (te[i], 0, frozen(i, f, nu, nf - 1))),
            pl.BlockSpec((1, 1, fc), lambda i, f, te, nu, tr: (te[i], 0, nf + frozen(i, f, nu, nf - 1))),
        ],
        out_specs=pl.BlockSpec((tm, fc), lambda i, f, te, nu, tr: (live(i, nu), frozen(i, f, nu, nf - 1))),
    )
    act = pl.pallas_call(
        functools.partial(_ffn_up_kernel, tm=tm, sub=sub),
        grid_spec=up_spec,
        out_shape=jax.ShapeDtypeStruct((n_slots, D_FF), BF16),
        compiler_params=_cparams(("arbitrary", "arbitrary")),
        name="ffn_up",
    )(tile_expert, n_used, tile_rows, xb, w_up, w_up, b_up3, b_up3)

    down_spec = pltpu.PrefetchScalarGridSpec(
        num_scalar_prefetch=3,
        grid=(n_tiles, nj),
        in_specs=[
            pl.BlockSpec((tm, D_FF), lambda i, j, te, nu, tr: (live(i, nu), 0)),
            pl.BlockSpec((1, D_FF, nc), lambda i, j, te, nu, tr: (te[i], 0, frozen(i, j, nu, nj - 1))),
            pl.BlockSpec((1, D_FF, nc), lambda i, j, te, nu, tr: (te[i], 0, nj + frozen(i, j, nu, nj - 1))),
            pl.BlockSpec((1, 1, nc), lambda i, j, te, nu, tr: (te[i], 0, frozen(i, j, nu, nj - 1))),
            pl.BlockSpec((1, 1, nc), lambda i, j, te, nu, tr: (te[i], 0, nj + frozen(i, j, nu, nj - 1))),
        ],
        out_specs=pl.BlockSpec((tm, nc), lambda i, j, te, nu, tr: (live(i, nu), frozen(i, j, nu, nj - 1))),
    )
    return pl.pallas_call(
        functools.partial(_ffn_down_kernel, tm=tm, sub=sub),
        grid_spec=down_spec,
        out_shape=jax.ShapeDtypeStruct((n_slots, HALF), jnp.uint32),
        compiler_params=_cparams(("arbitrary", "arbitrary")),
        name="ffn_down",
    )(tile_expert, n_used, tile_rows, act, w_down, w_down, b_down3, b_down3)


def _combine_kernel(d_ref, dn_ref, h_ref, g_ref, nw_ref, yb_ref, o_ref, ybuf, sem, *, tm):
    step = pl.program_id(0)
    slot = step % 2

    def gather_rows(idx_ref, s):
        def body(g, carry):
            r0 = pl.multiple_of(g * ROW_GROUP, ROW_GROUP)
            for i in range(ROW_GROUP):
                for k in range(TOP_K):
                    d = idx_ref[(r0 + i) * TOP_K + k]
                    pltpu.make_async_copy(yb_ref.at[pl.ds(d, 1)], ybuf.at[s, k, pl.ds(r0 + i, 1)],
                                          sem.at[s]).start()
            return carry

        lax.fori_loop(0, tm // ROW_GROUP, body, 0)

    @pl.when(step == 0)
    def _():
        gather_rows(d_ref, 0)

    @pl.when(step + 1 < pl.num_programs(0))
    def _():
        gather_rows(dn_ref, 1 - slot)

    for k in range(TOP_K):
        pltpu.make_async_copy(yb_ref.at[pl.ds(0, tm)], ybuf.at[slot, k], sem.at[slot]).wait()
    lo = h_ref[:, :HALF]
    hi = h_ref[:, HALF:]
    for k in range(TOP_K):
        w = ybuf[slot, k]
        g = g_ref[:, k:k + 1]
        lo = lo + g * _unpack_lo(w)
        hi = hi + g * _unpack_hi(w)
    ms = (jnp.sum(lo * lo, axis=-1, keepdims=True) + jnp.sum(hi * hi, axis=-1, keepdims=True)) / D_MODEL
    scale = lax.rsqrt(ms + EPS)
    o_ref[:, :HALF] = lo * scale * nw_ref[:, :HALF]
    o_ref[:, HALF:] = hi * scale * nw_ref[:, HALF:]


def _combine(dest, h, gates, nw, yb, tm):
    T = h.shape[0]
    n = T // tm
    return pl.pallas_call(
        functools.partial(_combine_kernel, tm=tm),
        grid=(n,),
        in_specs=[
            pl.BlockSpec((tm * TOP_K,), lambda i: (i,), memory_space=pltpu.SMEM),
            pl.BlockSpec((tm * TOP_K,), lambda i: (jnp.minimum(i + 1, n - 1),), memory_space=pltpu.SMEM),
            pl.BlockSpec((tm, D_MODEL), lambda i: (i, 0)),
            pl.BlockSpec((tm, LANES), lambda i: (i, 0)),
            pl.BlockSpec((1, D_MODEL), lambda i: (0, 0)),
            pl.BlockSpec(memory_space=pl.ANY),
        ],
        out_specs=pl.BlockSpec((tm, D_MODEL), lambda i: (i, 0)),
        out_shape=jax.ShapeDtypeStruct((T, D_MODEL), F32),
        scratch_shapes=[pltpu.VMEM((2, TOP_K, tm, HALF), jnp.uint32), pltpu.SemaphoreType.DMA((2,))],
        compiler_params=_cparams(("arbitrary",)),
        name="combine",
    )(dest, dest, h, gates, nw, yb)


def _row(v, width=None):
    v = v.astype(F32).reshape(1, -1)
    if width is not None and v.shape[1] < width:
        v = jnp.pad(v, ((0, 0), (0, width - v.shape[1])))
    return v


def _tile(n, pref):
    t = pref
    while n % t:
        t //= 2
    return t


def kernel(x, norm_mix_w, w_in, w_pool, pool_scale, conv_w, conv_b, dt_bias, a_log, d_skip,
           ssd_norm_w, w_ssd_out, w_out, norm_ffn_w, router_w, router_b, w_up, b_up,
           w_down, b_down, norm_final_w):
    B, S, D = x.shape
    T = B * S
    x2 = x.reshape(T, D)

    o_p, o_z, o_xbc = 0, D_MODEL, D_MODEL + SSD_INNER
    o_dt = o_xbc + SSD_INNER + 2 * SSD_BC
    o_gp = o_dt + SSD_HEADS
    o_gs = o_gp + D_MODEL
    w_main = _wprep(w_in, o_z, o_xbc, o_p, o_gp)
    w_dt = jnp.pad(w_in[:, o_dt:o_dt + SSD_HEADS], ((0, 0), (0, LANES - SSD_HEADS))).astype(BF16)

    proj, dt_raw = _inproj(x2, _row(norm_mix_w), w_main, w_dt, _tile(T, 1024), 2048)

    pool_g = _pool(proj, w_pool.astype(BF16), _row(pool_scale), _tile(S, 256), S)

    yn = _ssd(proj, dt_raw,
              conv_w[:, :SSD_INNER].astype(F32), _row(conv_b[:SSD_INNER]),
              conv_w[:, SSD_INNER:].astype(F32), _row(conv_b[SSD_INNER:]),
              _row(dt_bias, LANES), _row(a_log, LANES),
              _row(jnp.repeat(d_skip, SSD_HEADDIM)), _row(ssd_norm_w), B, S)

    rw = jnp.pad(router_w.astype(F32), ((0, 0), (0, LANES - N_EXPERTS)))
    rw_hi = rw.astype(BF16)
    rw_lo = (rw - rw_hi.astype(F32)).astype(BF16)
    h, u_pk, er, gates, cnt = _mixout(pool_g, yn, proj, x2, w_ssd_out.astype(BF16), w_out.astype(BF16),
                                      _row(norm_ffn_w), jnp.concatenate([rw_hi, rw_lo], axis=1),
                                      _row(router_b, LANES), _tile(T, 256))

    TK = T * TOP_K
    tm_ffn = 2304 if TK >= 32768 else 256
    sub = 256
    n_tiles = -(-TK // tm_ffn) + N_EXPERTS
    counts = cnt[0, :N_EXPERTS].astype(jnp.int32)
    tiles_e = (counts + tm_ffn - 1) // tm_ffn
    tend = jnp.cumsum(tiles_e)
    tstart = tend - tiles_e
    pstart = (tstart * tm_ffn).astype(jnp.int32)
    tile_id = jnp.arange(n_tiles, dtype=jnp.int32)
    tile_expert = jnp.minimum(jnp.searchsorted(tend, tile_id, side='right'), N_EXPERTS - 1).astype(jnp.int32)
    tile_rows = jnp.clip(counts[tile_expert] - (tile_id - tstart[tile_expert]) * tm_ffn, 0, tm_ffn).astype(jnp.int32)
    n_used = tend[-1].astype(jnp.int32).reshape(1)
    e_flat = er[:, :TOP_K].reshape(TK)
    r_flat = er[:, TOP_K:2 * TOP_K].reshape(TK)

    xb, dest = _dispatch(pstart, e_flat, r_flat, u_pk, n_tiles * tm_ffn, _tile(T, 512))
    yb = _ffn(tile_expert, n_used, tile_rows, xb, w_up, b_up.reshape(N_EXPERTS, 1, 2 * D_FF),
              w_down, b_down.reshape(N_EXPERTS, 1, D_MODEL), tm_ffn, sub, 512, 512)
    out = _combine(dest, h, gates, _row(norm_final_w), yb, _tile(T, 256))
    return out.reshape(B, S, D)
```

```python
import functools

import jax
import jax.numpy as jnp
from jax import lax
from jax.experimental import pallas as pl
from jax.experimental.pallas import tpu as pltpu

F32 = jnp.float32
BF16 = jnp.bfloat16

D_MODEL = 2048
EPS = 1e-5
POOL_GROUPS = 4
POOL_WINDOWS = (2, 4, 8, 16)
POOL_GW = D_MODEL // POOL_GROUPS
SSD_INNER = 2 * D_MODEL
SSD_HEADDIM = 64
SSD_HEADS = SSD_INNER // SSD_HEADDIM
SSD_GROUPS = 8
SSD_HPG = SSD_HEADS // SSD_GROUPS
SSD_STATE = 128
SSD_CONV = 4
SSD_CHUNK = 128
SSD_GW = SSD_HPG * SSD_HEADDIM
SSD_BC = SSD_GROUPS * SSD_STATE
N_EXPERTS = 32
TOP_K = 4
D_FF = D_MODEL
SWIGLU_ALPHA = 1.702
SWIGLU_LIMIT = 7.0

LANES = 128
BF16_ROWS = 16
HALF = D_MODEL // 2
CONV_SLAB = 256
QUAD = 4
QUAD_W = QUAD * SSD_HEADDIM
INPROJ_TN = 2048
FFN_BIG_BLOCK = 1024
ROW_GROUP = 8
VMEM_LIMIT = 56 * 1024 * 1024

COL_Z = 0
COL_XS = SSD_INNER
COL_BC = 2 * SSD_INNER
COL_P = 2 * SSD_INNER + 2 * SSD_BC
COL_GP = COL_P + D_MODEL
COL_GS = COL_GP + D_MODEL
PROJ_COLS = COL_GS + D_MODEL

NEG_BIG = -1e30


def _cparams(sem):
    return pltpu.CompilerParams(dimension_semantics=sem, vmem_limit_bytes=VMEM_LIMIT)


_NT = (((1,), (1,)), ((), ()))


def _inproj_kernel(x_ref, nw_ref, w_ref, wdt_ref, o_ref, dt_ref, u_scr):
    @pl.when(pl.program_id(1) == 0)
    def _():
        x = x_ref[...]
        ms = jnp.mean(x * x, axis=-1, keepdims=True)
        u = (x * lax.rsqrt(ms + EPS) * nw_ref[...]).astype(BF16)
        u_scr[...] = u
        dt_ref[...] = lax.dot_general(u, wdt_ref[...], _NT, preferred_element_type=F32)

    o_ref[...] = lax.dot_general(u_scr[...], w_ref[...], _NT, preferred_element_type=F32).astype(o_ref.dtype)


def _inproj(x2, norm_w, w_t, w_dt_t, row_of_block, tm, tn):
    T = x2.shape[0]

    def w_rows(i, j):
        row = row_of_block[-1]
        for b in range(len(row_of_block) - 2, -1, -1):
            row = jnp.where(j == b, row_of_block[b], row)
        return pl.multiple_of(row, BF16_ROWS), 0

    return pl.pallas_call(
        _inproj_kernel,
        grid=(T // tm, PROJ_COLS // tn),
        in_specs=[
            pl.BlockSpec((tm, D_MODEL), lambda i, j: (i, 0)),
            pl.BlockSpec((1, D_MODEL), lambda i, j: (0, 0)),
            pl.BlockSpec((pl.Element(tn), pl.Element(D_MODEL)), w_rows),
            pl.BlockSpec((LANES, D_MODEL), lambda i, j: (0, 0)),
        ],
        out_specs=[
            pl.BlockSpec((tm, tn), lambda i, j: (i, j)),
            pl.BlockSpec((tm, LANES), lambda i, j: (i, 0)),
        ],
        out_shape=[
            jax.ShapeDtypeStruct((T, PROJ_COLS), BF16),
            jax.ShapeDtypeStruct((T, LANES), F32),
        ],
        scratch_shapes=[pltpu.VMEM((tm, D_MODEL), BF16)],
        compiler_params=_cparams(("parallel", "arbitrary")),
        name="inproj",
    )(x2, norm_w, w_t, w_dt_t)


def _pool_kernel(prev_ref, p_ref, g_ref, band_ref, hband_ref, wp_ref, ps_ref, o_ref, *, tm, seq):
    row0 = (pl.program_id(0) * tm) % seq
    pos = row0 + lax.broadcasted_iota(jnp.int32, (tm, 1), 0)
    for g, w in enumerate(POOL_WINDOWS):
        cols = slice(g * POOL_GW, (g + 1) * POOL_GW)
        cur = p_ref[:, cols]
        prev = prev_ref[:, cols]
        prev = jnp.where(row0 == 0, jnp.zeros_like(prev), prev)
        acc = (jnp.dot(band_ref[g], cur, preferred_element_type=F32)
               + jnp.dot(hband_ref[g], prev, preferred_element_type=F32))
        cnt = jnp.minimum(pos + 1, w).astype(F32)
        pooled = acc / cnt - cur.astype(F32)
        y = jnp.dot(pooled.astype(BF16), wp_ref[g], preferred_element_type=F32)
        y = y * ps_ref[:, cols] * jax.nn.sigmoid(g_ref[:, cols].astype(F32))
        o_ref[:, cols] = y.astype(o_ref.dtype)


def _pool(proj, w_pool, pool_scale, tm, seq):
    T = proj.shape[0]
    hist = max(POOL_WINDOWS)
    t_out = jnp.arange(tm, dtype=jnp.int32)[None, :, None]
    src = jnp.arange(hist + tm, dtype=jnp.int32)[None, None, :]
    win = jnp.asarray(POOL_WINDOWS, jnp.int32)[:, None, None]
    band = jnp.logical_and(src > hist + t_out - win, src <= hist + t_out).astype(BF16)
    return pl.pallas_call(
        functools.partial(_pool_kernel, tm=tm, seq=seq),
        grid=(T // tm,),
        in_specs=[
            pl.BlockSpec((hist, D_MODEL), lambda i: (jnp.maximum(i * (tm // hist) - 1, 0), COL_P // D_MODEL)),
            pl.BlockSpec((tm, D_MODEL), lambda i: (i, COL_P // D_MODEL)),
            pl.BlockSpec((tm, D_MODEL), lambda i: (i, COL_GP // D_MODEL)),
            pl.BlockSpec((POOL_GROUPS, tm, tm), lambda i: (0, 0, 0)),
            pl.BlockSpec((POOL_GROUPS, tm, hist), lambda i: (0, 0, 0)),
            pl.BlockSpec((POOL_GROUPS, POOL_GW, POOL_GW), lambda i: (0, 0, 0)),
            pl.BlockSpec((1, D_MODEL), lambda i: (0, 0)),
        ],
        out_specs=pl.BlockSpec((tm, D_MODEL), lambda i: (i, 0)),
        out_shape=jax.ShapeDtypeStruct((T, D_MODEL), BF16),
        compiler_params=_cparams(("parallel",)),
        name="pool",
    )(proj, proj, proj, band[:, :, hist:], band[:, :, :hist], w_pool, pool_scale)


def _split3(v):
    hi = v.astype(BF16)
    r1 = v - hi.astype(F32)
    mid = r1.astype(BF16)
    lo = (r1 - mid.astype(F32)).astype(BF16)
    return hi, mid, lo


def _ssd_kernel(z_ref, xs_ref, bc_ref, dt_ref, cwx_ref, cbx_ref, cwbc_ref, cbbc_ref,
                dtb_ref, alog_ref, dsk_ref, nw_ref, qm_ref, sh_ref, o_ref,
                prev_x, prev_bc, xs_scr, b_scr, c_scr, a_scr, qt_scr, wt_scr, state):
    L = SSD_CHUNK
    c = pl.program_id(1)

    @pl.when(c == 0)
    def _():
        prev_x[...] = jnp.zeros_like(prev_x)
        prev_bc[...] = jnp.zeros_like(prev_bc)
        state[...] = jnp.zeros_like(state)

    def conv_slab(src_ref, prev_ref, cw_ref, cb_ref, col):
        cols = slice(col, col + CONV_SLAB)
        raw = src_ref[:, cols]
        both = jnp.concatenate([prev_ref[:, cols], raw], axis=0)
        prev_ref[:, cols] = raw
        acc = cb_ref[:, cols] + cw_ref[SSD_CONV - 1:SSD_CONV, cols] * raw.astype(F32)
        for k in range(SSD_CONV - 1):
            shifted = jnp.dot(sh_ref[k * L:(k + 1) * L, :], both, preferred_element_type=F32)
            acc = acc + cw_ref[k:k + 1, cols] * shifted
        return acc * jax.nn.sigmoid(acc)

    for s in range(SSD_INNER // CONV_SLAB):
        xs_scr[:, s * CONV_SLAB:(s + 1) * CONV_SLAB] = conv_slab(xs_ref, prev_x, cwx_ref, cbx_ref, s * CONV_SLAB)
    for s in range(SSD_BC // CONV_SLAB):
        b_scr[:, s * CONV_SLAB:(s + 1) * CONV_SLAB] = conv_slab(bc_ref, prev_bc, cwbc_ref, cbbc_ref, s * CONV_SLAB)
    for s in range(SSD_BC // CONV_SLAB):
        c_scr[:, s * CONV_SLAB:(s + 1) * CONV_SLAB] = conv_slab(
            bc_ref, prev_bc, cwbc_ref, cbbc_ref, SSD_BC + s * CONV_SLAB).astype(BF16)

    dtv = jax.nn.softplus(dt_ref[...] + dtb_ref[...])
    d_a = dtv * (-jnp.exp(alog_ref[...]))
    rr = lax.broadcasted_iota(jnp.int32, (L, L), 0)
    cc = lax.broadcasted_iota(jnp.int32, (L, L), 1)
    causal = rr >= cc
    low_half = cc < SSD_HEADDIM
    tri = jnp.where(causal, 1.0, 0.0).astype(BF16)
    a_cum = sum(jnp.dot(tri, part, preferred_element_type=F32) for part in _split3(d_a))
    a_t = a_cum.T
    dt_t = dtv.T
    a_scr[...] = a_cum
    qt_scr[...] = a_t - jnp.log(dt_t)
    wt_scr[...] = dt_t * jnp.exp(a_t[:, L - 1:L] - a_t)

    def group_body(g, carry):
        off = pl.multiple_of(g * SSD_GW, SSD_GW)
        offn = pl.multiple_of(g * SSD_STATE, SSD_STATE)
        xg = xs_scr[:, pl.ds(off, SSD_GW)]
        xg_b = xg.astype(BF16)
        b_f = b_scr[:, pl.ds(offn, SSD_STATE)]
        c_b = c_scr[:, pl.ds(offn, SSD_STATE)]
        b_t = b_f.T
        cbm = lax.dot_general(c_b, b_f.astype(BF16), (((1,), (1,)), ((), ())),
                              preferred_element_type=F32)
        s_g = state[g]
        y_off = jnp.dot(c_b, s_g.astype(BF16), preferred_element_type=F32)
        a_g = pltpu.roll(a_scr[...], (LANES - SSD_HPG * g) % LANES, 1)
        ys = []
        for quad in range(SSD_HPG // QUAD):
            qs = slice(quad * QUAD_W, (quad + 1) * QUAD_W)
            m_parts, b_parts, a_cols = [], [], []
            for jj in range(QUAD):
                j = quad * QUAD + jj
                h = g * SSD_HPG + j
                a_col = jnp.broadcast_to(a_g[:, j:j + 1], (L, L))
                seg = a_col - qt_scr[pl.ds(h, 1), :]
                m_parts.append((cbm * jnp.exp(jnp.where(causal, seg, NEG_BIG))).astype(BF16))
                b_parts.append((b_t * wt_scr[pl.ds(h, 1), :]).astype(BF16))
                a_cols.append(a_col)
            lhs = jnp.concatenate([jnp.concatenate(m_parts, axis=1), jnp.concatenate(b_parts, axis=1)], axis=0)
            xq = xg_b[:, qs]
            rhs = jnp.concatenate([xq * qm_ref[jj] for jj in range(QUAD)], axis=0)
            res = jnp.dot(lhs, rhs, preferred_element_type=F32)
            ea_q = jnp.concatenate(
                [jnp.exp(jnp.where(low_half, a_cols[2 * p], a_cols[2 * p + 1])) for p in range(QUAD // 2)], axis=1)
            ys.append(res[:L] + y_off[:, qs] * ea_q)
            state[g, :, qs] = s_g[:, qs] * ea_q[L - 1:L, :] + res[L:]
        y = jnp.concatenate(ys, axis=1) + dsk_ref[:, pl.ds(off, SSD_GW)] * xg
        zg = z_ref[:, pl.ds(off, SSD_GW)].astype(F32)
        yg = y * (zg * jax.nn.sigmoid(zg))
        ms = jnp.mean(yg * yg, axis=-1, keepdims=True)
        yn = yg * lax.rsqrt(ms + EPS) * nw_ref[:, pl.ds(off, SSD_GW)]
        o_ref[:, pl.ds(off, SSD_GW)] = yn.astype(o_ref.dtype)
        return carry

    def group_pair(gp, carry):
        group_body(2 * gp, carry)
        return group_body(2 * gp + 1, carry)

    lax.fori_loop(0, SSD_GROUPS // 2, group_pair, 0)


def _ssd(proj, dt_raw, cwx, cbx, cwbc, cbbc, dtb, alog, dsk, nw, batch, seq):
    L = SSD_CHUNK
    nc = seq // L
    T = proj.shape[0]
    row = lambda b, c: b * nc + c
    const = lambda b, c: (0, 0)
    head_of_lane = jnp.arange(QUAD_W, dtype=jnp.int32) // SSD_HEADDIM
    qmask = (head_of_lane[None, None, :] == jnp.arange(QUAD, dtype=jnp.int32)[:, None, None])
    qmask = jnp.broadcast_to(qmask, (QUAD, L, QUAD_W)).astype(BF16)
    tap = jnp.arange(SSD_CONV - 1, dtype=jnp.int32)[:, None, None]
    t_out = jnp.arange(L, dtype=jnp.int32)[None, :, None]
    src = jnp.arange(2 * L, dtype=jnp.int32)[None, None, :]
    shift = (src == L + t_out - (SSD_CONV - 1) + tap).astype(BF16).reshape((SSD_CONV - 1) * L, 2 * L)
    return pl.pallas_call(
        _ssd_kernel,
        grid=(batch, nc),
        in_specs=[
            pl.BlockSpec((L, SSD_INNER), lambda b, c: (row(b, c), COL_Z // SSD_INNER)),
            pl.BlockSpec((L, SSD_INNER), lambda b, c: (row(b, c), COL_XS // SSD_INNER)),
            pl.BlockSpec((L, 2 * SSD_BC), lambda b, c: (row(b, c), COL_BC // (2 * SSD_BC))),
            pl.BlockSpec((L, LANES), lambda b, c: (row(b, c), 0)),
            pl.BlockSpec((SSD_CONV, SSD_INNER), const),
            pl.BlockSpec((1, SSD_INNER), const),
            pl.BlockSpec((SSD_CONV, 2 * SSD_BC), const),
            pl.BlockSpec((1, 2 * SSD_BC), const),
            pl.BlockSpec((1, LANES), const),
            pl.BlockSpec((1, LANES), const),
            pl.BlockSpec((1, SSD_INNER), const),
            pl.BlockSpec((1, SSD_INNER), const),
            pl.BlockSpec((QUAD, L, QUAD_W), lambda b, c: (0, 0, 0)),
            pl.BlockSpec(((SSD_CONV - 1) * L, 2 * L), const),
        ],
        out_specs=pl.BlockSpec((L, SSD_INNER), lambda b, c: (row(b, c), 0)),
        out_shape=jax.ShapeDtypeStruct((T, SSD_INNER), BF16),
        scratch_shapes=[
            pltpu.VMEM((L, SSD_INNER), BF16),
            pltpu.VMEM((L, 2 * SSD_BC), BF16),
            pltpu.VMEM((L, SSD_INNER), F32),
            pltpu.VMEM((L, SSD_BC), F32),
            pltpu.VMEM((L, SSD_BC), BF16),
            pltpu.VMEM((L, LANES), F32),
            pltpu.VMEM((LANES, L), F32),
            pltpu.VMEM((LANES, L), F32),
            pltpu.VMEM((SSD_GROUPS, SSD_STATE, SSD_GW), F32),
        ],
        compiler_params=_cparams(("arbitrary", "arbitrary")),
        name="ssd",
    )(proj, proj, proj, dt_raw, cwx, cbx, cwbc, cbbc, dtb, alog, dsk, nw, qmask, shift)


def _pack_bf16_pairs(v):
    bits = pltpu.bitcast(v.astype(BF16).astype(F32), jnp.uint32)
    return (bits[:, :HALF] >> 16) | (bits[:, HALF:] & jnp.uint32(0xFFFF0000))


def _unpack_lo(w):
    return pltpu.bitcast(w << 16, F32)


def _unpack_hi(w):
    return pltpu.bitcast(w & jnp.uint32(0xFFFF0000), F32)


def _mixout_kernel(pg_ref, yn_ref, gs_ref, x_ref, wso_ref, wo_ref, nw_ref, rw_ref, rb_ref,
                   h_ref, u_ref, er_ref, gt_ref, cnt_ref, run_scr, lg_scr, *, tm):
    i = pl.program_id(0)
    last = pl.num_programs(0) - 1

    @pl.when(i == 0)
    def _():
        run_scr[...] = jnp.zeros_like(run_scr)
        lg_scr[...] = jnp.zeros_like(lg_scr)

    def route_previous_tile(lg_prev):
        er, gt = _route(lg_prev, run_scr, tm, i > 0)
        er_ref[...] = er
        gt_ref[...] = gt
        cnt_ref[...] = run_scr[...]

    @pl.when(i < last)
    def _():
        lg_prev = lg_scr[...]
        ssd = jnp.dot(yn_ref[...], wso_ref[...], preferred_element_type=F32)
        merged = pg_ref[...].astype(F32) + jax.nn.sigmoid(gs_ref[...].astype(F32)) * ssd
        h = x_ref[...] + jnp.dot(merged.astype(BF16), wo_ref[...], preferred_element_type=F32)
        h_ref[...] = h
        ms = jnp.mean(h * h, axis=-1, keepdims=True)
        u = h * lax.rsqrt(ms + EPS) * nw_ref[...]
        u_hi = u.astype(BF16)
        u_ref[...] = _pack_bf16_pairs(u)
        u_lo = (u - u_hi.astype(F32)).astype(BF16)
        both = jnp.dot(u_hi, rw_ref[...], preferred_element_type=F32)
        lg = both[:, :LANES] + both[:, LANES:] + jnp.dot(u_lo, rw_ref[:, :LANES], preferred_element_type=F32)
        lg_scr[...] = lg + rb_ref[...]
        route_previous_tile(lg_prev)

    @pl.when(i == last)
    def _():
        route_previous_tile(lg_scr[...])


def _mixout(pool_g, yn, proj, x2, wso, wo, nw, rw, rb, tm):
    T = x2.shape[0]
    n = T // tm
    const = lambda i: (0, 0)
    cur = lambda i: jnp.minimum(i, n - 1)
    prev = lambda i: jnp.maximum(i - 1, 0)
    return pl.pallas_call(
        functools.partial(_mixout_kernel, tm=tm),
        grid=(n + 1,),
        in_specs=[
            pl.BlockSpec((tm, D_MODEL), lambda i: (cur(i), 0)),
            pl.BlockSpec((tm, SSD_INNER), lambda i: (cur(i), 0)),
            pl.BlockSpec((tm, D_MODEL), lambda i: (cur(i), COL_GS // D_MODEL)),
            pl.BlockSpec((tm, D_MODEL), lambda i: (cur(i), 0)),
            pl.BlockSpec((SSD_INNER, D_MODEL), const),
            pl.BlockSpec((D_MODEL, D_MODEL), const),
            pl.BlockSpec((1, D_MODEL), const),
            pl.BlockSpec((D_MODEL, 2 * LANES), const),
            pl.BlockSpec((1, LANES), const),
        ],
        out_specs=[
            pl.BlockSpec((tm, D_MODEL), lambda i: (cur(i), 0)),
            pl.BlockSpec((tm, HALF), lambda i: (cur(i), 0)),
            pl.BlockSpec((tm, LANES), lambda i: (prev(i), 0)),
            pl.BlockSpec((tm, LANES), lambda i: (prev(i), 0)),
            pl.BlockSpec((1, LANES), const),
        ],
        out_shape=[
            jax.ShapeDtypeStruct((T, D_MODEL), F32),
            jax.ShapeDtypeStruct((T, HALF), jnp.uint32),
            jax.ShapeDtypeStruct((T, LANES), jnp.int32),
            jax.ShapeDtypeStruct((T, LANES), F32),
            jax.ShapeDtypeStruct((1, LANES), F32),
        ],
        scratch_shapes=[pltpu.VMEM((1, LANES), F32), pltpu.VMEM((tm, LANES), F32)],
        compiler_params=_cparams(("arbitrary",)),
        name="mixout",
    )(pool_g, yn, proj, x2, wso, wo, nw, rw, rb)


def _route(lg, run_scr, tm, valid):
    lane = lax.broadcasted_iota(jnp.int32, (tm, LANES), 1)
    lg = jnp.where(lane < N_EXPERTS, lg, -jnp.inf)
    idxs, vals, hots = [], [], []
    for _ in range(TOP_K):
        m = jnp.max(lg, axis=-1, keepdims=True)
        idx = jnp.min(jnp.where(lg == m, lane, LANES), axis=-1, keepdims=True)
        hot = lane == idx
        lg = jnp.where(hot, -jnp.inf, lg)
        idxs.append(idx)
        vals.append(m)
        hots.append(hot)
    exps = [jnp.exp(v - vals[0]) for v in vals]
    den = exps[0] + exps[1] + exps[2] + exps[3]
    cnt = sum(jnp.where(hot, 1.0, 0.0) for hot in hots)
    rr = lax.broadcasted_iota(jnp.int32, (tm, tm), 0)
    cc = lax.broadcasted_iota(jnp.int32, (tm, tm), 1)
    before = jnp.where(rr > cc, 1.0, 0.0).astype(BF16)
    base = jnp.dot(before, cnt.astype(BF16), preferred_element_type=F32) + run_scr[...]
    er = jnp.zeros((tm, LANES), jnp.int32)
    gt = jnp.zeros((tm, LANES), F32)
    for k in range(TOP_K):
        rank = jnp.sum(jnp.where(hots[k], base, 0.0), axis=-1, keepdims=True)
        er = jnp.where(lane == k, idxs[k], er)
        er = jnp.where(lane == TOP_K + k, rank.astype(jnp.int32), er)
        gt = jnp.where(lane == k, exps[k] / den, gt)
    run_scr[...] += jnp.where(valid, jnp.sum(cnt, axis=0, keepdims=True), 0.0)
    return er, gt


def _dispatch_kernel(ps_ref, e_ref, r_ref, u_ref, xb_ref, dest_ref, sem, *, tm):
    def body(g, carry):
        r0 = pl.multiple_of(g * ROW_GROUP, ROW_GROUP)
        for i in range(ROW_GROUP):
            for k in range(TOP_K):
                j = (r0 + i) * TOP_K + k
                d = ps_ref[e_ref[j]] + r_ref[j]
                dest_ref[j] = d
                pltpu.make_async_copy(u_ref.at[pl.ds(r0 + i, 1)], xb_ref.at[pl.ds(d, 1)], sem).start()
        return carry

    lax.fori_loop(0, tm // ROW_GROUP, body, 0)
    for _ in range(TOP_K):
        pltpu.make_async_copy(u_ref, xb_ref.at[pl.ds(0, tm)], sem).wait()


def _dispatch(pstart, e_flat, r_flat, u_pk, n_slots, tm):
    T = u_pk.shape[0]
    grid_spec = pltpu.PrefetchScalarGridSpec(
        num_scalar_prefetch=1,
        grid=(T // tm,),
        in_specs=[
            pl.BlockSpec((tm * TOP_K,), lambda i, ps: (i,), memory_space=pltpu.SMEM),
            pl.BlockSpec((tm * TOP_K,), lambda i, ps: (i,), memory_space=pltpu.SMEM),
            pl.BlockSpec((tm, HALF), lambda i, ps: (i, 0)),
        ],
        out_specs=[
            pl.BlockSpec(memory_space=pl.ANY),
            pl.BlockSpec((tm * TOP_K,), lambda i, ps: (i,), memory_space=pltpu.SMEM),
        ],
        scratch_shapes=[pltpu.SemaphoreType.DMA(())],
    )
    return pl.pallas_call(
        functools.partial(_dispatch_kernel, tm=tm),
        grid_spec=grid_spec,
        out_shape=[
            jax.ShapeDtypeStruct((n_slots, HALF), jnp.uint32),
            jax.ShapeDtypeStruct((T * TOP_K,), jnp.int32),
        ],
        compiler_params=_cparams(("arbitrary",)),
        name="dispatch",
    )(pstart, e_flat, r_flat, u_pk)


def _row_blocks(nrows, tm, sub, weights, block):
    nblk = (nrows + sub - 1) // sub
    per_big = min(FFN_BIG_BLOCK, tm) // sub
    nbig = tm // (per_big * sub)
    for b in range(nbig):
        @pl.when(nblk >= (b + 1) * per_big)
        def _():
            block(b * per_big * sub, per_big * sub, weights())

    def rest(sb, carry):
        block(pl.multiple_of(sb * sub, sub), sub, weights())
        return carry

    lax.fori_loop(jnp.minimum(nblk // per_big, nbig) * per_big, nblk, rest, 0)


def _ffn_up_kernel(te_ref, nu_ref, tr_ref, x_ref, wg_ref, wl_ref, bg_ref, bl_ref, o_ref, *, tm, sub):
    nrows = tr_ref[pl.program_id(0)]

    def weights():
        return wg_ref[0].astype(BF16), wl_ref[0].astype(BF16)

    def block(r0, nr, w):
        packed = x_ref[pl.ds(r0, nr), :]
        x = jnp.concatenate([_unpack_lo(packed).astype(BF16), _unpack_hi(packed).astype(BF16)], axis=1)
        glu = jnp.minimum(jnp.dot(x, w[0], preferred_element_type=F32) + bg_ref[0], SWIGLU_LIMIT)
        lin = jnp.clip(jnp.dot(x, w[1], preferred_element_type=F32) + bl_ref[0], -SWIGLU_LIMIT, SWIGLU_LIMIT)
        act = glu * jax.nn.sigmoid(SWIGLU_ALPHA * glu) * (lin + 1.0)
        o_ref[pl.ds(r0, nr), :] = act.astype(o_ref.dtype)

    _row_blocks(nrows, tm, sub, weights, block)


def _ffn_down_kernel(te_ref, nu_ref, tr_ref, a_ref, wa_ref, wb_ref, ba_ref, bb_ref, o_ref, *, tm, sub):
    nrows = tr_ref[pl.program_id(0)]

    def weights():
        return wa_ref[0].astype(BF16), wb_ref[0].astype(BF16)

    def block(r0, nr, w):
        a = a_ref[pl.ds(r0, nr), :]
        ya = jnp.dot(a, w[0], preferred_element_type=F32) + ba_ref[0]
        yb = jnp.dot(a, w[1], preferred_element_type=F32) + bb_ref[0]
        lo = pltpu.bitcast(ya.astype(BF16).astype(F32), jnp.uint32) >> 16
        hi = pltpu.bitcast(yb.astype(BF16).astype(F32), jnp.uint32) & jnp.uint32(0xFFFF0000)
        o_ref[pl.ds(r0, nr), :] = lo | hi

    _row_blocks(nrows, tm, sub, weights, block)


def _ffn(tile_expert, n_used, tile_rows, xb, w_up, b_up3, w_down, b_down3, tm, sub, fc, nc):
    n_slots = xb.shape[0]
    n_tiles = n_slots // tm
    nf = D_FF // fc
    nj = HALF // nc

    def live(i, nu):
        return jnp.minimum(i, nu[0] - 1)

    def frozen(i, j, nu, last):
        return jnp.where(i < nu[0], j, last)

    up_spec = pltpu.PrefetchScalarGridSpec(
        num_scalar_prefetch=3,
        grid=(n_tiles, nf),
        in_specs=[
            pl.BlockSpec((tm, HALF), lambda i, f, te, nu, tr: (live(i, nu), 0)),
            pl.BlockSpec((1, D_MODEL, fc), lambda i, f, te, nu, tr: (te[i], 0, frozen(i, f, nu, nf - 1))),
            pl.BlockSpec((1, D_MODEL, fc), lambda i, f, te, nu, tr: (te[i], 0, nf + frozen(i, f, nu, nf - 1))),
            pl.BlockSpec((1, 1, fc), lambda i, f, te, nu, tr: (te[i], 0, frozen(i, f, nu, nf - 1))),
            pl.BlockSpec((1, 1, fc), lambda i, f, te, nu, tr: (te[i], 0, nf + frozen(i, f, nu, nf - 1))),
        ],
        out_specs=pl.BlockSpec((tm, fc), lambda i, f, te, nu, tr: (live(i, nu), frozen(i, f, nu, nf - 1))),
    )
    act = pl.pallas_call(
        functools.partial(_ffn_up_kernel, tm=tm, sub=sub),
        grid_spec=up_spec,
        out_shape=jax.ShapeDtypeStruct((n_slots, D_FF), BF16),
        compiler_params=_cparams(("arbitrary", "arbitrary")),
        name="ffn_up",
    )(tile_expert, n_used, tile_rows, xb, w_up, w_up, b_up3, b_up3)

    down_spec = pltpu.PrefetchScalarGridSpec(
        num_scalar_prefetch=3,
        grid=(n_tiles, nj),
        in_specs=[
            pl.BlockSpec((tm, D_FF), lambda i, j, te, nu, tr: (live(i, nu), 0)),
            pl.BlockSpec((1, D_FF, nc), lambda i, j, te, nu, tr: (te[i], 0, frozen(i, j, nu, nj - 1))),
            pl.BlockSpec((1, D_FF, nc), lambda i, j, te, nu, tr: (te[i], 0, nj + frozen(i, j, nu, nj - 1))),
            pl.BlockSpec((1, 1, nc), lambda i, j, te, nu, tr: (te[i], 0, frozen(i, j, nu, nj - 1))),
            pl.BlockSpec((1, 1, nc), lambda i, j, te, nu, tr: (te[i], 0, nj + frozen(i, j, nu, nj - 1))),
        ],
        out_specs=pl.BlockSpec((tm, nc), lambda i, j, te, nu, tr: (live(i, nu), frozen(i, j, nu, nj - 1))),
    )
    return pl.pallas_call(
        functools.partial(_ffn_down_kernel, tm=tm, sub=sub),
        grid_spec=down_spec,
        out_shape=jax.ShapeDtypeStruct((n_slots, HALF), jnp.uint32),
        compiler_params=_cparams(("arbitrary", "arbitrary")),
        name="ffn_down",
    )(tile_expert, n_used, tile_rows, act, w_down, w_down, b_down3, b_down3)


def _combine_kernel(d_ref, dn_ref, h_ref, g_ref, nw_ref, yb_ref, o_ref, ybuf, sem, *, tm):
    step = pl.program_id(0)
    slot = step % 2

    def gather_rows(idx_ref, s):
        def body(g, carry):
            r0 = pl.multiple_of(g * ROW_GROUP, ROW_GROUP)
            for i in range(ROW_GROUP):
                for k in range(TOP_K):
                    d = idx_ref[(r0 + i) * TOP_K + k]
                    pltpu.make_async_copy(yb_ref.at[pl.ds(d, 1)], ybuf.at[s, k, pl.ds(r0 + i, 1)],
                                          sem.at[s]).start()
            return carry

        lax.fori_loop(0, tm // ROW_GROUP, body, 0)

    @pl.when(step == 0)
    def _():
        gather_rows(d_ref, 0)

    @pl.when(step + 1 < pl.num_programs(0))
    def _():
        gather_rows(dn_ref, 1 - slot)

    for k in range(TOP_K):
        pltpu.make_async_copy(yb_ref.at[pl.ds(0, tm)], ybuf.at[slot, k], sem.at[slot]).wait()
    lo = h_ref[:, :HALF]
    hi = h_ref[:, HALF:]
    for k in range(TOP_K):
        w = ybuf[slot, k]
        g = g_ref[:, k:k + 1]
        lo = lo + g * _unpack_lo(w)
        hi = hi + g * _unpack_hi(w)
    ms = (jnp.sum(lo * lo, axis=-1, keepdims=True) + jnp.sum(hi * hi, axis=-1, keepdims=True)) / D_MODEL
    scale = lax.rsqrt(ms + EPS)
    o_ref[:, :HALF] = lo * scale * nw_ref[:, :HALF]
    o_ref[:, HALF:] = hi * scale * nw_ref[:, HALF:]


def _combine(dest, h, gates, nw, yb, tm):
    T = h.shape[0]
    n = T // tm
    return pl.pallas_call(
        functools.partial(_combine_kernel, tm=tm),
        grid=(n,),
        in_specs=[
            pl.BlockSpec((tm * TOP_K,), lambda i: (i,), memory_space=pltpu.SMEM),
            pl.BlockSpec((tm * TOP_K,), lambda i: (jnp.minimum(i + 1, n - 1),), memory_space=pltpu.SMEM),
            pl.BlockSpec((tm, D_MODEL), lambda i: (i, 0)),
            pl.BlockSpec((tm, LANES), lambda i: (i, 0)),
            pl.BlockSpec((1, D_MODEL), lambda i: (0, 0)),
            pl.BlockSpec(memory_space=pl.ANY),
        ],
        out_specs=pl.BlockSpec((tm, D_MODEL), lambda i: (i, 0)),
        out_shape=jax.ShapeDtypeStruct((T, D_MODEL), F32),
        scratch_shapes=[pltpu.VMEM((2, TOP_K, tm, HALF), jnp.uint32), pltpu.SemaphoreType.DMA((2,))],
        compiler_params=_cparams(("arbitrary",)),
        name="combine",
    )(dest, dest, h, gates, nw, yb)


def _row(v, width=None):
    v = v.astype(F32).reshape(1, -1)
    if width is not None and v.shape[1] < width:
        v = jnp.pad(v, ((0, 0), (0, width - v.shape[1])))
    return v


def _tile(n, pref):
    t = pref
    while n % t:
        t //= 2
    return t


def kernel(x, norm_mix_w, w_in, w_pool, pool_scale, conv_w, conv_b, dt_bias, a_log, d_skip,
           ssd_norm_w, w_ssd_out, w_out, norm_ffn_w, router_w, router_b, w_up, b_up,
           w_down, b_down, norm_final_w):
    B, S, D = x.shape
    T = B * S
    x2 = x.reshape(T, D)

    o_p, o_z, o_xbc = 0, D_MODEL, D_MODEL + SSD_INNER
    o_dt = o_xbc + SSD_INNER + 2 * SSD_BC
    o_gp = o_dt + SSD_HEADS
    o_gs = o_gp + D_MODEL
    w_t = w_in.T.astype(BF16)
    w_dt_t = jnp.pad(w_t[o_dt:o_dt + SSD_HEADS], ((0, LANES - SSD_HEADS), (0, 0)))
    tn = INPROJ_TN
    row_of_block = ([o_z + c for c in range(0, SSD_INNER, tn)]
                    + [o_xbc + c for c in range(0, SSD_INNER + 2 * SSD_BC, tn)]
                    + [o_p, o_gp, o_gs])
    proj, dt_raw = _inproj(x2, _row(norm_mix_w), w_t, w_dt_t, row_of_block, _tile(T, 1024), tn)

    pool_g = _pool(proj, w_pool.astype(BF16), _row(pool_scale), _tile(S, 256), S)

    yn = _ssd(proj, dt_raw,
              conv_w[:, :SSD_INNER].astype(F32), _row(conv_b[:SSD_INNER]),
              conv_w[:, SSD_INNER:].astype(F32), _row(conv_b[SSD_INNER:]),
              _row(dt_bias, LANES), _row(a_log, LANES),
              _row(jnp.repeat(d_skip, SSD_HEADDIM)), _row(ssd_norm_w), B, S)

    rw = jnp.pad(router_w.astype(F32), ((0, 0), (0, LANES - N_EXPERTS)))
    rw_hi = rw.astype(BF16)
    rw_lo = (rw - rw_hi.astype(F32)).astype(BF16)
    h, u_pk, er, gates, cnt = _mixout(pool_g, yn, proj, x2, w_ssd_out.astype(BF16), w_out.astype(BF16),
                                      _row(norm_ffn_w), jnp.concatenate([rw_hi, rw_lo], axis=1),
                                      _row(router_b, LANES), _tile(T, 256))

    TK = T * TOP_K
    tm_ffn = 2304 if TK >= 32768 else 256
    sub = 256
    n_tiles = -(-TK // tm_ffn) + N_EXPERTS
    counts = cnt[0, :N_EXPERTS].astype(jnp.int32)
    tiles_e = (counts + tm_ffn - 1) // tm_ffn
    tend = jnp.cumsum(tiles_e)
    tstart = tend - tiles_e
    pstart = (tstart * tm_ffn).astype(jnp.int32)
    tile_id = jnp.arange(n_tiles, dtype=jnp.int32)
    tile_expert = jnp.minimum(jnp.searchsorted(tend, tile_id, side='right'), N_EXPERTS - 1).astype(jnp.int32)
    tile_rows = jnp.clip(counts[tile_expert] - (tile_id - tstart[tile_expert]) * tm_ffn, 0, tm_ffn).astype(jnp.int32)
    n_used = tend[-1].astype(jnp.int32).reshape(1)
    e_flat = er[:, :TOP_K].reshape(TK)
    r_flat = er[:, TOP_K:2 * TOP_K].reshape(TK)

    xb, dest = _dispatch(pstart, e_flat, r_flat, u_pk, n_tiles * tm_ffn, _tile(T, 512))
    yb = _ffn(tile_expert, n_used, tile_rows, xb, w_up, b_up.reshape(N_EXPERTS, 1, 2 * D_FF),
              w_down, b_down.reshape(N_EXPERTS, 1, D_MODEL), tm_ffn, sub, 512, 512)
    out = _combine(dest, h, gates, _row(norm_final_w), yb, _tile(T, 256))
    return out.reshape(B, S, D)
```

```python
import functools

import jax
import jax.numpy as jnp
from jax import lax
from jax.experimental import pallas as pl
from jax.experimental.pallas import tpu as pltpu

F32 = jnp.float32
BF16 = jnp.bfloat16

D_MODEL = 2048
EPS = 1e-5
POOL_GROUPS = 4
POOL_WINDOWS = (2, 4, 8, 16)
POOL_GW = D_MODEL // POOL_GROUPS
SSD_INNER = 2 * D_MODEL
SSD_HEADDIM = 64
SSD_HEADS = SSD_INNER // SSD_HEADDIM
SSD_GROUPS = 8
SSD_HPG = SSD_HEADS // SSD_GROUPS
SSD_STATE = 128
SSD_CONV = 4
SSD_CHUNK = 128
SSD_GW = SSD_HPG * SSD_HEADDIM
SSD_BC = SSD_GROUPS * SSD_STATE
N_EXPERTS = 32
TOP_K = 4
D_FF = D_MODEL
SWIGLU_ALPHA = 1.702
SWIGLU_LIMIT = 7.0

LANES = 128
BF16_ROWS = 16
HALF = D_MODEL // 2
CONV_SLAB = 256
QUAD = 4
QUAD_W = QUAD * SSD_HEADDIM
INPROJ_TN = 2048
FFN_BIG_BLOCK = 1024
ROW_GROUP = 8
VMEM_LIMIT = 56 * 1024 * 1024

COL_Z = 0
COL_XS = SSD_INNER
COL_BC = 2 * SSD_INNER
COL_P = 2 * SSD_INNER + 2 * SSD_BC
COL_GP = COL_P + D_MODEL
COL_GS = COL_GP + D_MODEL
PROJ_COLS = COL_GS + D_MODEL

NEG_BIG = -1e30


def _cparams(sem):
    return pltpu.CompilerParams(dimension_semantics=sem, vmem_limit_bytes=VMEM_LIMIT)


_NT = (((1,), (1,)), ((), ()))


def _inproj_kernel(x_ref, nw_ref, w_ref, wdt_ref, o_ref, dt_ref, u_scr):
    @pl.when(pl.program_id(1) == 0)
    def _():
        x = x_ref[...]
        ms = jnp.mean(x * x, axis=-1, keepdims=True)
        u = (x * lax.rsqrt(ms + EPS) * nw_ref[...]).astype(BF16)
        u_scr[...] = u
        dt_ref[...] = lax.dot_general(u, wdt_ref[...], _NT, preferred_element_type=F32)

    o_ref[...] = lax.dot_general(u_scr[...], w_ref[...], _NT, preferred_element_type=F32).astype(o_ref.dtype)


def _inproj(x2, norm_w, w_t, w_dt_t, row_of_block, tm, tn):
    T = x2.shape[0]

    def w_rows(i, j):
        row = row_of_block[-1]
        for b in range(len(row_of_block) - 2, -1, -1):
            row = jnp.where(j == b, row_of_block[b], row)
        return pl.multiple_of(row, BF16_ROWS), 0

    return pl.pallas_call(
        _inproj_kernel,
        grid=(T // tm, PROJ_COLS // tn),
        in_specs=[
            pl.BlockSpec((tm, D_MODEL), lambda i, j: (i, 0)),
            pl.BlockSpec((1, D_MODEL), lambda i, j: (0, 0)),
            pl.BlockSpec((pl.Element(tn), pl.Element(D_MODEL)), w_rows),
            pl.BlockSpec((LANES, D_MODEL), lambda i, j: (0, 0)),
        ],
        out_specs=[
            pl.BlockSpec((tm, tn), lambda i, j: (i, j)),
            pl.BlockSpec((tm, LANES), lambda i, j: (i, 0)),
        ],
        out_shape=[
            jax.ShapeDtypeStruct((T, PROJ_COLS), BF16),
            jax.ShapeDtypeStruct((T, LANES), F32),
        ],
        scratch_shapes=[pltpu.VMEM((tm, D_MODEL), BF16)],
        compiler_params=_cparams(("parallel", "arbitrary")),
        name="inproj",
    )(x2, norm_w, w_t, w_dt_t)


def _pool_kernel(prev_ref, p_ref, g_ref, band_ref, hband_ref, wp_ref, ps_ref, o_ref, *, tm, seq):
    row0 = (pl.program_id(0) * tm) % seq
    pos = row0 + lax.broadcasted_iota(jnp.int32, (tm, 1), 0)
    for g, w in enumerate(POOL_WINDOWS):
        cols = slice(g * POOL_GW, (g + 1) * POOL_GW)
        cur = p_ref[:, cols]
        prev = prev_ref[:, cols]
        prev = jnp.where(row0 == 0, jnp.zeros_like(prev), prev)
        acc = (jnp.dot(band_ref[g], cur, preferred_element_type=F32)
               + jnp.dot(hband_ref[g], prev, preferred_element_type=F32))
        cnt = jnp.minimum(pos + 1, w).astype(F32)
        pooled = acc / cnt - cur.astype(F32)
        y = jnp.dot(pooled.astype(BF16), wp_ref[g], preferred_element_type=F32)
        y = y * ps_ref[:, cols] * jax.nn.sigmoid(g_ref[:, cols].astype(F32))
        o_ref[:, cols] = y.astype(o_ref.dtype)


def _pool(proj, w_pool, pool_scale, tm, seq):
    T = proj.shape[0]
    hist = max(POOL_WINDOWS)
    t_out = jnp.arange(tm, dtype=jnp.int32)[None, :, None]
    src = jnp.arange(hist + tm, dtype=jnp.int32)[None, None, :]
    win = jnp.asarray(POOL_WINDOWS, jnp.int32)[:, None, None]
    band = jnp.logical_and(src > hist + t_out - win, src <= hist + t_out).astype(BF16)
    return pl.pallas_call(
        functools.partial(_pool_kernel, tm=tm, seq=seq),
        grid=(T // tm,),
        in_specs=[
            pl.BlockSpec((hist, D_MODEL), lambda i: (jnp.maximum(i * (tm // hist) - 1, 0), COL_P // D_MODEL)),
            pl.BlockSpec((tm, D_MODEL), lambda i: (i, COL_P // D_MODEL)),
            pl.BlockSpec((tm, D_MODEL), lambda i: (i, COL_GP // D_MODEL)),
            pl.BlockSpec((POOL_GROUPS, tm, tm), lambda i: (0, 0, 0)),
            pl.BlockSpec((POOL_GROUPS, tm, hist), lambda i: (0, 0, 0)),
            pl.BlockSpec((POOL_GROUPS, POOL_GW, POOL_GW), lambda i: (0, 0, 0)),
            pl.BlockSpec((1, D_MODEL), lambda i: (0, 0)),
        ],
        out_specs=pl.BlockSpec((tm, D_MODEL), lambda i: (i, 0)),
        out_shape=jax.ShapeDtypeStruct((T, D_MODEL), BF16),
        compiler_params=_cparams(("parallel",)),
        name="pool",
    )(proj, proj, proj, band[:, :, hist:], band[:, :, :hist], w_pool, pool_scale)


def _split3(v):
    hi = v.astype(BF16)
    r1 = v - hi.astype(F32)
    mid = r1.astype(BF16)
    lo = (r1 - mid.astype(F32)).astype(BF16)
    return hi, mid, lo


def _ssd_kernel(z_ref, xs_ref, bc_ref, dt_ref, cwx_ref, cbx_ref, cwbc_ref, cbbc_ref,
                dtb_ref, alog_ref, dsk_ref, nw_ref, qm_ref, sh_ref, o_ref,
                prev_x, prev_bc, xs_scr, b_scr, c_scr, a_scr, qt_scr, wt_scr, state):
    L = SSD_CHUNK
    c = pl.program_id(1)

    @pl.when(c == 0)
    def _():
        prev_x[...] = jnp.zeros_like(prev_x)
        prev_bc[...] = jnp.zeros_like(prev_bc)
        state[...] = jnp.zeros_like(state)

    def conv_slab(src_ref, prev_ref, cw_ref, cb_ref, col):
        cols = slice(col, col + CONV_SLAB)
        raw = src_ref[:, cols]
        both = jnp.concatenate([prev_ref[:, cols], raw], axis=0)
        prev_ref[:, cols] = raw
        acc = cb_ref[:, cols] + cw_ref[SSD_CONV - 1:SSD_CONV, cols] * raw.astype(F32)
        for k in range(SSD_CONV - 1):
            shifted = jnp.dot(sh_ref[k * L:(k + 1) * L, :], both, preferred_element_type=F32)
            acc = acc + cw_ref[k:k + 1, cols] * shifted
        return acc * jax.nn.sigmoid(acc)

    for s in range(SSD_INNER // CONV_SLAB):
        xs_scr[:, s * CONV_SLAB:(s + 1) * CONV_SLAB] = conv_slab(xs_ref, prev_x, cwx_ref, cbx_ref, s * CONV_SLAB)
    for s in range(SSD_BC // CONV_SLAB):
        b_scr[:, s * CONV_SLAB:(s + 1) * CONV_SLAB] = conv_slab(bc_ref, prev_bc, cwbc_ref, cbbc_ref, s * CONV_SLAB)
    for s in range(SSD_BC // CONV_SLAB):
        c_scr[:, s * CONV_SLAB:(s + 1) * CONV_SLAB] = conv_slab(
            bc_ref, prev_bc, cwbc_ref, cbbc_ref, SSD_BC + s * CONV_SLAB).astype(BF16)

    dtv = jax.nn.softplus(dt_ref[...] + dtb_ref[...])
    d_a = dtv * (-jnp.exp(alog_ref[...]))
    rr = lax.broadcasted_iota(jnp.int32, (L, L), 0)
    cc = lax.broadcasted_iota(jnp.int32, (L, L), 1)
    causal = rr >= cc
    low_half = cc < SSD_HEADDIM
    tri = jnp.where(causal, 1.0, 0.0).astype(BF16)
    a_cum = sum(jnp.dot(tri, part, preferred_element_type=F32) for part in _split3(d_a))
    a_t = a_cum.T
    dt_t = dtv.T
    a_scr[...] = a_cum
    qt_scr[...] = a_t - jnp.log(dt_t)
    wt_scr[...] = dt_t * jnp.exp(a_t[:, L - 1:L] - a_t)

    def group_body(g):
        off = g * SSD_GW
        offn = g * SSD_STATE
        xg = xs_scr[:, pl.ds(off, SSD_GW)]
        xg_b = xg.astype(BF16)
        b_f = b_scr[:, pl.ds(offn, SSD_STATE)]
        c_b = c_scr[:, pl.ds(offn, SSD_STATE)]
        b_t = b_f.T
        cbm = lax.dot_general(c_b, b_f.astype(BF16), (((1,), (1,)), ((), ())),
                              preferred_element_type=F32)
        s_g = state[g]
        y_off = jnp.dot(c_b, s_g.astype(BF16), preferred_element_type=F32)
        a_all = a_scr[...]
        ys = []
        for quad in range(SSD_HPG // QUAD):
            qs = slice(quad * QUAD_W, (quad + 1) * QUAD_W)
            m_parts, b_parts, a_cols = [], [], []
            for jj in range(QUAD):
                j = quad * QUAD + jj
                h = g * SSD_HPG + j
                a_col = jnp.broadcast_to(a_all[:, h:h + 1], (L, L))
                seg = a_col - qt_scr[pl.ds(h, 1), :]
                m_parts.append((cbm * jnp.exp(jnp.where(causal, seg, NEG_BIG))).astype(BF16))
                b_parts.append((b_t * wt_scr[pl.ds(h, 1), :]).astype(BF16))
                a_cols.append(a_col)
            lhs = jnp.concatenate([jnp.concatenate(m_parts, axis=1), jnp.concatenate(b_parts, axis=1)], axis=0)
            xq = xg_b[:, qs]
            rhs = jnp.concatenate([xq * qm_ref[jj] for jj in range(QUAD)], axis=0)
            res = jnp.dot(lhs, rhs, preferred_element_type=F32)
            ea_q = jnp.concatenate(
                [jnp.exp(jnp.where(low_half, a_cols[2 * p], a_cols[2 * p + 1])) for p in range(QUAD // 2)], axis=1)
            ys.append(res[:L] + y_off[:, qs] * ea_q)
            state[g, :, qs] = s_g[:, qs] * ea_q[L - 1:L, :] + res[L:]
        y = jnp.concatenate(ys, axis=1) + dsk_ref[:, pl.ds(off, SSD_GW)] * xg
        zg = z_ref[:, pl.ds(off, SSD_GW)].astype(F32)
        yg = y * (zg * jax.nn.sigmoid(zg))
        ms = jnp.mean(yg * yg, axis=-1, keepdims=True)
        yn = yg * lax.rsqrt(ms + EPS) * nw_ref[:, pl.ds(off, SSD_GW)]
        o_ref[:, pl.ds(off, SSD_GW)] = yn.astype(o_ref.dtype)

    for g in range(SSD_GROUPS):
        group_body(g)


def _ssd(proj, dt_raw, cwx, cbx, cwbc, cbbc, dtb, alog, dsk, nw, batch, seq):
    L = SSD_CHUNK
    nc = seq // L
    T = proj.shape[0]
    row = lambda b, c: b * nc + c
    const = lambda b, c: (0, 0)
    head_of_lane = jnp.arange(QUAD_W, dtype=jnp.int32) // SSD_HEADDIM
    qmask = (head_of_lane[None, None, :] == jnp.arange(QUAD, dtype=jnp.int32)[:, None, None])
    qmask = jnp.broadcast_to(qmask, (QUAD, L, QUAD_W)).astype(BF16)
    tap = jnp.arange(SSD_CONV - 1, dtype=jnp.int32)[:, None, None]
    t_out = jnp.arange(L, dtype=jnp.int32)[None, :, None]
    src = jnp.arange(2 * L, dtype=jnp.int32)[None, None, :]
    shift = (src == L + t_out - (SSD_CONV - 1) + tap).astype(BF16).reshape((SSD_CONV - 1) * L, 2 * L)
    return pl.pallas_call(
        _ssd_kernel,
        grid=(batch, nc),
        in_specs=[
            pl.BlockSpec((L, SSD_INNER), lambda b, c: (row(b, c), COL_Z // SSD_INNER)),
            pl.BlockSpec((L, SSD_INNER), lambda b, c: (row(b, c), COL_XS // SSD_INNER)),
            pl.BlockSpec((L, 2 * SSD_BC), lambda b, c: (row(b, c), COL_BC // (2 * SSD_BC))),
            pl.BlockSpec((L, LANES), lambda b, c: (row(b, c), 0)),
            pl.BlockSpec((SSD_CONV, SSD_INNER), const),
            pl.BlockSpec((1, SSD_INNER), const),
            pl.BlockSpec((SSD_CONV, 2 * SSD_BC), const),
            pl.BlockSpec((1, 2 * SSD_BC), const),
            pl.BlockSpec((1, LANES), const),
            pl.BlockSpec((1, LANES), const),
            pl.BlockSpec((1, SSD_INNER), const),
            pl.BlockSpec((1, SSD_INNER), const),
            pl.BlockSpec((QUAD, L, QUAD_W), lambda b, c: (0, 0, 0)),
            pl.BlockSpec(((SSD_CONV - 1) * L, 2 * L), const),
        ],
        out_specs=pl.BlockSpec((L, SSD_INNER), lambda b, c: (row(b, c), 0)),
        out_shape=jax.ShapeDtypeStruct((T, SSD_INNER), BF16),
        scratch_shapes=[
            pltpu.VMEM((L, SSD_INNER), BF16),
            pltpu.VMEM((L, 2 * SSD_BC), BF16),
            pltpu.VMEM((L, SSD_INNER), F32),
            pltpu.VMEM((L, SSD_BC), F32),
            pltpu.VMEM((L, SSD_BC), BF16),
            pltpu.VMEM((L, LANES), F32),
            pltpu.VMEM((LANES, L), F32),
            pltpu.VMEM((LANES, L), F32),
            pltpu.VMEM((SSD_GROUPS, SSD_STATE, SSD_GW), F32),
        ],
        compiler_params=_cparams(("arbitrary", "arbitrary")),
        name="ssd",
    )(proj, proj, proj, dt_raw, cwx, cbx, cwbc, cbbc, dtb, alog, dsk, nw, qmask, shift)


def _pack_bf16_pairs(v):
    bits = pltpu.bitcast(v.astype(BF16).astype(F32), jnp.uint32)
    return (bits[:, :HALF] >> 16) | (bits[:, HALF:] & jnp.uint32(0xFFFF0000))


def _unpack_lo(w):
    return pltpu.bitcast(w << 16, F32)


def _unpack_hi(w):
    return pltpu.bitcast(w & jnp.uint32(0xFFFF0000), F32)


def _mixout_kernel(pg_ref, yn_ref, gs_ref, x_ref, wso_ref, wo_ref, nw_ref, rw_ref, rb_ref,
                   h_ref, u_ref, er_ref, gt_ref, cnt_ref, run_scr, lg_scr, *, tm):
    i = pl.program_id(0)
    last = pl.num_programs(0) - 1

    @pl.when(i == 0)
    def _():
        run_scr[...] = jnp.zeros_like(run_scr)
        lg_scr[...] = jnp.zeros_like(lg_scr)

    def route_previous_tile(lg_prev):
        er, gt = _route(lg_prev, run_scr, tm, i > 0)
        er_ref[...] = er
        gt_ref[...] = gt
        cnt_ref[...] = run_scr[...]

    @pl.when(i < last)
    def _():
        lg_prev = lg_scr[...]
        ssd = jnp.dot(yn_ref[...], wso_ref[...], preferred_element_type=F32)
        merged = pg_ref[...].astype(F32) + jax.nn.sigmoid(gs_ref[...].astype(F32)) * ssd
        h = x_ref[...] + jnp.dot(merged.astype(BF16), wo_ref[...], preferred_element_type=F32)
        h_ref[...] = h
        ms = jnp.mean(h * h, axis=-1, keepdims=True)
        u = h * lax.rsqrt(ms + EPS) * nw_ref[...]
        u_hi = u.astype(BF16)
        u_ref[...] = _pack_bf16_pairs(u)
        u_lo = (u - u_hi.astype(F32)).astype(BF16)
        both = jnp.dot(u_hi, rw_ref[...], preferred_element_type=F32)
        lg = both[:, :LANES] + both[:, LANES:] + jnp.dot(u_lo, rw_ref[:, :LANES], preferred_element_type=F32)
        lg_scr[...] = lg + rb_ref[...]
        route_previous_tile(lg_prev)

    @pl.when(i == last)
    def _():
        route_previous_tile(lg_scr[...])


def _mixout(pool_g, yn, proj, x2, wso, wo, nw, rw, rb, tm):
    T = x2.shape[0]
    n = T // tm
    const = lambda i: (0, 0)
    cur = lambda i: jnp.minimum(i, n - 1)
    prev = lambda i: jnp.maximum(i - 1, 0)
    return pl.pallas_call(
        functools.partial(_mixout_kernel, tm=tm),
        grid=(n + 1,),
        in_specs=[
            pl.BlockSpec((tm, D_MODEL), lambda i: (cur(i), 0)),
            pl.BlockSpec((tm, SSD_INNER), lambda i: (cur(i), 0)),
            pl.BlockSpec((tm, D_MODEL), lambda i: (cur(i), COL_GS // D_MODEL)),
            pl.BlockSpec((tm, D_MODEL), lambda i: (cur(i), 0)),
            pl.BlockSpec((SSD_INNER, D_MODEL), const),
            pl.BlockSpec((D_MODEL, D_MODEL), const),
            pl.BlockSpec((1, D_MODEL), const),
            pl.BlockSpec((D_MODEL, 2 * LANES), const),
            pl.BlockSpec((1, LANES), const),
        ],
        out_specs=[
            pl.BlockSpec((tm, D_MODEL), lambda i: (cur(i), 0)),
            pl.BlockSpec((tm, HALF), lambda i: (cur(i), 0)),
            pl.BlockSpec((tm, LANES), lambda i: (prev(i), 0)),
            pl.BlockSpec((tm, LANES), lambda i: (prev(i), 0)),
            pl.BlockSpec((1, LANES), const),
        ],
        out_shape=[
            jax.ShapeDtypeStruct((T, D_MODEL), F32),
            jax.ShapeDtypeStruct((T, HALF), jnp.uint32),
            jax.ShapeDtypeStruct((T, LANES), jnp.int32),
            jax.ShapeDtypeStruct((T, LANES), F32),
            jax.ShapeDtypeStruct((1, LANES), F32),
        ],
        scratch_shapes=[pltpu.VMEM((1, LANES), F32), pltpu.VMEM((tm, LANES), F32)],
        compiler_params=_cparams(("arbitrary",)),
        name="mixout",
    )(pool_g, yn, proj, x2, wso, wo, nw, rw, rb)


def _route(lg, run_scr, tm, valid):
    lane = lax.broadcasted_iota(jnp.int32, (tm, LANES), 1)
    lg = jnp.where(lane < N_EXPERTS, lg, -jnp.inf)
    idxs, vals, hots = [], [], []
    for _ in range(TOP_K):
        m = jnp.max(lg, axis=-1, keepdims=True)
        idx = jnp.min(jnp.where(lg == m, lane, LANES), axis=-1, keepdims=True)
        hot = lane == idx
        lg = jnp.where(hot, -jnp.inf, lg)
        idxs.append(idx)
        vals.append(m)
        hots.append(hot)
    exps = [jnp.exp(v - vals[0]) for v in vals]
    den = exps[0] + exps[1] + exps[2] + exps[3]
    cnt = sum(jnp.where(hot, 1.0, 0.0) for hot in hots)
    rr = lax.broadcasted_iota(jnp.int32, (tm, tm), 0)
    cc = lax.broadcasted_iota(jnp.int32, (tm, tm), 1)
    before = jnp.where(rr > cc, 1.0, 0.0).astype(BF16)
    base = jnp.dot(before, cnt.astype(BF16), preferred_element_type=F32) + run_scr[...]
    er = jnp.zeros((tm, LANES), jnp.int32)
    gt = jnp.zeros((tm, LANES), F32)
    for k in range(TOP_K):
        rank = jnp.sum(jnp.where(hots[k], base, 0.0), axis=-1, keepdims=True)
        er = jnp.where(lane == k, idxs[k], er)
        er = jnp.where(lane == TOP_K + k, rank.astype(jnp.int32), er)
        gt = jnp.where(lane == k, exps[k] / den, gt)
    run_scr[...] += jnp.where(valid, jnp.sum(cnt, axis=0, keepdims=True), 0.0)
    return er, gt


def _dispatch_kernel(ps_ref, e_ref, r_ref, u_ref, xb_ref, dest_ref, sem, *, tm):
    def body(g, carry):
        r0 = pl.multiple_of(g * ROW_GROUP, ROW_GROUP)
        for i in range(ROW_GROUP):
            for k in range(TOP_K):
                j = (r0 + i) * TOP_K + k
                d = ps_ref[e_ref[j]] + r_ref[j]
                dest_ref[j] = d
                pltpu.make_async_copy(u_ref.at[pl.ds(r0 + i, 1)], xb_ref.at[pl.ds(d, 1)], sem).start()
        return carry

    lax.fori_loop(0, tm // ROW_GROUP, body, 0)
    for _ in range(TOP_K):
        pltpu.make_async_copy(u_ref, xb_ref.at[pl.ds(0, tm)], sem).wait()


def _dispatch(pstart, e_flat, r_flat, u_pk, n_slots, tm):
    T = u_pk.shape[0]
    grid_spec = pltpu.PrefetchScalarGridSpec(
        num_scalar_prefetch=1,
        grid=(T // tm,),
        in_specs=[
            pl.BlockSpec((tm * TOP_K,), lambda i, ps: (i,), memory_space=pltpu.SMEM),
            pl.BlockSpec((tm * TOP_K,), lambda i, ps: (i,), memory_space=pltpu.SMEM),
            pl.BlockSpec((tm, HALF), lambda i, ps: (i, 0)),
        ],
        out_specs=[
            pl.BlockSpec(memory_space=pl.ANY),
            pl.BlockSpec((tm * TOP_K,), lambda i, ps: (i,), memory_space=pltpu.SMEM),
        ],
        scratch_shapes=[pltpu.SemaphoreType.DMA(())],
    )
    return pl.pallas_call(
        functools.partial(_dispatch_kernel, tm=tm),
        grid_spec=grid_spec,
        out_shape=[
            jax.ShapeDtypeStruct((n_slots, HALF), jnp.uint32),
            jax.ShapeDtypeStruct((T * TOP_K,), jnp.int32),
        ],
        compiler_params=_cparams(("arbitrary",)),
        name="dispatch",
    )(pstart, e_flat, r_flat, u_pk)


def _row_blocks(nrows, tm, sub, weights, block):
    nblk = (nrows + sub - 1) // sub
    per_big = min(FFN_BIG_BLOCK, tm) // sub
    nbig = tm // (per_big * sub)
    for b in range(nbig):
        @pl.when(nblk >= (b + 1) * per_big)
        def _():
            block(b * per_big * sub, per_big * sub, weights())

    def rest(sb, carry):
        block(pl.multiple_of(sb * sub, sub), sub, weights())
        return carry

    lax.fori_loop(jnp.minimum(nblk // per_big, nbig) * per_big, nblk, rest, 0)


def _ffn_up_kernel(te_ref, nu_ref, tr_ref, x_ref, wg_ref, wl_ref, bg_ref, bl_ref, o_ref, *, tm, sub):
    nrows = tr_ref[pl.program_id(0)]

    def weights():
        return wg_ref[0].astype(BF16), wl_ref[0].astype(BF16)

    def block(r0, nr, w):
        packed = x_ref[pl.ds(r0, nr), :]
        x = jnp.concatenate([_unpack_lo(packed).astype(BF16), _unpack_hi(packed).astype(BF16)], axis=1)
        glu = jnp.minimum(jnp.dot(x, w[0], preferred_element_type=F32) + bg_ref[0], SWIGLU_LIMIT)
        lin = jnp.clip(jnp.dot(x, w[1], preferred_element_type=F32) + bl_ref[0], -SWIGLU_LIMIT, SWIGLU_LIMIT)
        act = glu * jax.nn.sigmoid(SWIGLU_ALPHA * glu) * (lin + 1.0)
        o_ref[pl.ds(r0, nr), :] = act.astype(o_ref.dtype)

    _row_blocks(nrows, tm, sub, weights, block)


def _ffn_down_kernel(te_ref, nu_ref, tr_ref, a_ref, wa_ref, wb_ref, ba_ref, bb_ref, o_ref, *, tm, sub):
    nrows = tr_ref[pl.program_id(0)]

    def weights():
        return wa_ref[0].astype(BF16), wb_ref[0].astype(BF16)

    def block(r0, nr, w):
        a = a_ref[pl.ds(r0, nr), :]
        ya = jnp.dot(a, w[0], preferred_element_type=F32) + ba_ref[0]
        yb = jnp.dot(a, w[1], preferred_element_type=F32) + bb_ref[0]
        lo = pltpu.bitcast(ya.astype(BF16).astype(F32), jnp.uint32) >> 16
        hi = pltpu.bitcast(yb.astype(BF16).astype(F32), jnp.uint32) & jnp.uint32(0xFFFF0000)
        o_ref[pl.ds(r0, nr), :] = lo | hi

    _row_blocks(nrows, tm, sub, weights, block)


def _ffn(tile_expert, n_used, tile_rows, xb, w_up, b_up3, w_down, b_down3, tm, sub, fc, nc):
    n_slots = xb.shape[0]
    n_tiles = n_slots // tm
    nf = D_FF // fc
    nj = HALF // nc

    def live(i, nu):
        return jnp.minimum(i, nu[0] - 1)

    def frozen(i, j, nu, last):
        return jnp.where(i < nu[0], j, last)

    up_spec = pltpu.PrefetchScalarGridSpec(
        num_scalar_prefetch=3,
        grid=(n_tiles, nf),
        in_specs=[
            pl.BlockSpec((tm, HALF), lambda i, f, te, nu, tr: (live(i, nu), 0)),
            pl.BlockSpec((1, D_MODEL, fc), lambda i, f, te, nu, tr: (te[i], 0, frozen(i, f, nu, nf - 1))),
            pl.BlockSpec((1, D_MODEL, fc), lambda i, f, te, nu, tr: (te[i], 0, nf + frozen(i, f, nu, nf - 1))),
            pl.BlockSpec((1, 1, fc), lambda i, f, te, nu, tr: (te[i], 0, frozen(i, f, nu, nf - 1))),
            pl.BlockSpec((1, 1, fc), lambda i, f, te, nu, tr: (te[i], 0, nf + frozen(i, f, nu, nf - 1))),
        ],
        out_specs=pl.BlockSpec((tm, fc), lambda i, f, te, nu, tr: (live(i, nu), frozen(i, f, nu, nf - 1))),
    )
    act = pl.pallas_call(
        functools.partial(_ffn_up_kernel, tm=tm, sub=sub),
        grid_spec=up_spec,
        out_shape=jax.ShapeDtypeStruct((n_slots, D_FF), BF16),
        compiler_params=_cparams(("arbitrary", "arbitrary")),
        name="ffn_up",
    )(tile_expert, n_used, tile_rows, xb, w_up, w_up, b_up3, b_up3)

    down_spec = pltpu.PrefetchScalarGridSpec(
        num_scalar_prefetch=3,
        grid=(n_tiles, nj),
        in_specs=[
            pl.BlockSpec((tm, D_FF), lambda i, j, te, nu, tr: (live(i, nu), 0)),
            pl.BlockSpec((1, D_FF, nc), lambda i, j, te, nu, tr: (te[i], 0, frozen(i, j, nu, nj - 1))),
            pl.BlockSpec((1, D_FF, nc), lambda i, j, te, nu, tr: (te[i], 0, nj + frozen(i, j, nu, nj - 1))),
            pl.BlockSpec((1, 1, nc), lambda i, j, te, nu, tr: (te[i], 0, frozen(i, j, nu, nj - 1))),
            pl.BlockSpec((1, 1, nc), lambda i, j, te, nu, tr: (te[i], 0, nj + frozen(i, j, nu, nj - 1))),
        ],
        out_specs=pl.BlockSpec((tm, nc), lambda i, j, te, nu, tr: (live(i, nu), frozen(i, j, nu, nj - 1))),
    )
    return pl.pallas_call(
        functools.partial(_ffn_down_kernel, tm=tm, sub=sub),
        grid_spec=down_spec,
        out_shape=jax.ShapeDtypeStruct((n_slots, HALF), jnp.uint32),
        compiler_params=_cparams(("arbitrary", "arbitrary")),
        name="ffn_down",
    )(tile_expert, n_used, tile_rows, act, w_down, w_down, b_down3, b_down3)


def _combine_kernel(d_ref, dn_ref, h_ref, g_ref, nw_ref, yb_ref, o_ref, ybuf, sem, *, tm):
    step = pl.program_id(0)
    slot = step % 2

    def gather_rows(idx_ref, s):
        def body(g, carry):
            r0 = pl.multiple_of(g * ROW_GROUP, ROW_GROUP)
            for i in range(ROW_GROUP):
                for k in range(TOP_K):
                    d = idx_ref[(r0 + i) * TOP_K + k]
                    pltpu.make_async_copy(yb_ref.at[pl.ds(d, 1)], ybuf.at[s, k, pl.ds(r0 + i, 1)],
                                          sem.at[s]).start()
            return carry

        lax.fori_loop(0, tm // ROW_GROUP, body, 0)

    @pl.when(step == 0)
    def _():
        gather_rows(d_ref, 0)

    @pl.when(step + 1 < pl.num_programs(0))
    def _():
        gather_rows(dn_ref, 1 - slot)

    for k in range(TOP_K):
        pltpu.make_async_copy(yb_ref.at[pl.ds(0, tm)], ybuf.at[slot, k], sem.at[slot]).wait()
    lo = h_ref[:, :HALF]
    hi = h_ref[:, HALF:]
    for k in range(TOP_K):
        w = ybuf[slot, k]
        g = g_ref[:, k:k + 1]
        lo = lo + g * _unpack_lo(w)
        hi = hi + g * _unpack_hi(w)
    ms = (jnp.sum(lo * lo, axis=-1, keepdims=True) + jnp.sum(hi * hi, axis=-1, keepdims=True)) / D_MODEL
    scale = lax.rsqrt(ms + EPS)
    o_ref[:, :HALF] = lo * scale * nw_ref[:, :HALF]
    o_ref[:, HALF:] = hi * scale * nw_ref[:, HALF:]


def _combine(dest, h, gates, nw, yb, tm):
    T = h.shape[0]
    n = T // tm
    return pl.pallas_call(
        functools.partial(_combine_kernel, tm=tm),
        grid=(n,),
        in_specs=[
            pl.BlockSpec((tm * TOP_K,), lambda i: (i,), memory_space=pltpu.SMEM),
            pl.BlockSpec((tm * TOP_K,), lambda i: (jnp.minimum(i + 1, n - 1),), memory_space=pltpu.SMEM),
            pl.BlockSpec((tm, D_MODEL), lambda i: (i, 0)),
            pl.BlockSpec((tm, LANES), lambda i: (i, 0)),
            pl.BlockSpec((1, D_MODEL), lambda i: (0, 0)),
            pl.BlockSpec(memory_space=pl.ANY),
        ],
        out_specs=pl.BlockSpec((tm, D_MODEL), lambda i: (i, 0)),
        out_shape=jax.ShapeDtypeStruct((T, D_MODEL), F32),
        scratch_shapes=[pltpu.VMEM((2, TOP_K, tm, HALF), jnp.uint32), pltpu.SemaphoreType.DMA((2,))],
        compiler_params=_cparams(("arbitrary",)),
        name="combine",
    )(dest, dest, h, gates, nw, yb)


def _row(v, width=None):
    v = v.astype(F32).reshape(1, -1)
    if width is not None and v.shape[1] < width:
        v = jnp.pad(v, ((0, 0), (0, width - v.shape[1])))
    return v


def _tile(n, pref):
    t = pref
    while n % t:
        t //= 2
    return t


def kernel(x, norm_mix_w, w_in, w_pool, pool_scale, conv_w, conv_b, dt_bias, a_log, d_skip,
           ssd_norm_w, w_ssd_out, w_out, norm_ffn_w, router_w, router_b, w_up, b_up,
           w_down, b_down, norm_final_w):
    B, S, D = x.shape
    T = B * S
    x2 = x.reshape(T, D)

    o_p, o_z, o_xbc = 0, D_MODEL, D_MODEL + SSD_INNER
    o_dt = o_xbc + SSD_INNER + 2 * SSD_BC
    o_gp = o_dt + SSD_HEADS
    o_gs = o_gp + D_MODEL
    w_t = w_in.T.astype(BF16)
    w_dt_t = jnp.pad(w_t[o_dt:o_dt + SSD_HEADS], ((0, LANES - SSD_HEADS), (0, 0)))
    tn = INPROJ_TN
    row_of_block = ([o_z + c for c in range(0, SSD_INNER, tn)]
                    + [o_xbc + c for c in range(0, SSD_INNER + 2 * SSD_BC, tn)]
                    + [o_p, o_gp, o_gs])
    proj, dt_raw = _inproj(x2, _row(norm_mix_w), w_t, w_dt_t, row_of_block, _tile(T, 1024), tn)

    pool_g = _pool(proj, w_pool.astype(BF16), _row(pool_scale), _tile(S, 256), S)

    yn = _ssd(proj, dt_raw,
              conv_w[:, :SSD_INNER].astype(F32), _row(conv_b[:SSD_INNER]),
              conv_w[:, SSD_INNER:].astype(F32), _row(conv_b[SSD_INNER:]),
              _row(dt_bias, LANES), _row(a_log, LANES),
              _row(jnp.repeat(d_skip, SSD_HEADDIM)), _row(ssd_norm_w), B, S)

    rw = jnp.pad(router_w.astype(F32), ((0, 0), (0, LANES - N_EXPERTS)))
    rw_hi = rw.astype(BF16)
    rw_lo = (rw - rw_hi.astype(F32)).astype(BF16)
    h, u_pk, er, gates, cnt = _mixout(pool_g, yn, proj, x2, w_ssd_out.astype(BF16), w_out.astype(BF16),
                                      _row(norm_ffn_w), jnp.concatenate([rw_hi, rw_lo], axis=1),
                                      _row(router_b, LANES), _tile(T, 256))

    TK = T * TOP_K
    tm_ffn = 2304 if TK >= 32768 else 256
    sub = 256
    n_tiles = -(-TK // tm_ffn) + N_EXPERTS
    counts = cnt[0, :N_EXPERTS].astype(jnp.int32)
    tiles_e = (counts + tm_ffn - 1) // tm_ffn
    tend = jnp.cumsum(tiles_e)
    tstart = tend - tiles_e
    pstart = (tstart * tm_ffn).astype(jnp.int32)
    tile_id = jnp.arange(n_tiles, dtype=jnp.int32)
    tile_expert = jnp.minimum(jnp.searchsorted(tend, tile_id, side='right'), N_EXPERTS - 1).astype(jnp.int32)
    tile_rows = jnp.clip(counts[tile_expert] - (tile_id - tstart[tile_expert]) * tm_ffn, 0, tm_ffn).astype(jnp.int32)
    n_used = tend[-1].astype(jnp.int32).reshape(1)
    e_flat = er[:, :TOP_K].reshape(TK)
    r_flat = er[:, TOP_K:2 * TOP_K].reshape(TK)

    xb, dest = _dispatch(pstart, e_flat, r_flat, u_pk, n_tiles * tm_ffn, _tile(T, 512))
    yb = _ffn(tile_expert, n_used, tile_rows, xb, w_up, b_up.reshape(N_EXPERTS, 1, 2 * D_FF),
              w_down, b_down.reshape(N_EXPERTS, 1, D_MODEL), tm_ffn, sub, 512, 512)
    out = _combine(dest, h, gates, _row(norm_final_w), yb, _tile(T, 256))
    return out.reshape(B, S, D)
```

```python
import functools

import jax
import jax.numpy as jnp
from jax import lax
from jax.experimental import pallas as pl
from jax.experimental.pallas import tpu as pltpu

F32 = jnp.float32
BF16 = jnp.bfloat16

D_MODEL = 2048
EPS = 1e-5
POOL_GROUPS = 4
POOL_WINDOWS = (2, 4, 8, 16)
POOL_GW = D_MODEL // POOL_GROUPS
SSD_INNER = 2 * D_MODEL
SSD_HEADDIM = 64
SSD_HEADS = SSD_INNER // SSD_HEADDIM
SSD_GROUPS = 8
SSD_HPG = SSD_HEADS // SSD_GROUPS
SSD_STATE = 128
SSD_CONV = 4
SSD_CHUNK = 128
SSD_GW = SSD_HPG * SSD_HEADDIM
SSD_BC = SSD_GROUPS * SSD_STATE
N_EXPERTS = 32
TOP_K = 4
D_FF = D_MODEL
SWIGLU_ALPHA = 1.702
SWIGLU_LIMIT = 7.0

LANES = 128
BF16_ROWS = 16
HALF = D_MODEL // 2
CONV_SLAB = 256
QUAD = 4
QUAD_W = QUAD * SSD_HEADDIM
SSD_SUBS = 2
INPROJ_TN = 2048
FFN_BIG_BLOCK = 1024
ROW_GROUP = 8
VMEM_LIMIT = 56 * 1024 * 1024

COL_Z = 0
COL_XS = SSD_INNER
COL_BC = 2 * SSD_INNER
COL_P = 2 * SSD_INNER + 2 * SSD_BC
COL_GP = COL_P + D_MODEL
COL_GS = COL_GP + D_MODEL
PROJ_COLS = COL_GS + D_MODEL

NEG_BIG = -1e30


def _cparams(sem):
    return pltpu.CompilerParams(dimension_semantics=sem, vmem_limit_bytes=VMEM_LIMIT)


_NT = (((1,), (1,)), ((), ()))


def _inproj_kernel(x_ref, nw_ref, w_ref, wdt_ref, o_ref, dt_ref, u_scr):
    @pl.when(pl.program_id(1) == 0)
    def _():
        x = x_ref[...]
        ms = jnp.mean(x * x, axis=-1, keepdims=True)
        u = (x * lax.rsqrt(ms + EPS) * nw_ref[...]).astype(BF16)
        u_scr[...] = u
        dt_ref[...] = lax.dot_general(u, wdt_ref[...], _NT, preferred_element_type=F32)

    o_ref[...] = lax.dot_general(u_scr[...], w_ref[...], _NT, preferred_element_type=F32).astype(o_ref.dtype)


def _inproj(x2, norm_w, w_t, w_dt_t, row_of_block, tm, tn):
    T = x2.shape[0]

    def w_rows(i, j):
        row = row_of_block[-1]
        for b in range(len(row_of_block) - 2, -1, -1):
            row = jnp.where(j == b, row_of_block[b], row)
        return pl.multiple_of(row, BF16_ROWS), 0

    return pl.pallas_call(
        _inproj_kernel,
        grid=(T // tm, PROJ_COLS // tn),
        in_specs=[
            pl.BlockSpec((tm, D_MODEL), lambda i, j: (i, 0)),
            pl.BlockSpec((1, D_MODEL), lambda i, j: (0, 0)),
            pl.BlockSpec((pl.Element(tn), pl.Element(D_MODEL)), w_rows),
            pl.BlockSpec((LANES, D_MODEL), lambda i, j: (0, 0)),
        ],
        out_specs=[
            pl.BlockSpec((tm, tn), lambda i, j: (i, j)),
            pl.BlockSpec((tm, LANES), lambda i, j: (i, 0)),
        ],
        out_shape=[
            jax.ShapeDtypeStruct((T, PROJ_COLS), BF16),
            jax.ShapeDtypeStruct((T, LANES), F32),
        ],
        scratch_shapes=[pltpu.VMEM((tm, D_MODEL), BF16)],
        compiler_params=_cparams(("parallel", "arbitrary")),
        name="inproj",
    )(x2, norm_w, w_t, w_dt_t)


def _pool_kernel(prev_ref, p_ref, g_ref, band_ref, hband_ref, wp_ref, ps_ref, o_ref, *, tm, seq):
    row0 = (pl.program_id(0) * tm) % seq
    pos = row0 + lax.broadcasted_iota(jnp.int32, (tm, 1), 0)
    for g, w in enumerate(POOL_WINDOWS):
        cols = slice(g * POOL_GW, (g + 1) * POOL_GW)
        cur = p_ref[:, cols]
        prev = prev_ref[:, cols]
        prev = jnp.where(row0 == 0, jnp.zeros_like(prev), prev)
        acc = (jnp.dot(band_ref[g], cur, preferred_element_type=F32)
               + jnp.dot(hband_ref[g], prev, preferred_element_type=F32))
        cnt = jnp.minimum(pos + 1, w).astype(F32)
        pooled = acc / cnt - cur.astype(F32)
        y = jnp.dot(pooled.astype(BF16), wp_ref[g], preferred_element_type=F32)
        y = y * ps_ref[:, cols] * jax.nn.sigmoid(g_ref[:, cols].astype(F32))
        o_ref[:, cols] = y.astype(o_ref.dtype)


def _pool(proj, w_pool, pool_scale, tm, seq):
    T = proj.shape[0]
    hist = max(POOL_WINDOWS)
    t_out = jnp.arange(tm, dtype=jnp.int32)[None, :, None]
    src = jnp.arange(hist + tm, dtype=jnp.int32)[None, None, :]
    win = jnp.asarray(POOL_WINDOWS, jnp.int32)[:, None, None]
    band = jnp.logical_and(src > hist + t_out - win, src <= hist + t_out).astype(BF16)
    return pl.pallas_call(
        functools.partial(_pool_kernel, tm=tm, seq=seq),
        grid=(T // tm,),
        in_specs=[
            pl.BlockSpec((hist, D_MODEL), lambda i: (jnp.maximum(i * (tm // hist) - 1, 0), COL_P // D_MODEL)),
            pl.BlockSpec((tm, D_MODEL), lambda i: (i, COL_P // D_MODEL)),
            pl.BlockSpec((tm, D_MODEL), lambda i: (i, COL_GP // D_MODEL)),
            pl.BlockSpec((POOL_GROUPS, tm, tm), lambda i: (0, 0, 0)),
            pl.BlockSpec((POOL_GROUPS, tm, hist), lambda i: (0, 0, 0)),
            pl.BlockSpec((POOL_GROUPS, POOL_GW, POOL_GW), lambda i: (0, 0, 0)),
            pl.BlockSpec((1, D_MODEL), lambda i: (0, 0)),
        ],
        out_specs=pl.BlockSpec((tm, D_MODEL), lambda i: (i, 0)),
        out_shape=jax.ShapeDtypeStruct((T, D_MODEL), BF16),
        compiler_params=_cparams(("parallel",)),
        name="pool",
    )(proj, proj, proj, band[:, :, hist:], band[:, :, :hist], w_pool, pool_scale)


def _split3(v):
    hi = v.astype(BF16)
    r1 = v - hi.astype(F32)
    mid = r1.astype(BF16)
    lo = (r1 - mid.astype(F32)).astype(BF16)
    return hi, mid, lo


def _ssd_kernel(z_ref, xs_ref, bc_ref, dt_ref, cwx_ref, cbx_ref, cwbc_ref, cbbc_ref,
                dtb_ref, alog_ref, dsk_ref, nw_ref, qm_ref, sh_ref, o_ref,
                prev_x, prev_bc, xs_scr, b_scr, c_scr, a_scr, qt_scr, wt_scr, state):
    L = SSD_CHUNK

    @pl.when(pl.program_id(1) == 0)
    def _():
        prev_x[...] = jnp.zeros_like(prev_x)
        prev_bc[...] = jnp.zeros_like(prev_bc)
        state[...] = jnp.zeros_like(state)

    rr = lax.broadcasted_iota(jnp.int32, (L, L), 0)
    cc = lax.broadcasted_iota(jnp.int32, (L, L), 1)
    causal = rr >= cc
    low_half = cc < SSD_HEADDIM
    tri = jnp.where(causal, 1.0, 0.0).astype(BF16)

    def conv_slab(sub, src_ref, prev_ref, cw_ref, cb_ref, col):
        cols = slice(col, col + CONV_SLAB)
        raw = src_ref[sub * L:(sub + 1) * L, cols]
        before = prev_ref[:, cols] if sub == 0 else src_ref[(sub - 1) * L:sub * L, cols]
        both = jnp.concatenate([before, raw], axis=0)
        acc = cb_ref[:, cols] + cw_ref[SSD_CONV - 1:SSD_CONV, cols] * raw.astype(F32)
        for k in range(SSD_CONV - 1):
            shifted = jnp.dot(sh_ref[k * L:(k + 1) * L, :], both, preferred_element_type=F32)
            acc = acc + cw_ref[k:k + 1, cols] * shifted
        return acc * jax.nn.sigmoid(acc)

    def chunk(sub):
        rows = slice(sub * L, (sub + 1) * L)
        for s in range(SSD_INNER // CONV_SLAB):
            xs_scr[sub, :, s * CONV_SLAB:(s + 1) * CONV_SLAB] = conv_slab(
                sub, xs_ref, prev_x, cwx_ref, cbx_ref, s * CONV_SLAB)
        for s in range(SSD_BC // CONV_SLAB):
            b_scr[sub, :, s * CONV_SLAB:(s + 1) * CONV_SLAB] = conv_slab(
                sub, bc_ref, prev_bc, cwbc_ref, cbbc_ref, s * CONV_SLAB)
        for s in range(SSD_BC // CONV_SLAB):
            c_scr[sub, :, s * CONV_SLAB:(s + 1) * CONV_SLAB] = conv_slab(
                sub, bc_ref, prev_bc, cwbc_ref, cbbc_ref, SSD_BC + s * CONV_SLAB).astype(BF16)

        dtv = jax.nn.softplus(dt_ref[rows, :] + dtb_ref[...])
        d_a = dtv * (-jnp.exp(alog_ref[...]))
        a_cum = sum(jnp.dot(tri, part, preferred_element_type=F32) for part in _split3(d_a))
        a_t = a_cum.T
        dt_t = dtv.T
        a_scr[sub] = a_cum
        qt_scr[sub] = a_t - jnp.log(dt_t)
        wt_scr[sub] = dt_t * jnp.exp(a_t[:, L - 1:L] - a_t)

        def group_body(g):
            off = g * SSD_GW
            offn = g * SSD_STATE
            xg = xs_scr[sub, :, pl.ds(off, SSD_GW)]
            xg_b = xg.astype(BF16)
            b_f = b_scr[sub, :, pl.ds(offn, SSD_STATE)]
            c_b = c_scr[sub, :, pl.ds(offn, SSD_STATE)]
            b_t = b_f.T
            cbm = lax.dot_general(c_b, b_f.astype(BF16), (((1,), (1,)), ((), ())),
                                  preferred_element_type=F32)
            s_g = state[g]
            y_off = jnp.dot(c_b, s_g.astype(BF16), preferred_element_type=F32)
            a_all = a_scr[sub]
            ys = []
            for quad in range(SSD_HPG // QUAD):
                qs = slice(quad * QUAD_W, (quad + 1) * QUAD_W)
                m_parts, b_parts, a_cols = [], [], []
                for jj in range(QUAD):
                    j = quad * QUAD + jj
                    h = g * SSD_HPG + j
                    a_col = jnp.broadcast_to(a_all[:, h:h + 1], (L, L))
                    seg = a_col - qt_scr[sub, pl.ds(h, 1), :]
                    m_parts.append((cbm * jnp.exp(jnp.where(causal, seg, NEG_BIG))).astype(BF16))
                    b_parts.append((b_t * wt_scr[sub, pl.ds(h, 1), :]).astype(BF16))
                    a_cols.append(a_col)
                lhs = jnp.concatenate([jnp.concatenate(m_parts, axis=1), jnp.concatenate(b_parts, axis=1)], axis=0)
                xq = xg_b[:, qs]
                rhs = jnp.concatenate([xq * qm_ref[jj] for jj in range(QUAD)], axis=0)
                res = jnp.dot(lhs, rhs, preferred_element_type=F32)
                ea_q = jnp.concatenate(
                    [jnp.exp(jnp.where(low_half, a_cols[2 * p], a_cols[2 * p + 1])) for p in range(QUAD // 2)],
                    axis=1)
                ys.append(res[:L] + y_off[:, qs] * ea_q)
                state[g, :, qs] = s_g[:, qs] * ea_q[L - 1:L, :] + res[L:]
            y = jnp.concatenate(ys, axis=1) + dsk_ref[:, pl.ds(off, SSD_GW)] * xg
            zg = z_ref[rows, pl.ds(off, SSD_GW)].astype(F32)
            yg = y * (zg * jax.nn.sigmoid(zg))
            ms = jnp.mean(yg * yg, axis=-1, keepdims=True)
            yn = yg * lax.rsqrt(ms + EPS) * nw_ref[:, pl.ds(off, SSD_GW)]
            o_ref[rows, pl.ds(off, SSD_GW)] = yn.astype(o_ref.dtype)

        for g in range(SSD_GROUPS):
            group_body(g)

    for sub in range(SSD_SUBS):
        chunk(sub)
    prev_x[...] = xs_ref[(SSD_SUBS - 1) * L:, :]
    prev_bc[...] = bc_ref[(SSD_SUBS - 1) * L:, :]


def _ssd(proj, dt_raw, cwx, cbx, cwbc, cbbc, dtb, alog, dsk, nw, batch, seq):
    L = SSD_CHUNK
    LS = SSD_SUBS * L
    nc = seq // LS
    T = proj.shape[0]
    row = lambda b, c: b * nc + c
    const = lambda b, c: (0, 0)
    head_of_lane = jnp.arange(QUAD_W, dtype=jnp.int32) // SSD_HEADDIM
    qmask = (head_of_lane[None, None, :] == jnp.arange(QUAD, dtype=jnp.int32)[:, None, None])
    qmask = jnp.broadcast_to(qmask, (QUAD, L, QUAD_W)).astype(BF16)
    tap = jnp.arange(SSD_CONV - 1, dtype=jnp.int32)[:, None, None]
    t_out = jnp.arange(L, dtype=jnp.int32)[None, :, None]
    src = jnp.arange(2 * L, dtype=jnp.int32)[None, None, :]
    shift = (src == L + t_out - (SSD_CONV - 1) + tap).astype(BF16).reshape((SSD_CONV - 1) * L, 2 * L)
    return pl.pallas_call(
        _ssd_kernel,
        grid=(batch, nc),
        in_specs=[
            pl.BlockSpec((LS, SSD_INNER), lambda b, c: (row(b, c), COL_Z // SSD_INNER)),
            pl.BlockSpec((LS, SSD_INNER), lambda b, c: (row(b, c), COL_XS // SSD_INNER)),
            pl.BlockSpec((LS, 2 * SSD_BC), lambda b, c: (row(b, c), COL_BC // (2 * SSD_BC))),
            pl.BlockSpec((LS, LANES), lambda b, c: (row(b, c), 0)),
            pl.BlockSpec((SSD_CONV, SSD_INNER), const),
            pl.BlockSpec((1, SSD_INNER), const),
            pl.BlockSpec((SSD_CONV, 2 * SSD_BC), const),
            pl.BlockSpec((1, 2 * SSD_BC), const),
            pl.BlockSpec((1, LANES), const),
            pl.BlockSpec((1, LANES), const),
            pl.BlockSpec((1, SSD_INNER), const),
            pl.BlockSpec((1, SSD_INNER), const),
            pl.BlockSpec((QUAD, L, QUAD_W), lambda b, c: (0, 0, 0)),
            pl.BlockSpec(((SSD_CONV - 1) * L, 2 * L), const),
        ],
        out_specs=pl.BlockSpec((LS, SSD_INNER), lambda b, c: (row(b, c), 0)),
        out_shape=jax.ShapeDtypeStruct((T, SSD_INNER), BF16),
        scratch_shapes=[
            pltpu.VMEM((L, SSD_INNER), BF16),
            pltpu.VMEM((L, 2 * SSD_BC), BF16),
            pltpu.VMEM((SSD_SUBS, L, SSD_INNER), F32),
            pltpu.VMEM((SSD_SUBS, L, SSD_BC), F32),
            pltpu.VMEM((SSD_SUBS, L, SSD_BC), BF16),
            pltpu.VMEM((SSD_SUBS, L, LANES), F32),
            pltpu.VMEM((SSD_SUBS, LANES, L), F32),
            pltpu.VMEM((SSD_SUBS, LANES, L), F32),
            pltpu.VMEM((SSD_GROUPS, SSD_STATE, SSD_GW), F32),
        ],
        compiler_params=_cparams(("arbitrary", "arbitrary")),
        name="ssd",
    )(proj, proj, proj, dt_raw, cwx, cbx, cwbc, cbbc, dtb, alog, dsk, nw, qmask, shift)


def _pack_bf16_pairs(v):
    bits = pltpu.bitcast(v.astype(BF16).astype(F32), jnp.uint32)
    return (bits[:, :HALF] >> 16) | (bits[:, HALF:] & jnp.uint32(0xFFFF0000))


def _unpack_lo(w):
    return pltpu.bitcast(w << 16, F32)


def _unpack_hi(w):
    return pltpu.bitcast(w & jnp.uint32(0xFFFF0000), F32)


def _mixout_kernel(pg_ref, yn_ref, gs_ref, x_ref, wso_ref, wo_ref, nw_ref, rw_ref, rb_ref,
                   h_ref, u_ref, er_ref, gt_ref, cnt_ref, run_scr, lg_scr, *, tm):
    i = pl.program_id(0)
    last = pl.num_programs(0) - 1

    @pl.when(i == 0)
    def _():
        run_scr[...] = jnp.zeros_like(run_scr)
        lg_scr[...] = jnp.zeros_like(lg_scr)

    def route_previous_tile(lg_prev):
        er, gt = _route(lg_prev, run_scr, tm, i > 0)
        er_ref[...] = er
        gt_ref[...] = gt
        cnt_ref[...] = run_scr[...]

    @pl.when(i < last)
    def _():
        lg_prev = lg_scr[...]
        ssd = jnp.dot(yn_ref[...], wso_ref[...], preferred_element_type=F32)
        merged = pg_ref[...].astype(F32) + jax.nn.sigmoid(gs_ref[...].astype(F32)) * ssd
        h = x_ref[...] + jnp.dot(merged.astype(BF16), wo_ref[...], preferred_element_type=F32)
        h_ref[...] = h
        ms = jnp.mean(h * h, axis=-1, keepdims=True)
        u = h * lax.rsqrt(ms + EPS) * nw_ref[...]
        u_hi = u.astype(BF16)
        u_ref[...] = _pack_bf16_pairs(u)
        u_lo = (u - u_hi.astype(F32)).astype(BF16)
        both = jnp.dot(u_hi, rw_ref[...], preferred_element_type=F32)
        lg = both[:, :LANES] + both[:, LANES:] + jnp.dot(u_lo, rw_ref[:, :LANES], preferred_element_type=F32)
        lg_scr[...] = lg + rb_ref[...]
        route_previous_tile(lg_prev)

    @pl.when(i == last)
    def _():
        route_previous_tile(lg_scr[...])


def _mixout(pool_g, yn, proj, x2, wso, wo, nw, rw, rb, tm):
    T = x2.shape[0]
    n = T // tm
    const = lambda i: (0, 0)
    cur = lambda i: jnp.minimum(i, n - 1)
    prev = lambda i: jnp.maximum(i - 1, 0)
    return pl.pallas_call(
        functools.partial(_mixout_kernel, tm=tm),
        grid=(n + 1,),
        in_specs=[
            pl.BlockSpec((tm, D_MODEL), lambda i: (cur(i), 0)),
            pl.BlockSpec((tm, SSD_INNER), lambda i: (cur(i), 0)),
            pl.BlockSpec((tm, D_MODEL), lambda i: (cur(i), COL_GS // D_MODEL)),
            pl.BlockSpec((tm, D_MODEL), lambda i: (cur(i), 0)),
            pl.BlockSpec((SSD_INNER, D_MODEL), const),
            pl.BlockSpec((D_MODEL, D_MODEL), const),
            pl.BlockSpec((1, D_MODEL), const),
            pl.BlockSpec((D_MODEL, 2 * LANES), const),
            pl.BlockSpec((1, LANES), const),
        ],
        out_specs=[
            pl.BlockSpec((tm, D_MODEL), lambda i: (cur(i), 0)),
            pl.BlockSpec((tm, HALF), lambda i: (cur(i), 0)),
            pl.BlockSpec((tm, LANES), lambda i: (prev(i), 0)),
            pl.BlockSpec((tm, LANES), lambda i: (prev(i), 0)),
            pl.BlockSpec((1, LANES), const),
        ],
        out_shape=[
            jax.ShapeDtypeStruct((T, D_MODEL), F32),
            jax.ShapeDtypeStruct((T, HALF), jnp.uint32),
            jax.ShapeDtypeStruct((T, LANES), jnp.int32),
            jax.ShapeDtypeStruct((T, LANES), F32),
            jax.ShapeDtypeStruct((1, LANES), F32),
        ],
        scratch_shapes=[pltpu.VMEM((1, LANES), F32), pltpu.VMEM((tm, LANES), F32)],
        compiler_params=_cparams(("arbitrary",)),
        name="mixout",
    )(pool_g, yn, proj, x2, wso, wo, nw, rw, rb)


def _route(lg, run_scr, tm, valid):
    lane = lax.broadcasted_iota(jnp.int32, (tm, LANES), 1)
    lg = jnp.where(lane < N_EXPERTS, lg, -jnp.inf)
    idxs, vals, hots = [], [], []
    for _ in range(TOP_K):
        m = jnp.max(lg, axis=-1, keepdims=True)
        idx = jnp.min(jnp.where(lg == m, lane, LANES), axis=-1, keepdims=True)
        hot = lane == idx
        lg = jnp.where(hot, -jnp.inf, lg)
        idxs.append(idx)
        vals.append(m)
        hots.append(hot)
    exps = [jnp.exp(v - vals[0]) for v in vals]
    den = exps[0] + exps[1] + exps[2] + exps[3]
    cnt = sum(jnp.where(hot, 1.0, 0.0) for hot in hots)
    rr = lax.broadcasted_iota(jnp.int32, (tm, tm), 0)
    cc = lax.broadcasted_iota(jnp.int32, (tm, tm), 1)
    before = jnp.where(rr > cc, 1.0, 0.0).astype(BF16)
    base = jnp.dot(before, cnt.astype(BF16), preferred_element_type=F32) + run_scr[...]
    er = jnp.zeros((tm, LANES), jnp.int32)
    gt = jnp.zeros((tm, LANES), F32)
    for k in range(TOP_K):
        rank = jnp.sum(jnp.where(hots[k], base, 0.0), axis=-1, keepdims=True)
        er = jnp.where(lane == k, idxs[k], er)
        er = jnp.where(lane == TOP_K + k, rank.astype(jnp.int32), er)
        gt = jnp.where(lane == k, exps[k] / den, gt)
    run_scr[...] += jnp.where(valid, jnp.sum(cnt, axis=0, keepdims=True), 0.0)
    return er, gt


def _dispatch_kernel(ps_ref, e_ref, r_ref, u_ref, xb_ref, dest_ref, sem, *, tm):
    def body(g, carry):
        r0 = pl.multiple_of(g * ROW_GROUP, ROW_GROUP)
        for i in range(ROW_GROUP):
            for k in range(TOP_K):
                j = (r0 + i) * TOP_K + k
                d = ps_ref[e_ref[j]] + r_ref[j]
                dest_ref[j] = d
                pltpu.make_async_copy(u_ref.at[pl.ds(r0 + i, 1)], xb_ref.at[pl.ds(d, 1)], sem).start()
        return carry

    lax.fori_loop(0, tm // ROW_GROUP, body, 0)
    for _ in range(TOP_K):
        pltpu.make_async_copy(u_ref, xb_ref.at[pl.ds(0, tm)], sem).wait()


def _dispatch(pstart, e_flat, r_flat, u_pk, n_slots, tm):
    T = u_pk.shape[0]
    grid_spec = pltpu.PrefetchScalarGridSpec(
        num_scalar_prefetch=1,
        grid=(T // tm,),
        in_specs=[
            pl.BlockSpec((tm * TOP_K,), lambda i, ps: (i,), memory_space=pltpu.SMEM),
            pl.BlockSpec((tm * TOP_K,), lambda i, ps: (i,), memory_space=pltpu.SMEM),
            pl.BlockSpec((tm, HALF), lambda i, ps: (i, 0)),
        ],
        out_specs=[
            pl.BlockSpec(memory_space=pl.ANY),
            pl.BlockSpec((tm * TOP_K,), lambda i, ps: (i,), memory_space=pltpu.SMEM),
        ],
        scratch_shapes=[pltpu.SemaphoreType.DMA(())],
    )
    return pl.pallas_call(
        functools.partial(_dispatch_kernel, tm=tm),
        grid_spec=grid_spec,
        out_shape=[
            jax.ShapeDtypeStruct((n_slots, HALF), jnp.uint32),
            jax.ShapeDtypeStruct((T * TOP_K,), jnp.int32),
        ],
        compiler_params=_cparams(("arbitrary",)),
        name="dispatch",
    )(pstart, e_flat, r_flat, u_pk)


def _row_blocks(nrows, tm, sub, weights, block):
    nblk = (nrows + sub - 1) // sub
    per_big = min(FFN_BIG_BLOCK, tm) // sub
    nbig = tm // (per_big * sub)
    for b in range(nbig):
        @pl.when(nblk >= (b + 1) * per_big)
        def _():
            block(b * per_big * sub, per_big * sub, weights())

    def rest(sb, carry):
        block(pl.multiple_of(sb * sub, sub), sub, weights())
        return carry

    lax.fori_loop(jnp.minimum(nblk // per_big, nbig) * per_big, nblk, rest, 0)


def _ffn_up_kernel(te_ref, nu_ref, tr_ref, x_ref, wg_ref, wl_ref, bg_ref, bl_ref, o_ref, *, tm, sub):
    nrows = tr_ref[pl.program_id(0)]

    def weights():
        return wg_ref[0].astype(BF16), wl_ref[0].astype(BF16)

    def block(r0, nr, w):
        packed = x_ref[pl.ds(r0, nr), :]
        x = jnp.concatenate([_unpack_lo(packed).astype(BF16), _unpack_hi(packed).astype(BF16)], axis=1)
        glu = jnp.minimum(jnp.dot(x, w[0], preferred_element_type=F32) + bg_ref[0], SWIGLU_LIMIT)
        lin = jnp.clip(jnp.dot(x, w[1], preferred_element_type=F32) + bl_ref[0], -SWIGLU_LIMIT, SWIGLU_LIMIT)
        act = glu * jax.nn.sigmoid(SWIGLU_ALPHA * glu) * (lin + 1.0)
        o_ref[pl.ds(r0, nr), :] = act.astype(o_ref.dtype)

    _row_blocks(nrows, tm, sub, weights, block)


def _ffn_down_kernel(te_ref, nu_ref, tr_ref, a_ref, wa_ref, wb_ref, ba_ref, bb_ref, o_ref, *, tm, sub):
    nrows = tr_ref[pl.program_id(0)]

    def weights():
        return wa_ref[0].astype(BF16), wb_ref[0].astype(BF16)

    def block(r0, nr, w):
        a = a_ref[pl.ds(r0, nr), :]
        ya = jnp.dot(a, w[0], preferred_element_type=F32) + ba_ref[0]
        yb = jnp.dot(a, w[1], preferred_element_type=F32) + bb_ref[0]
        lo = pltpu.bitcast(ya.astype(BF16).astype(F32), jnp.uint32) >> 16
        hi = pltpu.bitcast(yb.astype(BF16).astype(F32), jnp.uint32) & jnp.uint32(0xFFFF0000)
        o_ref[pl.ds(r0, nr), :] = lo | hi

    _row_blocks(nrows, tm, sub, weights, block)


def _ffn(tile_expert, n_used, tile_rows, xb, w_up, b_up3, w_down, b_down3, tm, sub, fc, nc):
    n_slots = xb.shape[0]
    n_tiles = n_slots // tm
    nf = D_FF // fc
    nj = HALF // nc

    def live(i, nu):
        return jnp.minimum(i, nu[0] - 1)

    def frozen(i, j, nu, last):
        return jnp.where(i < nu[0], j, last)

    up_spec = pltpu.PrefetchScalarGridSpec(
        num_scalar_prefetch=3,
        grid=(n_tiles, nf),
        in_specs=[
            pl.BlockSpec((tm, HALF), lambda i, f, te, nu, tr: (live(i, nu), 0)),
            pl.BlockSpec((1, D_MODEL, fc), lambda i, f, te, nu, tr: (te[i], 0, frozen(i, f, nu, nf - 1))),
            pl.BlockSpec((1, D_MODEL, fc), lambda i, f, te, nu, tr: (te[i], 0, nf + frozen(i, f, nu, nf - 1))),
            pl.BlockSpec((1, 1, fc), lambda i, f, te, nu, tr: (te[i], 0, frozen(i, f, nu, nf - 1))),
            pl.BlockSpec((1, 1, fc), lambda i, f, te, nu, tr: (te[i], 0, nf + frozen(i, f, nu, nf - 1))),
        ],
        out_specs=pl.BlockSpec((tm, fc), lambda i, f, te, nu, tr: (live(i, nu), frozen(i, f, nu, nf - 1))),
    )
    act = pl.pallas_call(
        functools.partial(_ffn_up_kernel, tm=tm, sub=sub),
        grid_spec=up_spec,
        out_shape=jax.ShapeDtypeStruct((n_slots, D_FF), BF16),
        compiler_params=_cparams(("arbitrary", "arbitrary")),
        name="ffn_up",
    )(tile_expert, n_used, tile_rows, xb, w_up, w_up, b_up3, b_up3)

    down_spec = pltpu.PrefetchScalarGridSpec(
        num_scalar_prefetch=3,
        grid=(n_tiles, nj),
        in_specs=[
            pl.BlockSpec((tm, D_FF), lambda i, j, te, nu, tr: (live(i, nu), 0)),
            pl.BlockSpec((1, D_FF, nc), lambda i, j, te, nu, tr: (te[i], 0, frozen(i, j, nu, nj - 1))),
            pl.BlockSpec((1, D_FF, nc), lambda i, j, te, nu, tr: (te[i], 0, nj + frozen(i, j, nu, nj - 1))),
            pl.BlockSpec((1, 1, nc), lambda i, j, te, nu, tr: (te[i], 0, frozen(i, j, nu, nj - 1))),
            pl.BlockSpec((1, 1, nc), lambda i, j, te, nu, tr: (te[i], 0, nj + frozen(i, j, nu, nj - 1))),
        ],
        out_specs=pl.BlockSpec((tm, nc), lambda i, j, te, nu, tr: (live(i, nu), frozen(i, j, nu, nj - 1))),
    )
    return pl.pallas_call(
        functools.partial(_ffn_down_kernel, tm=tm, sub=sub),
        grid_spec=down_spec,
        out_shape=jax.ShapeDtypeStruct((n_slots, HALF), jnp.uint32),
        compiler_params=_cparams(("arbitrary", "arbitrary")),
        name="ffn_down",
    )(tile_expert, n_used, tile_rows, act, w_down, w_down, b_down3, b_down3)


def _combine_kernel(d_ref, dn_ref, h_ref, g_ref, nw_ref, yb_ref, o_ref, ybuf, sem, *, tm):
    step = pl.program_id(0)
    slot = step % 2

    def gather_rows(idx_ref, s):
        def body(g, carry):
            r0 = pl.multiple_of(g * ROW_GROUP, ROW_GROUP)
            for i in range(ROW_GROUP):
                for k in range(TOP_K):
                    d = idx_ref[(r0 + i) * TOP_K + k]
                    pltpu.make_async_copy(yb_ref.at[pl.ds(d, 1)], ybuf.at[s, k, pl.ds(r0 + i, 1)],
                                          sem.at[s]).start()
            return carry

        lax.fori_loop(0, tm // ROW_GROUP, body, 0)

    @pl.when(step == 0)
    def _():
        gather_rows(d_ref, 0)

    @pl.when(step + 1 < pl.num_programs(0))
    def _():
        gather_rows(dn_ref, 1 - slot)

    for k in range(TOP_K):
        pltpu.make_async_copy(yb_ref.at[pl.ds(0, tm)], ybuf.at[slot, k], sem.at[slot]).wait()
    lo = h_ref[:, :HALF]
    hi = h_ref[:, HALF:]
    for k in range(TOP_K):
        w = ybuf[slot, k]
        g = g_ref[:, k:k + 1]
        lo = lo + g * _unpack_lo(w)
        hi = hi + g * _unpack_hi(w)
    ms = (jnp.sum(lo * lo, axis=-1, keepdims=True) + jnp.sum(hi * hi, axis=-1, keepdims=True)) / D_MODEL
    scale = lax.rsqrt(ms + EPS)
    o_ref[:, :HALF] = lo * scale * nw_ref[:, :HALF]
    o_ref[:, HALF:] = hi * scale * nw_ref[:, HALF:]


def _combine(dest, h, gates, nw, yb, tm):
    T = h.shape[0]
    n = T // tm
    return pl.pallas_call(
        functools.partial(_combine_kernel, tm=tm),
        grid=(n,),
        in_specs=[
            pl.BlockSpec((tm * TOP_K,), lambda i: (i,), memory_space=pltpu.SMEM),
            pl.BlockSpec((tm * TOP_K,), lambda i: (jnp.minimum(i + 1, n - 1),), memory_space=pltpu.SMEM),
            pl.BlockSpec((tm, D_MODEL), lambda i: (i, 0)),
            pl.BlockSpec((tm, LANES), lambda i: (i, 0)),
            pl.BlockSpec((1, D_MODEL), lambda i: (0, 0)),
            pl.BlockSpec(memory_space=pl.ANY),
        ],
        out_specs=pl.BlockSpec((tm, D_MODEL), lambda i: (i, 0)),
        out_shape=jax.ShapeDtypeStruct((T, D_MODEL), F32),
        scratch_shapes=[pltpu.VMEM((2, TOP_K, tm, HALF), jnp.uint32), pltpu.SemaphoreType.DMA((2,))],
        compiler_params=_cparams(("arbitrary",)),
        name="combine",
    )(dest, dest, h, gates, nw, yb)


def _row(v, width=None):
    v = v.astype(F32).reshape(1, -1)
    if width is not None and v.shape[1] < width:
        v = jnp.pad(v, ((0, 0), (0, width - v.shape[1])))
    return v


def _tile(n, pref):
    t = pref
    while n % t:
        t //= 2
    return t


def kernel(x, norm_mix_w, w_in, w_pool, pool_scale, conv_w, conv_b, dt_bias, a_log, d_skip,
           ssd_norm_w, w_ssd_out, w_out, norm_ffn_w, router_w, router_b, w_up, b_up,
           w_down, b_down, norm_final_w):
    B, S, D = x.shape
    T = B * S
    x2 = x.reshape(T, D)

    o_p, o_z, o_xbc = 0, D_MODEL, D_MODEL + SSD_INNER
    o_dt = o_xbc + SSD_INNER + 2 * SSD_BC
    o_gp = o_dt + SSD_HEADS
    o_gs = o_gp + D_MODEL
    w_t = w_in.T.astype(BF16)
    w_dt_t = jnp.pad(w_t[o_dt:o_dt + SSD_HEADS], ((0, LANES - SSD_HEADS), (0, 0)))
    tn = INPROJ_TN
    row_of_block = ([o_z + c for c in range(0, SSD_INNER, tn)]
                    + [o_xbc + c for c in range(0, SSD_INNER + 2 * SSD_BC, tn)]
                    + [o_p, o_gp, o_gs])
    proj, dt_raw = _inproj(x2, _row(norm_mix_w), w_t, w_dt_t, row_of_block, _tile(T, 1024), tn)

    pool_g = _pool(proj, w_pool.astype(BF16), _row(pool_scale), _tile(S, 256), S)

    yn = _ssd(proj, dt_raw,
              conv_w[:, :SSD_INNER].astype(F32), _row(conv_b[:SSD_INNER]),
              conv_w[:, SSD_INNER:].astype(F32), _row(conv_b[SSD_INNER:]),
              _row(dt_bias, LANES), _row(a_log, LANES),
              _row(jnp.repeat(d_skip, SSD_HEADDIM)), _row(ssd_norm_w), B, S)

    rw = jnp.pad(router_w.astype(F32), ((0, 0), (0, LANES - N_EXPERTS)))
    rw_hi = rw.astype(BF16)
    rw_lo = (rw - rw_hi.astype(F32)).astype(BF16)
    h, u_pk, er, gates, cnt = _mixout(pool_g, yn, proj, x2, w_ssd_out.astype(BF16), w_out.astype(BF16),
                                      _row(norm_ffn_w), jnp.concatenate([rw_hi, rw_lo], axis=1),
                                      _row(router_b, LANES), _tile(T, 256))

    TK = T * TOP_K
    tm_ffn = 2304 if TK >= 32768 else 256
    sub = 256
    n_tiles = -(-TK // tm_ffn) + N_EXPERTS
    counts = cnt[0, :N_EXPERTS].astype(jnp.int32)
    tiles_e = (counts + tm_ffn - 1) // tm_ffn
    tend = jnp.cumsum(tiles_e)
    tstart = tend - tiles_e
    pstart = (tstart * tm_ffn).astype(jnp.int32)
    tile_id = jnp.arange(n_tiles, dtype=jnp.int32)
    tile_expert = jnp.minimum(jnp.searchsorted(tend, tile_id, side='right'), N_EXPERTS - 1).astype(jnp.int32)
    tile_rows = jnp.clip(counts[tile_expert] - (tile_id - tstart[tile_expert]) * tm_ffn, 0, tm_ffn).astype(jnp.int32)
    n_used = tend[-1].astype(jnp.int32).reshape(1)
    e_flat = er[:, :TOP_K].reshape(TK)
    r_flat = er[:, TOP_K:2 * TOP_K].reshape(TK)

    xb, dest = _dispatch(pstart, e_flat, r_flat, u_pk, n_tiles * tm_ffn, _tile(T, 512))
    yb = _ffn(tile_expert, n_used, tile_rows, xb, w_up, b_up.reshape(N_EXPERTS, 1, 2 * D_FF),
              w_down, b_down.reshape(N_EXPERTS, 1, D_MODEL), tm_ffn, sub, 512, 512)
    out = _combine(dest, h, gates, _row(norm_final_w), yb, _tile(T, 256))
    return out.reshape(B, S, D)
```

```python
import functools

import jax
import jax.numpy as jnp
from jax import lax
from jax.experimental import pallas as pl
from jax.experimental.pallas import tpu as pltpu

F32 = jnp.float32
BF16 = jnp.bfloat16

D_MODEL = 2048
EPS = 1e-5
POOL_GROUPS = 4
POOL_WINDOWS = (2, 4, 8, 16)
POOL_GW = D_MODEL // POOL_GROUPS
SSD_INNER = 2 * D_MODEL
SSD_HEADDIM = 64
SSD_HEADS = SSD_INNER // SSD_HEADDIM
SSD_GROUPS = 8
SSD_HPG = SSD_HEADS // SSD_GROUPS
SSD_STATE = 128
SSD_CONV = 4
SSD_CHUNK = 128
SSD_GW = SSD_HPG * SSD_HEADDIM
SSD_BC = SSD_GROUPS * SSD_STATE
N_EXPERTS = 32
TOP_K = 4
D_FF = D_MODEL
SWIGLU_ALPHA = 1.702
SWIGLU_LIMIT = 7.0

LANES = 128
BF16_ROWS = 16
HALF = D_MODEL // 2
CONV_SLAB = 256
QUAD = 4
QUAD_W = QUAD * SSD_HEADDIM
SSD_SUBS = 2
INPROJ_TN = 2048
FFN_BIG_BLOCK = 1024
ROW_GROUP = 8
VMEM_LIMIT = 56 * 1024 * 1024

COL_Z = 0
COL_XS = SSD_INNER
COL_BC = 2 * SSD_INNER
COL_P = 2 * SSD_INNER + 2 * SSD_BC
COL_GP = COL_P + D_MODEL
COL_GS = COL_GP + D_MODEL
PROJ_COLS = COL_GS + D_MODEL

NEG_BIG = -1e30


def _cparams(sem):
    return pltpu.CompilerParams(dimension_semantics=sem, vmem_limit_bytes=VMEM_LIMIT)


_NT = (((1,), (1,)), ((), ()))


def _inproj_kernel(x_ref, nw_ref, w_ref, wdt_ref, o_ref, dt_ref, u_scr):
    @pl.when(pl.program_id(1) == 0)
    def _():
        x = x_ref[...]
        ms = jnp.mean(x * x, axis=-1, keepdims=True)
        u = (x * lax.rsqrt(ms + EPS) * nw_ref[...]).astype(BF16)
        u_scr[...] = u
        dt_ref[...] = lax.dot_general(u, wdt_ref[...], _NT, preferred_element_type=F32)

    o_ref[...] = lax.dot_general(u_scr[...], w_ref[...], _NT, preferred_element_type=F32).astype(o_ref.dtype)


def _inproj(x2, norm_w, w_t, w_dt_t, row_of_block, tm, tn):
    T = x2.shape[0]

    def w_rows(i, j):
        row = row_of_block[-1]
        for b in range(len(row_of_block) - 2, -1, -1):
            row = jnp.where(j == b, row_of_block[b], row)
        return pl.multiple_of(row, BF16_ROWS), 0

    return pl.pallas_call(
        _inproj_kernel,
        grid=(T // tm, PROJ_COLS // tn),
        in_specs=[
            pl.BlockSpec((tm, D_MODEL), lambda i, j: (i, 0)),
            pl.BlockSpec((1, D_MODEL), lambda i, j: (0, 0)),
            pl.BlockSpec((pl.Element(tn), pl.Element(D_MODEL)), w_rows),
            pl.BlockSpec((LANES, D_MODEL), lambda i, j: (0, 0)),
        ],
        out_specs=[
            pl.BlockSpec((tm, tn), lambda i, j: (i, j)),
            pl.BlockSpec((tm, LANES), lambda i, j: (i, 0)),
        ],
        out_shape=[
            jax.ShapeDtypeStruct((T, PROJ_COLS), BF16),
            jax.ShapeDtypeStruct((T, LANES), F32),
        ],
        scratch_shapes=[pltpu.VMEM((tm, D_MODEL), BF16)],
        compiler_params=_cparams(("parallel", "arbitrary")),
        name="inproj",
    )(x2, norm_w, w_t, w_dt_t)


def _pool_kernel(prev_ref, p_ref, g_ref, band_ref, hband_ref, wp_ref, ps_ref, o_ref, *, tm, seq):
    row0 = (pl.program_id(0) * tm) % seq
    pos = row0 + lax.broadcasted_iota(jnp.int32, (tm, 1), 0)
    for g, w in enumerate(POOL_WINDOWS):
        cols = slice(g * POOL_GW, (g + 1) * POOL_GW)
        cur = p_ref[:, cols]
        prev = prev_ref[:, cols]
        prev = jnp.where(row0 == 0, jnp.zeros_like(prev), prev)
        acc = (jnp.dot(band_ref[g], cur, preferred_element_type=F32)
               + jnp.dot(hband_ref[g], prev, preferred_element_type=F32))
        cnt = jnp.minimum(pos + 1, w).astype(F32)
        pooled = acc / cnt - cur.astype(F32)
        y = jnp.dot(pooled.astype(BF16), wp_ref[g], preferred_element_type=F32)
        y = y * ps_ref[:, cols] * jax.nn.sigmoid(g_ref[:, cols].astype(F32))
        o_ref[:, cols] = y.astype(o_ref.dtype)


def _pool(proj, w_pool, pool_scale, tm, seq):
    T = proj.shape[0]
    hist = max(POOL_WINDOWS)
    t_out = jnp.arange(tm, dtype=jnp.int32)[None, :, None]
    src = jnp.arange(hist + tm, dtype=jnp.int32)[None, None, :]
    win = jnp.asarray(POOL_WINDOWS, jnp.int32)[:, None, None]
    band = jnp.logical_and(src > hist + t_out - win, src <= hist + t_out).astype(BF16)
    return pl.pallas_call(
        functools.partial(_pool_kernel, tm=tm, seq=seq),
        grid=(T // tm,),
        in_specs=[
            pl.BlockSpec((hist, D_MODEL), lambda i: (jnp.maximum(i * (tm // hist) - 1, 0), COL_P // D_MODEL)),
            pl.BlockSpec((tm, D_MODEL), lambda i: (i, COL_P // D_MODEL)),
            pl.BlockSpec((tm, D_MODEL), lambda i: (i, COL_GP // D_MODEL)),
            pl.BlockSpec((POOL_GROUPS, tm, tm), lambda i: (0, 0, 0)),
            pl.BlockSpec((POOL_GROUPS, tm, hist), lambda i: (0, 0, 0)),
            pl.BlockSpec((POOL_GROUPS, POOL_GW, POOL_GW), lambda i: (0, 0, 0)),
            pl.BlockSpec((1, D_MODEL), lambda i: (0, 0)),
        ],
        out_specs=pl.BlockSpec((tm, D_MODEL), lambda i: (i, 0)),
        out_shape=jax.ShapeDtypeStruct((T, D_MODEL), BF16),
        compiler_params=_cparams(("parallel",)),
        name="pool",
    )(proj, proj, proj, band[:, :, hist:], band[:, :, :hist], w_pool, pool_scale)


def _split3(v):
    hi = v.astype(BF16)
    r1 = v - hi.astype(F32)
    mid = r1.astype(BF16)
    lo = (r1 - mid.astype(F32)).astype(BF16)
    return hi, mid, lo


def _ssd_kernel(z_ref, xs_ref, bc_ref, dt_ref, cwx_ref, cbx_ref, cwbc_ref, cbbc_ref,
                dtb_ref, alog_ref, dsk_ref, nw_ref, qm_ref, sh_ref, o_ref,
                prev_x, prev_bc, xs_scr, b_scr, c_scr, a_scr, qt_scr, wt_scr, state):
    L = SSD_CHUNK

    @pl.when(pl.program_id(1) == 0)
    def _():
        prev_x[...] = jnp.zeros_like(prev_x)
        prev_bc[...] = jnp.zeros_like(prev_bc)
        state[...] = jnp.zeros_like(state)

    rr = lax.broadcasted_iota(jnp.int32, (L, L), 0)
    cc = lax.broadcasted_iota(jnp.int32, (L, L), 1)
    causal = rr >= cc
    low_half = cc < SSD_HEADDIM
    tri = jnp.where(causal, 1.0, 0.0).astype(BF16)

    def conv_slab(sub, src_ref, prev_ref, cw_ref, cb_ref, col):
        cols = slice(col, col + CONV_SLAB)
        raw = src_ref[sub * L:(sub + 1) * L, cols]
        before = prev_ref[:, cols] if sub == 0 else src_ref[(sub - 1) * L:sub * L, cols]
        both = jnp.concatenate([before, raw], axis=0)
        acc = cb_ref[:, cols] + cw_ref[SSD_CONV - 1:SSD_CONV, cols] * raw.astype(F32)
        for k in range(SSD_CONV - 1):
            shifted = jnp.dot(sh_ref[k * L:(k + 1) * L, :], both, preferred_element_type=F32)
            acc = acc + cw_ref[k:k + 1, cols] * shifted
        return acc * jax.nn.sigmoid(acc)

    def chunk(sub):
        rows = slice(sub * L, (sub + 1) * L)
        for s in range(SSD_INNER // CONV_SLAB):
            xs_scr[sub, :, s * CONV_SLAB:(s + 1) * CONV_SLAB] = conv_slab(
                sub, xs_ref, prev_x, cwx_ref, cbx_ref, s * CONV_SLAB)
        for s in range(SSD_BC // CONV_SLAB):
            b_scr[sub, :, s * CONV_SLAB:(s + 1) * CONV_SLAB] = conv_slab(
                sub, bc_ref, prev_bc, cwbc_ref, cbbc_ref, s * CONV_SLAB)
        for s in range(SSD_BC // CONV_SLAB):
            c_scr[sub, :, s * CONV_SLAB:(s + 1) * CONV_SLAB] = conv_slab(
                sub, bc_ref, prev_bc, cwbc_ref, cbbc_ref, SSD_BC + s * CONV_SLAB).astype(BF16)

        dtv = jax.nn.softplus(dt_ref[rows, :] + dtb_ref[...])
        d_a = dtv * (-jnp.exp(alog_ref[...]))
        a_cum = sum(jnp.dot(tri, part, preferred_element_type=F32) for part in _split3(d_a))
        a_t = a_cum.T
        dt_t = dtv.T
        a_scr[sub] = a_cum
        qt_scr[sub] = a_t - jnp.log(dt_t)
        wt_scr[sub] = dt_t * jnp.exp(a_t[:, L - 1:L] - a_t)

        def group_body(g):
            off = g * SSD_GW
            offn = g * SSD_STATE
            xg = xs_scr[sub, :, pl.ds(off, SSD_GW)]
            xg_b = xg.astype(BF16)
            b_f = b_scr[sub, :, pl.ds(offn, SSD_STATE)]
            c_b = c_scr[sub, :, pl.ds(offn, SSD_STATE)]
            b_t = b_f.T
            cbm = lax.dot_general(c_b, b_f.astype(BF16), (((1,), (1,)), ((), ())),
                                  preferred_element_type=F32)
            s_g = state[g]
            y_off = jnp.dot(c_b, s_g.astype(BF16), preferred_element_type=F32)
            a_all = a_scr[sub]
            ys = []
            for quad in range(SSD_HPG // QUAD):
                qs = slice(quad * QUAD_W, (quad + 1) * QUAD_W)
                m_parts, b_parts, a_cols = [], [], []
                for jj in range(QUAD):
                    j = quad * QUAD + jj
                    h = g * SSD_HPG + j
                    a_col = jnp.broadcast_to(a_all[:, h:h + 1], (L, L))
                    seg = a_col - qt_scr[sub, pl.ds(h, 1), :]
                    m_parts.append((cbm * jnp.exp(jnp.where(causal, seg, NEG_BIG))).astype(BF16))
                    b_parts.append((b_t * wt_scr[sub, pl.ds(h, 1), :]).astype(BF16))
                    a_cols.append(a_col)
                lhs = jnp.concatenate([jnp.concatenate(m_parts, axis=1), jnp.concatenate(b_parts, axis=1)], axis=0)
                xq = xg_b[:, qs]
                rhs = jnp.concatenate([xq * qm_ref[jj] for jj in range(QUAD)], axis=0)
                res = jnp.dot(lhs, rhs, preferred_element_type=F32)
                ea_q = jnp.concatenate(
                    [jnp.exp(jnp.where(low_half, a_cols[2 * p], a_cols[2 * p + 1])) for p in range(QUAD // 2)],
                    axis=1)
                ys.append(res[:L] + y_off[:, qs] * ea_q)
                state[g, :, qs] = s_g[:, qs] * ea_q[L - 1:L, :] + res[L:]
            y = jnp.concatenate(ys, axis=1) + dsk_ref[:, pl.ds(off, SSD_GW)] * xg
            zg = z_ref[rows, pl.ds(off, SSD_GW)].astype(F32)
            yg = y * (zg * jax.nn.sigmoid(zg))
            ms = jnp.mean(yg * yg, axis=-1, keepdims=True)
            yn = yg * lax.rsqrt(ms + EPS) * nw_ref[:, pl.ds(off, SSD_GW)]
            o_ref[rows, pl.ds(off, SSD_GW)] = yn.astype(o_ref.dtype)

        for g in range(SSD_GROUPS):
            group_body(g)

    for sub in range(SSD_SUBS):
        chunk(sub)
    prev_x[...] = xs_ref[(SSD_SUBS - 1) * L:, :]
    prev_bc[...] = bc_ref[(SSD_SUBS - 1) * L:, :]


def _ssd(proj, dt_raw, cwx, cbx, cwbc, cbbc, dtb, alog, dsk, nw, batch, seq):
    L = SSD_CHUNK
    LS = SSD_SUBS * L
    nc = seq // LS
    T = proj.shape[0]
    row = lambda b, c: b * nc + c
    const = lambda b, c: (0, 0)
    head_of_lane = jnp.arange(QUAD_W, dtype=jnp.int32) // SSD_HEADDIM
    qmask = (head_of_lane[None, None, :] == jnp.arange(QUAD, dtype=jnp.int32)[:, None, None])
    qmask = jnp.broadcast_to(qmask, (QUAD, L, QUAD_W)).astype(BF16)
    tap = jnp.arange(SSD_CONV - 1, dtype=jnp.int32)[:, None, None]
    t_out = jnp.arange(L, dtype=jnp.int32)[None, :, None]
    src = jnp.arange(2 * L, dtype=jnp.int32)[None, None, :]
    shift = (src == L + t_out - (SSD_CONV - 1) + tap).astype(BF16).reshape((SSD_CONV - 1) * L, 2 * L)
    return pl.pallas_call(
        _ssd_kernel,
        grid=(batch, nc),
        in_specs=[
            pl.BlockSpec((LS, SSD_INNER), lambda b, c: (row(b, c), COL_Z // SSD_INNER)),
            pl.BlockSpec((LS, SSD_INNER), lambda b, c: (row(b, c), COL_XS // SSD_INNER)),
            pl.BlockSpec((LS, 2 * SSD_BC), lambda b, c: (row(b, c), COL_BC // (2 * SSD_BC))),
            pl.BlockSpec((LS, LANES), lambda b, c: (row(b, c), 0)),
            pl.BlockSpec((SSD_CONV, SSD_INNER), const),
            pl.BlockSpec((1, SSD_INNER), const),
            pl.BlockSpec((SSD_CONV, 2 * SSD_BC), const),
            pl.BlockSpec((1, 2 * SSD_BC), const),
            pl.BlockSpec((1, LANES), const),
            pl.BlockSpec((1, LANES), const),
            pl.BlockSpec((1, SSD_INNER), const),
            pl.BlockSpec((1, SSD_INNER), const),
            pl.BlockSpec((QUAD, L, QUAD_W), lambda b, c: (0, 0, 0)),
            pl.BlockSpec(((SSD_CONV - 1) * L, 2 * L), const),
        ],
        out_specs=pl.BlockSpec((LS, SSD_INNER), lambda b, c: (row(b, c), 0)),
        out_shape=jax.ShapeDtypeStruct((T, SSD_INNER), BF16),
        scratch_shapes=[
            pltpu.VMEM((L, SSD_INNER), BF16),
            pltpu.VMEM((L, 2 * SSD_BC), BF16),
            pltpu.VMEM((SSD_SUBS, L, SSD_INNER), F32),
            pltpu.VMEM((SSD_SUBS, L, SSD_BC), F32),
            pltpu.VMEM((SSD_SUBS, L, SSD_BC), BF16),
            pltpu.VMEM((SSD_SUBS, L, LANES), F32),
            pltpu.VMEM((SSD_SUBS, LANES, L), F32),
            pltpu.VMEM((SSD_SUBS, LANES, L), F32),
            pltpu.VMEM((SSD_GROUPS, SSD_STATE, SSD_GW), F32),
        ],
        compiler_params=_cparams(("arbitrary", "arbitrary")),
        name="ssd",
    )(proj, proj, proj, dt_raw, cwx, cbx, cwbc, cbbc, dtb, alog, dsk, nw, qmask, shift)


def _pack_bf16_pairs(v):
    bits = pltpu.bitcast(v.astype(BF16).astype(F32), jnp.uint32)
    return (bits[:, :HALF] >> 16) | (bits[:, HALF:] & jnp.uint32(0xFFFF0000))


def _unpack_lo(w):
    return pltpu.bitcast(w << 16, F32)


def _unpack_hi(w):
    return pltpu.bitcast(w & jnp.uint32(0xFFFF0000), F32)


def _mixout_kernel(pg_ref, yn_ref, gs_ref, x_ref, wso_ref, wo_ref, nw_ref, rw_ref, rb_ref,
                   h_ref, u_ref, er_ref, gt_ref, cnt_ref, run_scr, lg_scr, *, tm):
    i = pl.program_id(0)
    last = pl.num_programs(0) - 1

    @pl.when(i == 0)
    def _():
        run_scr[...] = jnp.zeros_like(run_scr)
        lg_scr[...] = jnp.zeros_like(lg_scr)

    def route_previous_tile(lg_prev):
        er, gt = _route(lg_prev, run_scr, tm, i > 0)
        er_ref[...] = er
        gt_ref[...] = gt
        cnt_ref[...] = run_scr[...]

    @pl.when(i < last)
    def _():
        lg_prev = lg_scr[...]
        ssd = jnp.dot(yn_ref[...], wso_ref[...], preferred_element_type=F32)
        merged = pg_ref[...].astype(F32) + jax.nn.sigmoid(gs_ref[...].astype(F32)) * ssd
        h = x_ref[...] + jnp.dot(merged.astype(BF16), wo_ref[...], preferred_element_type=F32)
        h_ref[...] = h
        ms = jnp.mean(h * h, axis=-1, keepdims=True)
        u = h * lax.rsqrt(ms + EPS) * nw_ref[...]
        u_hi = u.astype(BF16)
        u_ref[...] = _pack_bf16_pairs(u)
        u_lo = (u - u_hi.astype(F32)).astype(BF16)
        both = jnp.dot(u_hi, rw_ref[...], preferred_element_type=F32)
        lg = both[:, :LANES] + both[:, LANES:] + jnp.dot(u_lo, rw_ref[:, :LANES], preferred_element_type=F32)
        lg_scr[...] = lg + rb_ref[...]
        route_previous_tile(lg_prev)

    @pl.when(i == last)
    def _():
        route_previous_tile(lg_scr[...])


def _mixout(pool_g, yn, proj, x2, wso, wo, nw, rw, rb, tm):
    T = x2.shape[0]
    n = T // tm
    const = lambda i: (0, 0)
    cur = lambda i: jnp.minimum(i, n - 1)
    prev = lambda i: jnp.maximum(i - 1, 0)
    return pl.pallas_call(
        functools.partial(_mixout_kernel, tm=tm),
        grid=(n + 1,),
        in_specs=[
            pl.BlockSpec((tm, D_MODEL), lambda i: (cur(i), 0)),
            pl.BlockSpec((tm, SSD_INNER), lambda i: (cur(i), 0)),
            pl.BlockSpec((tm, D_MODEL), lambda i: (cur(i), COL_GS // D_MODEL)),
            pl.BlockSpec((tm, D_MODEL), lambda i: (cur(i), 0)),
            pl.BlockSpec((SSD_INNER, D_MODEL), const),
            pl.BlockSpec((D_MODEL, D_MODEL), const),
            pl.BlockSpec((1, D_MODEL), const),
            pl.BlockSpec((D_MODEL, 2 * LANES), const),
            pl.BlockSpec((1, LANES), const),
        ],
        out_specs=[
            pl.BlockSpec((tm, D_MODEL), lambda i: (cur(i), 0)),
            pl.BlockSpec((tm, HALF), lambda i: (cur(i), 0)),
            pl.BlockSpec((tm, LANES), lambda i: (prev(i), 0)),
            pl.BlockSpec((tm, LANES), lambda i: (prev(i), 0)),
            pl.BlockSpec((1, LANES), const),
        ],
        out_shape=[
            jax.ShapeDtypeStruct((T, D_MODEL), F32),
            jax.ShapeDtypeStruct((T, HALF), jnp.uint32),
            jax.ShapeDtypeStruct((T, LANES), jnp.int32),
            jax.ShapeDtypeStruct((T, LANES), F32),
            jax.ShapeDtypeStruct((1, LANES), F32),
        ],
        scratch_shapes=[pltpu.VMEM((1, LANES), F32), pltpu.VMEM((tm, LANES), F32)],
        compiler_params=_cparams(("arbitrary",)),
        name="mixout",
    )(pool_g, yn, proj, x2, wso, wo, nw, rw, rb)


def _route(lg, run_scr, tm, valid):
    lane = lax.broadcasted_iota(jnp.int32, (tm, LANES), 1)
    lg = jnp.where(lane < N_EXPERTS, lg, -jnp.inf)
    idxs, vals, hots = [], [], []
    for _ in range(TOP_K):
        m = jnp.max(lg, axis=-1, keepdims=True)
        idx = jnp.min(jnp.where(lg == m, lane, LANES), axis=-1, keepdims=True)
        hot = lane == idx
        lg = jnp.where(hot, -jnp.inf, lg)
        idxs.append(idx)
        vals.append(m)
        hots.append(hot)
    exps = [jnp.exp(v - vals[0]) for v in vals]
    den = exps[0] + exps[1] + exps[2] + exps[3]
    cnt = sum(jnp.where(hot, 1.0, 0.0) for hot in hots)
    rr = lax.broadcasted_iota(jnp.int32, (tm, tm), 0)
    cc = lax.broadcasted_iota(jnp.int32, (tm, tm), 1)
    before = jnp.where(rr > cc, 1.0, 0.0).astype(BF16)
    base = jnp.dot(before, cnt.astype(BF16), preferred_element_type=F32) + run_scr[...]
    er = jnp.zeros((tm, LANES), jnp.int32)
    gt = jnp.zeros((tm, LANES), F32)
    for k in range(TOP_K):
        rank = jnp.sum(jnp.where(hots[k], base, 0.0), axis=-1, keepdims=True)
        er = jnp.where(lane == k, idxs[k], er)
        er = jnp.where(lane == TOP_K + k, rank.astype(jnp.int32), er)
        gt = jnp.where(lane == k, exps[k] / den, gt)
    run_scr[...] += jnp.where(valid, jnp.sum(cnt, axis=0, keepdims=True), 0.0)
    return er, gt


def _dispatch_kernel(ps_ref, e_ref, r_ref, u_ref, xb_ref, dest_ref, sem, *, tm):
    def body(g, carry):
        r0 = pl.multiple_of(g * ROW_GROUP, ROW_GROUP)
        for i in range(ROW_GROUP):
            for k in range(TOP_K):
                j = (r0 + i) * TOP_K + k
                d = ps_ref[e_ref[j]] + r_ref[j]
                dest_ref[j] = d
                pltpu.make_async_copy(u_ref.at[pl.ds(r0 + i, 1)], xb_ref.at[pl.ds(d, 1)], sem).start()
        return carry

    lax.fori_loop(0, tm // ROW_GROUP, body, 0)
    for _ in range(TOP_K):
        pltpu.make_async_copy(u_ref, xb_ref.at[pl.ds(0, tm)], sem).wait()


def _dispatch(pstart, e_flat, r_flat, u_pk, n_slots, tm):
    T = u_pk.shape[0]
    grid_spec = pltpu.PrefetchScalarGridSpec(
        num_scalar_prefetch=1,
        grid=(T // tm,),
        in_specs=[
            pl.BlockSpec((tm * TOP_K,), lambda i, ps: (i,), memory_space=pltpu.SMEM),
            pl.BlockSpec((tm * TOP_K,), lambda i, ps: (i,), memory_space=pltpu.SMEM),
            pl.BlockSpec((tm, HALF), lambda i, ps: (i, 0)),
        ],
        out_specs=[
            pl.BlockSpec(memory_space=pl.ANY),
            pl.BlockSpec((tm * TOP_K,), lambda i, ps: (i,), memory_space=pltpu.SMEM),
        ],
        scratch_shapes=[pltpu.SemaphoreType.DMA(())],
    )
    return pl.pallas_call(
        functools.partial(_dispatch_kernel, tm=tm),
        grid_spec=grid_spec,
        out_shape=[
            jax.ShapeDtypeStruct((n_slots, HALF), jnp.uint32),
            jax.ShapeDtypeStruct((T * TOP_K,), jnp.int32),
        ],
        compiler_params=_cparams(("arbitrary",)),
        name="dispatch",
    )(pstart, e_flat, r_flat, u_pk)


def _row_blocks(nrows, tm, sub, weights, block):
    nblk = (nrows + sub - 1) // sub
    per_big = min(FFN_BIG_BLOCK, tm) // sub
    nbig = tm // (per_big * sub)
    for b in range(nbig):
        @pl.when(nblk >= (b + 1) * per_big)
        def _():
            block(b * per_big * sub, per_big * sub, weights())

    def rest(sb, carry):
        block(pl.multiple_of(sb * sub, sub), sub, weights())
        return carry

    lax.fori_loop(jnp.minimum(nblk // per_big, nbig) * per_big, nblk, rest, 0)


def _ffn_up_kernel(te_ref, nu_ref, tr_ref, x_ref, wg_ref, wl_ref, bg_ref, bl_ref, o_ref, *, tm, sub):
    nrows = tr_ref[pl.program_id(0)]

    def weights():
        return wg_ref[0].astype(BF16), wl_ref[0].astype(BF16)

    def block(r0, nr, w):
        packed = x_ref[pl.ds(r0, nr), :]
        x = jnp.concatenate([_unpack_lo(packed).astype(BF16), _unpack_hi(packed).astype(BF16)], axis=1)
        glu = jnp.minimum(jnp.dot(x, w[0], preferred_element_type=F32) + bg_ref[0], SWIGLU_LIMIT)
        lin = jnp.clip(jnp.dot(x, w[1], preferred_element_type=F32) + bl_ref[0], -SWIGLU_LIMIT, SWIGLU_LIMIT)
        act = glu * jax.nn.sigmoid(SWIGLU_ALPHA * glu) * (lin + 1.0)
        o_ref[pl.ds(r0, nr), :] = act.astype(o_ref.dtype)

    _row_blocks(nrows, tm, sub, weights, block)


def _ffn_down_kernel(te_ref, nu_ref, tr_ref, a_ref, wa_ref, wb_ref, ba_ref, bb_ref, o_ref, *, tm, sub):
    nrows = tr_ref[pl.program_id(0)]

    def weights():
        return wa_ref[0].astype(BF16), wb_ref[0].astype(BF16)

    def block(r0, nr, w):
        a = a_ref[pl.ds(r0, nr), :]
        ya = jnp.dot(a, w[0], preferred_element_type=F32) + ba_ref[0]
        yb = jnp.dot(a, w[1], preferred_element_type=F32) + bb_ref[0]
        lo = pltpu.bitcast(ya.astype(BF16).astype(F32), jnp.uint32) >> 16
        hi = pltpu.bitcast(yb.astype(BF16).astype(F32), jnp.uint32) & jnp.uint32(0xFFFF0000)
        o_ref[pl.ds(r0, nr), :] = lo | hi

    _row_blocks(nrows, tm, sub, weights, block)


def _ffn(tile_expert, n_used, tile_rows, xb, w_up, b_up3, w_down, b_down3, tm, sub, fc, nc):
    n_slots = xb.shape[0]
    n_tiles = n_slots // tm
    nf = D_FF // fc
    nj = HALF // nc

    def live(i, nu):
        return jnp.minimum(i, nu[0] - 1)

    def frozen(i, j, nu, last):
        return jnp.where(i < nu[0], j, last)

    up_spec = pltpu.PrefetchScalarGridSpec(
        num_scalar_prefetch=3,
        grid=(n_tiles, nf),
        in_specs=[
            pl.BlockSpec((tm, HALF), lambda i, f, te, nu, tr: (live(i, nu), 0)),
            pl.BlockSpec((1, D_MODEL, fc), lambda i, f, te, nu, tr: (te[i], 0, frozen(i, f, nu, nf - 1))),
            pl.BlockSpec((1, D_MODEL, fc), lambda i, f, te, nu, tr: (te[i], 0, nf + frozen(i, f, nu, nf - 1))),
            pl.BlockSpec((1, 1, fc), lambda i, f, te, nu, tr: (te[i], 0, frozen(i, f, nu, nf - 1))),
            pl.BlockSpec((1, 1, fc), lambda i, f, te, nu, tr: (te[i], 0, nf + frozen(i, f, nu, nf - 1))),
        ],
        out_specs=pl.BlockSpec((tm, fc), lambda i, f, te, nu, tr: (live(i, nu), frozen(i, f, nu, nf - 1))),
    )
    act = pl.pallas_call(
        functools.partial(_ffn_up_kernel, tm=tm, sub=sub),
        grid_spec=up_spec,
        out_shape=jax.ShapeDtypeStruct((n_slots, D_FF), BF16),
        compiler_params=_cparams(("arbitrary", "arbitrary")),
        name="ffn_up",
    )(tile_expert, n_used, tile_rows, xb, w_up, w_up, b_up3, b_up3)

    down_spec = pltpu.PrefetchScalarGridSpec(
        num_scalar_prefetch=3,
        grid=(n_tiles, nj),
        in_specs=[
            pl.BlockSpec((tm, D_FF), lambda i, j, te, nu, tr: (live(i, nu), 0)),
            pl.BlockSpec((1, D_FF, nc), lambda i, j, te, nu, tr: (te[i], 0, frozen(i, j, nu, nj - 1))),
            pl.BlockSpec((1, D_FF, nc), lambda i, j, te, nu, tr: (te[i], 0, nj + frozen(i, j, nu, nj - 1))),
            pl.BlockSpec((1, 1, nc), lambda i, j, te, nu, tr: (te[i], 0, frozen(i, j, nu, nj - 1))),
            pl.BlockSpec((1, 1, nc), lambda i, j, te, nu, tr: (te[i], 0, nj + frozen(i, j, nu, nj - 1))),
        ],
        out_specs=pl.BlockSpec((tm, nc), lambda i, j, te, nu, tr: (live(i, nu), frozen(i, j, nu, nj - 1))),
    )
    return pl.pallas_call(
        functools.partial(_ffn_down_kernel, tm=tm, sub=sub),
        grid_spec=down_spec,
        out_shape=jax.ShapeDtypeStruct((n_slots, HALF), jnp.uint32),
        compiler_params=_cparams(("arbitrary", "arbitrary")),
        name="ffn_down",
    )(tile_expert, n_used, tile_rows, act, w_down, w_down, b_down3, b_down3)


def _combine_kernel(d_ref, dn_ref, h_ref, g_ref, nw_ref, yb_ref, o_ref, ybuf, sem, *, tm):
    step = pl.program_id(0)
    slot = step % 2

    def gather_rows(idx_ref, s):
        def body(g, carry):
            r0 = pl.multiple_of(g * ROW_GROUP, ROW_GROUP)
            for i in range(ROW_GROUP):
                for k in range(TOP_K):
                    d = idx_ref[(r0 + i) * TOP_K + k]
                    pltpu.make_async_copy(yb_ref.at[pl.ds(d, 1)], ybuf.at[s, k, pl.ds(r0 + i, 1)],
                                          sem.at[s]).start()
            return carry

        lax.fori_loop(0, tm // ROW_GROUP, body, 0)

    @pl.when(step == 0)
    def _():
        gather_rows(d_ref, 0)

    @pl.when(step + 1 < pl.num_programs(0))
    def _():
        gather_rows(dn_ref, 1 - slot)

    for k in range(TOP_K):
        pltpu.make_async_copy(yb_ref.at[pl.ds(0, tm)], ybuf.at[slot, k], sem.at[slot]).wait()
    lo = h_ref[:, :HALF]
    hi = h_ref[:, HALF:]
    for k in range(TOP_K):
        w = ybuf[slot, k]
        g = g_ref[:, k:k + 1]
        lo = lo + g * _unpack_lo(w)
        hi = hi + g * _unpack_hi(w)
    ms = (jnp.sum(lo * lo, axis=-1, keepdims=True) + jnp.sum(hi * hi, axis=-1, keepdims=True)) / D_MODEL
    scale = lax.rsqrt(ms + EPS)
    o_ref[:, :HALF] = lo * scale * nw_ref[:, :HALF]
    o_ref[:, HALF:] = hi * scale * nw_ref[:, HALF:]


def _combine(dest, h, gates, nw, yb, tm):
    T = h.shape[0]
    n = T // tm
    return pl.pallas_call(
        functools.partial(_combine_kernel, tm=tm),
        grid=(n,),
        in_specs=[
            pl.BlockSpec((tm * TOP_K,), lambda i: (i,), memory_space=pltpu.SMEM),
            pl.BlockSpec((tm * TOP_K,), lambda i: (jnp.minimum(i + 1, n - 1),), memory_space=pltpu.SMEM),
            pl.BlockSpec((tm, D_MODEL), lambda i: (i, 0)),
            pl.BlockSpec((tm, LANES), lambda i: (i, 0)),
            pl.BlockSpec((1, D_MODEL), lambda i: (0, 0)),
            pl.BlockSpec(memory_space=pl.ANY),
        ],
        out_specs=pl.BlockSpec((tm, D_MODEL), lambda i: (i, 0)),
        out_shape=jax.ShapeDtypeStruct((T, D_MODEL), F32),
        scratch_shapes=[pltpu.VMEM((2, TOP_K, tm, HALF), jnp.uint32), pltpu.SemaphoreType.DMA((2,))],
        compiler_params=_cparams(("arbitrary",)),
        name="combine",
    )(dest, dest, h, gates, nw, yb)


def _row(v, width=None):
    v = v.astype(F32).reshape(1, -1)
    if width is not None and v.shape[1] < width:
        v = jnp.pad(v, ((0, 0), (0, width - v.shape[1])))
    return v


def _tile(n, pref):
    t = pref
    while n % t:
        t //= 2
    return t


def kernel(x, norm_mix_w, w_in, w_pool, pool_scale, conv_w, conv_b, dt_bias, a_log, d_skip,
           ssd_norm_w, w_ssd_out, w_out, norm_ffn_w, router_w, router_b, w_up, b_up,
           w_down, b_down, norm_final_w):
    B, S, D = x.shape
    T = B * S
    x2 = x.reshape(T, D)

    o_p, o_z, o_xbc = 0, D_MODEL, D_MODEL + SSD_INNER
    o_dt = o_xbc + SSD_INNER + 2 * SSD_BC
    o_gp = o_dt + SSD_HEADS
    o_gs = o_gp + D_MODEL
    w_t = w_in.T.astype(BF16)
    w_dt_t = jnp.pad(w_t[o_dt:o_dt + SSD_HEADS], ((0, LANES - SSD_HEADS), (0, 0)))
    tn = INPROJ_TN
    row_of_block = ([o_z + c for c in range(0, SSD_INNER, tn)]
                    + [o_xbc + c for c in range(0, SSD_INNER + 2 * SSD_BC, tn)]
                    + [o_p, o_gp, o_gs])
    proj, dt_raw = _inproj(x2, _row(norm_mix_w), w_t, w_dt_t, row_of_block, _tile(T, 1024), tn)

    pool_g = _pool(proj, w_pool.astype(BF16), _row(pool_scale), _tile(S, 256), S)

    yn = _ssd(proj, dt_raw,
              conv_w[:, :SSD_INNER].astype(F32), _row(conv_b[:SSD_INNER]),
              conv_w[:, SSD_INNER:].astype(F32), _row(conv_b[SSD_INNER:]),
              _row(dt_bias, LANES), _row(a_log, LANES),
              _row(jnp.repeat(d_skip, SSD_HEADDIM)), _row(ssd_norm_w), B, S)

    rw = jnp.pad(router_w.astype(F32), ((0, 0), (0, LANES - N_EXPERTS)))
    rw_hi = rw.astype(BF16)
    rw_lo = (rw - rw_hi.astype(F32)).astype(BF16)
    h, u_pk, er, gates, cnt = _mixout(pool_g, yn, proj, x2, w_ssd_out.astype(BF16), w_out.astype(BF16),
                                      _row(norm_ffn_w), jnp.concatenate([rw_hi, rw_lo], axis=1),
                                      _row(router_b, LANES), _tile(T, 256))

    TK = T * TOP_K
    tm_ffn = 2304 if TK >= 32768 else 256
    sub = 256
    n_tiles = -(-TK // tm_ffn) + N_EXPERTS
    counts = cnt[0, :N_EXPERTS].astype(jnp.int32)
    tiles_e = (counts + tm_ffn - 1) // tm_ffn
    tend = jnp.cumsum(tiles_e)
    tstart = tend - tiles_e
    pstart = (tstart * tm_ffn).astype(jnp.int32)
    tile_id = jnp.arange(n_tiles, dtype=jnp.int32)
    tile_expert = jnp.minimum(jnp.searchsorted(tend, tile_id, side='right'), N_EXPERTS - 1).astype(jnp.int32)
    tile_rows = jnp.clip(counts[tile_expert] - (tile_id - tstart[tile_expert]) * tm_ffn, 0, tm_ffn).astype(jnp.int32)
    n_used = tend[-1].astype(jnp.int32).reshape(1)
    e_flat = er[:, :TOP_K].reshape(TK)
    r_flat = er[:, TOP_K:2 * TOP_K].reshape(TK)

    xb, dest = _dispatch(pstart, e_flat, r_flat, u_pk, n_tiles * tm_ffn, _tile(T, 1024))
    yb = _ffn(tile_expert, n_used, tile_rows, xb, w_up, b_up.reshape(N_EXPERTS, 1, 2 * D_FF),
              w_down, b_down.reshape(N_EXPERTS, 1, D_MODEL), tm_ffn, sub, 512, 512)
    out = _combine(dest, h, gates, _row(norm_final_w), yb, _tile(T, 256))
    return out.reshape(B, S, D)
```

```python
import functools

import jax
import jax.numpy as jnp
from jax import lax
from jax.experimental import pallas as pl
from jax.experimental.pallas import tpu as pltpu

F32 = jnp.float32
BF16 = jnp.bfloat16

D_MODEL = 2048
EPS = 1e-5
POOL_GROUPS = 4
POOL_WINDOWS = (2, 4, 8, 16)
POOL_GW = D_MODEL // POOL_GROUPS
SSD_INNER = 2 * D_MODEL
SSD_HEADDIM = 64
SSD_HEADS = SSD_INNER // SSD_HEADDIM
SSD_GROUPS = 8
SSD_HPG = SSD_HEADS // SSD_GROUPS
SSD_STATE = 128
SSD_CONV = 4
SSD_CHUNK = 128
SSD_GW = SSD_HPG * SSD_HEADDIM
SSD_BC = SSD_GROUPS * SSD_STATE
N_EXPERTS = 32
TOP_K = 4
D_FF = D_MODEL
SWIGLU_ALPHA = 1.702
SWIGLU_LIMIT = 7.0

LANES = 128
BF16_ROWS = 16
HALF = D_MODEL // 2
CONV_SLAB = 256
QUAD = 4
QUAD_W = QUAD * SSD_HEADDIM
SSD_SUBS = 2
INPROJ_TN = 2048
FFN_BIG_BLOCK = 1024
ROW_GROUP = 8
VMEM_LIMIT = 56 * 1024 * 1024

COL_Z = 0
COL_XS = SSD_INNER
COL_BC = 2 * SSD_INNER
COL_P = 2 * SSD_INNER + 2 * SSD_BC
COL_GP = COL_P + D_MODEL
COL_GS = COL_GP + D_MODEL
PROJ_COLS = COL_GS + D_MODEL

NEG_BIG = -1e30


def _cparams(sem):
    return pltpu.CompilerParams(dimension_semantics=sem, vmem_limit_bytes=VMEM_LIMIT)


_NT = (((1,), (1,)), ((), ()))


def _inproj_kernel(x_ref, nw_ref, w_ref, wdt_ref, o_ref, dt_ref, u_scr):
    @pl.when(pl.program_id(1) == 0)
    def _():
        x = x_ref[...]
        ms = jnp.mean(x * x, axis=-1, keepdims=True)
        u = (x * lax.rsqrt(ms + EPS) * nw_ref[...]).astype(BF16)
        u_scr[...] = u
        dt_ref[...] = lax.dot_general(u, wdt_ref[...], _NT, preferred_element_type=F32)

    o_ref[...] = lax.dot_general(u_scr[...], w_ref[...], _NT, preferred_element_type=F32).astype(o_ref.dtype)


def _inproj(x2, norm_w, w_t, w_dt_t, row_of_block, tm, tn):
    T = x2.shape[0]

    def w_rows(i, j):
        row = row_of_block[-1]
        for b in range(len(row_of_block) - 2, -1, -1):
            row = jnp.where(j == b, row_of_block[b], row)
        return pl.multiple_of(row, BF16_ROWS), 0

    return pl.pallas_call(
        _inproj_kernel,
        grid=(T // tm, PROJ_COLS // tn),
        in_specs=[
            pl.BlockSpec((tm, D_MODEL), lambda i, j: (i, 0)),
            pl.BlockSpec((1, D_MODEL), lambda i, j: (0, 0)),
            pl.BlockSpec((pl.Element(tn), pl.Element(D_MODEL)), w_rows),
            pl.BlockSpec((LANES, D_MODEL), lambda i, j: (0, 0)),
        ],
        out_specs=[
            pl.BlockSpec((tm, tn), lambda i, j: (i, j)),
            pl.BlockSpec((tm, LANES), lambda i, j: (i, 0)),
        ],
        out_shape=[
            jax.ShapeDtypeStruct((T, PROJ_COLS), BF16),
            jax.ShapeDtypeStruct((T, LANES), F32),
        ],
        scratch_shapes=[pltpu.VMEM((tm, D_MODEL), BF16)],
        compiler_params=_cparams(("parallel", "arbitrary")),
        name="inproj",
    )(x2, norm_w, w_t, w_dt_t)


def _pool_kernel(prev_ref, p_ref, g_ref, band_ref, hband_ref, wp_ref, ps_ref, o_ref, *, tm, seq):
    row0 = (pl.program_id(0) * tm) % seq
    pos = row0 + lax.broadcasted_iota(jnp.int32, (tm, 1), 0)
    for g, w in enumerate(POOL_WINDOWS):
        cols = slice(g * POOL_GW, (g + 1) * POOL_GW)
        cur = p_ref[:, cols]
        prev = prev_ref[:, cols]
        prev = jnp.where(row0 == 0, jnp.zeros_like(prev), prev)
        acc = (jnp.dot(band_ref[g], cur, preferred_element_type=F32)
               + jnp.dot(hband_ref[g], prev, preferred_element_type=F32))
        cnt = jnp.minimum(pos + 1, w).astype(F32)
        pooled = acc / cnt - cur.astype(F32)
        y = jnp.dot(pooled.astype(BF16), wp_ref[g], preferred_element_type=F32)
        y = y * ps_ref[:, cols] * jax.nn.sigmoid(g_ref[:, cols].astype(F32))
        o_ref[:, cols] = y.astype(o_ref.dtype)


def _pool(proj, w_pool, pool_scale, tm, seq):
    T = proj.shape[0]
    hist = max(POOL_WINDOWS)
    t_out = jnp.arange(tm, dtype=jnp.int32)[None, :, None]
    src = jnp.arange(hist + tm, dtype=jnp.int32)[None, None, :]
    win = jnp.asarray(POOL_WINDOWS, jnp.int32)[:, None, None]
    band = jnp.logical_and(src > hist + t_out - win, src <= hist + t_out).astype(BF16)
    return pl.pallas_call(
        functools.partial(_pool_kernel, tm=tm, seq=seq),
        grid=(T // tm,),
        in_specs=[
            pl.BlockSpec((hist, D_MODEL), lambda i: (jnp.maximum(i * (tm // hist) - 1, 0), COL_P // D_MODEL)),
            pl.BlockSpec((tm, D_MODEL), lambda i: (i, COL_P // D_MODEL)),
            pl.BlockSpec((tm, D_MODEL), lambda i: (i, COL_GP // D_MODEL)),
            pl.BlockSpec((POOL_GROUPS, tm, tm), lambda i: (0, 0, 0)),
            pl.BlockSpec((POOL_GROUPS, tm, hist), lambda i: (0, 0, 0)),
            pl.BlockSpec((POOL_GROUPS, POOL_GW, POOL_GW), lambda i: (0, 0, 0)),
            pl.BlockSpec((1, D_MODEL), lambda i: (0, 0)),
        ],
        out_specs=pl.BlockSpec((tm, D_MODEL), lambda i: (i, 0)),
        out_shape=jax.ShapeDtypeStruct((T, D_MODEL), BF16),
        compiler_params=_cparams(("parallel",)),
        name="pool",
    )(proj, proj, proj, band[:, :, hist:], band[:, :, :hist], w_pool, pool_scale)


def _split3(v):
    hi = v.astype(BF16)
    r1 = v - hi.astype(F32)
    mid = r1.astype(BF16)
    lo = (r1 - mid.astype(F32)).astype(BF16)
    return hi, mid, lo


def _ssd_kernel(z_ref, xs_ref, bc_ref, dt_ref, cwx_ref, cbx_ref, cwbc_ref, cbbc_ref,
                dtb_ref, alog_ref, dsk_ref, nw_ref, qm_ref, sh_ref, o_ref,
                prev_x, prev_bc, xs_scr, b_scr, c_scr, a_scr, qt_scr, wt_scr, state):
    L = SSD_CHUNK

    @pl.when(pl.program_id(1) == 0)
    def _():
        prev_x[...] = jnp.zeros_like(prev_x)
        prev_bc[...] = jnp.zeros_like(prev_bc)
        state[...] = jnp.zeros_like(state)

    rr = lax.broadcasted_iota(jnp.int32, (L, L), 0)
    cc = lax.broadcasted_iota(jnp.int32, (L, L), 1)
    causal = rr >= cc
    low_half = cc < SSD_HEADDIM
    tri = jnp.where(causal, 1.0, 0.0).astype(BF16)

    def conv_slab(sub, src_ref, prev_ref, cw_ref, cb_ref, col):
        cols = slice(col, col + CONV_SLAB)
        raw = src_ref[sub * L:(sub + 1) * L, cols]
        before = prev_ref[:, cols] if sub == 0 else src_ref[(sub - 1) * L:sub * L, cols]
        both = jnp.concatenate([before, raw], axis=0)
        acc = cb_ref[:, cols] + cw_ref[SSD_CONV - 1:SSD_CONV, cols] * raw.astype(F32)
        for k in range(SSD_CONV - 1):
            shifted = jnp.dot(sh_ref[k * L:(k + 1) * L, :], both, preferred_element_type=F32)
            acc = acc + cw_ref[k:k + 1, cols] * shifted
        return acc * jax.nn.sigmoid(acc)

    def chunk(sub):
        rows = slice(sub * L, (sub + 1) * L)
        for s in range(SSD_INNER // CONV_SLAB):
            xs_scr[sub, :, s * CONV_SLAB:(s + 1) * CONV_SLAB] = conv_slab(
                sub, xs_ref, prev_x, cwx_ref, cbx_ref, s * CONV_SLAB)
        for s in range(SSD_BC // CONV_SLAB):
            b_scr[sub, :, s * CONV_SLAB:(s + 1) * CONV_SLAB] = conv_slab(
                sub, bc_ref, prev_bc, cwbc_ref, cbbc_ref, s * CONV_SLAB)
        for s in range(SSD_BC // CONV_SLAB):
            c_scr[sub, :, s * CONV_SLAB:(s + 1) * CONV_SLAB] = conv_slab(
                sub, bc_ref, prev_bc, cwbc_ref, cbbc_ref, SSD_BC + s * CONV_SLAB).astype(BF16)

        dtv = jax.nn.softplus(dt_ref[rows, :] + dtb_ref[...])
        d_a = dtv * (-jnp.exp(alog_ref[...]))
        a_cum = sum(jnp.dot(tri, part, preferred_element_type=F32) for part in _split3(d_a))
        a_t = a_cum.T
        dt_t = dtv.T
        a_scr[sub] = a_cum
        qt_scr[sub] = a_t - jnp.log(dt_t)
        wt_scr[sub] = dt_t * jnp.exp(a_t[:, L - 1:L] - a_t)

        def group_body(g):
            off = g * SSD_GW
            offn = g * SSD_STATE
            xg = xs_scr[sub, :, pl.ds(off, SSD_GW)]
            xg_b = xg.astype(BF16)
            b_f = b_scr[sub, :, pl.ds(offn, SSD_STATE)]
            c_b = c_scr[sub, :, pl.ds(offn, SSD_STATE)]
            b_t = b_f.T
            cbm = lax.dot_general(c_b, b_f.astype(BF16), (((1,), (1,)), ((), ())),
                                  preferred_element_type=F32)
            s_g = state[g]
            y_off = jnp.dot(c_b, s_g.astype(BF16), preferred_element_type=F32)
            a_all = a_scr[sub]
            ys = []
            for quad in range(SSD_HPG // QUAD):
                qs = slice(quad * QUAD_W, (quad + 1) * QUAD_W)
                m_parts, b_parts, a_cols = [], [], []
                for jj in range(QUAD):
                    j = quad * QUAD + jj
                    h = g * SSD_HPG + j
                    a_col = jnp.broadcast_to(a_all[:, h:h + 1], (L, L))
                    seg = a_col - qt_scr[sub, pl.ds(h, 1), :]
                    m_parts.append((cbm * jnp.exp(jnp.where(causal, seg, NEG_BIG))).astype(BF16))
                    b_parts.append((b_t * wt_scr[sub, pl.ds(h, 1), :]).astype(BF16))
                    a_cols.append(a_col)
                lhs = jnp.concatenate([jnp.concatenate(m_parts, axis=1), jnp.concatenate(b_parts, axis=1)], axis=0)
                xq = xg_b[:, qs]
                rhs = jnp.concatenate([xq * qm_ref[jj] for jj in range(QUAD)], axis=0)
                res = jnp.dot(lhs, rhs, preferred_element_type=F32)
                ea_q = jnp.concatenate(
                    [jnp.exp(jnp.where(low_half, a_cols[2 * p], a_cols[2 * p + 1])) for p in range(QUAD // 2)],
                    axis=1)
                ys.append(res[:L] + y_off[:, qs] * ea_q)
                state[g, :, qs] = s_g[:, qs] * ea_q[L - 1:L, :] + res[L:]
            y = jnp.concatenate(ys, axis=1) + dsk_ref[:, pl.ds(off, SSD_GW)] * xg
            zg = z_ref[rows, pl.ds(off, SSD_GW)].astype(F32)
            yg = y * (zg * jax.nn.sigmoid(zg))
            ms = jnp.mean(yg * yg, axis=-1, keepdims=True)
            yn = yg * lax.rsqrt(ms + EPS) * nw_ref[:, pl.ds(off, SSD_GW)]
            o_ref[rows, pl.ds(off, SSD_GW)] = yn.astype(o_ref.dtype)

        for g in range(SSD_GROUPS):
            group_body(g)

    for sub in range(SSD_SUBS):
        chunk(sub)
    prev_x[...] = xs_ref[(SSD_SUBS - 1) * L:, :]
    prev_bc[...] = bc_ref[(SSD_SUBS - 1) * L:, :]


def _ssd(proj, dt_raw, cwx, cbx, cwbc, cbbc, dtb, alog, dsk, nw, batch, seq):
    L = SSD_CHUNK
    LS = SSD_SUBS * L
    nc = seq // LS
    T = proj.shape[0]
    row = lambda b, c: b * nc + c
    const = lambda b, c: (0, 0)
    head_of_lane = jnp.arange(QUAD_W, dtype=jnp.int32) // SSD_HEADDIM
    qmask = (head_of_lane[None, None, :] == jnp.arange(QUAD, dtype=jnp.int32)[:, None, None])
    qmask = jnp.broadcast_to(qmask, (QUAD, L, QUAD_W)).astype(BF16)
    tap = jnp.arange(SSD_CONV - 1, dtype=jnp.int32)[:, None, None]
    t_out = jnp.arange(L, dtype=jnp.int32)[None, :, None]
    src = jnp.arange(2 * L, dtype=jnp.int32)[None, None, :]
    shift = (src == L + t_out - (SSD_CONV - 1) + tap).astype(BF16).reshape((SSD_CONV - 1) * L, 2 * L)
    return pl.pallas_call(
        _ssd_kernel,
        grid=(batch, nc),
        in_specs=[
            pl.BlockSpec((LS, SSD_INNER), lambda b, c: (row(b, c), COL_Z // SSD_INNER)),
            pl.BlockSpec((LS, SSD_INNER), lambda b, c: (row(b, c), COL_XS // SSD_INNER)),
            pl.BlockSpec((LS, 2 * SSD_BC), lambda b, c: (row(b, c), COL_BC // (2 * SSD_BC))),
            pl.BlockSpec((LS, LANES), lambda b, c: (row(b, c), 0)),
            pl.BlockSpec((SSD_CONV, SSD_INNER), const),
            pl.BlockSpec((1, SSD_INNER), const),
            pl.BlockSpec((SSD_CONV, 2 * SSD_BC), const),
            pl.BlockSpec((1, 2 * SSD_BC), const),
            pl.BlockSpec((1, LANES), const),
            pl.BlockSpec((1, LANES), const),
            pl.BlockSpec((1, SSD_INNER), const),
            pl.BlockSpec((1, SSD_INNER), const),
            pl.BlockSpec((QUAD, L, QUAD_W), lambda b, c: (0, 0, 0)),
            pl.BlockSpec(((SSD_CONV - 1) * L, 2 * L), const),
        ],
        out_specs=pl.BlockSpec((LS, SSD_INNER), lambda b, c: (row(b, c), 0)),
        out_shape=jax.ShapeDtypeStruct((T, SSD_INNER), BF16),
        scratch_shapes=[
            pltpu.VMEM((L, SSD_INNER), BF16),
            pltpu.VMEM((L, 2 * SSD_BC), BF16),
            pltpu.VMEM((SSD_SUBS, L, SSD_INNER), F32),
            pltpu.VMEM((SSD_SUBS, L, SSD_BC), F32),
            pltpu.VMEM((SSD_SUBS, L, SSD_BC), BF16),
            pltpu.VMEM((SSD_SUBS, L, LANES), F32),
            pltpu.VMEM((SSD_SUBS, LANES, L), F32),
            pltpu.VMEM((SSD_SUBS, LANES, L), F32),
            pltpu.VMEM((SSD_GROUPS, SSD_STATE, SSD_GW), F32),
        ],
        compiler_params=_cparams(("arbitrary", "arbitrary")),
        name="ssd",
    )(proj, proj, proj, dt_raw, cwx, cbx, cwbc, cbbc, dtb, alog, dsk, nw, qmask, shift)


def _pack_bf16_pairs(v):
    bits = pltpu.bitcast(v.astype(BF16).astype(F32), jnp.uint32)
    return (bits[:, :HALF] >> 16) | (bits[:, HALF:] & jnp.uint32(0xFFFF0000))


def _unpack_lo(w):
    return pltpu.bitcast(w << 16, F32)


def _unpack_hi(w):
    return pltpu.bitcast(w & jnp.uint32(0xFFFF0000), F32)


def _mixout_kernel(pg_ref, yn_ref, gs_ref, x_ref, wso_ref, wo_ref, nw_ref, rw_ref, rb_ref,
                   h_ref, u_ref, er_ref, gt_ref, cnt_ref, run_scr, lg_scr, *, tm):
    i = pl.program_id(0)
    last = pl.num_programs(0) - 1

    @pl.when(i == 0)
    def _():
        run_scr[...] = jnp.zeros_like(run_scr)
        lg_scr[...] = jnp.zeros_like(lg_scr)

    def route_previous_tile(lg_prev):
        er, gt = _route(lg_prev, run_scr, tm, i > 0)
        er_ref[...] = er
        gt_ref[...] = gt
        cnt_ref[...] = run_scr[...]

    @pl.when(i < last)
    def _():
        lg_prev = lg_scr[...]
        ssd = jnp.dot(yn_ref[...], wso_ref[...], preferred_element_type=F32)
        merged = pg_ref[...].astype(F32) + jax.nn.sigmoid(gs_ref[...].astype(F32)) * ssd
        h = x_ref[...] + jnp.dot(merged.astype(BF16), wo_ref[...], preferred_element_type=F32)
        h_ref[...] = h
        ms = jnp.mean(h * h, axis=-1, keepdims=True)
        u = h * lax.rsqrt(ms + EPS) * nw_ref[...]
        u_hi = u.astype(BF16)
        u_ref[...] = _pack_bf16_pairs(u)
        u_lo = (u - u_hi.astype(F32)).astype(BF16)
        both = jnp.dot(u_hi, rw_ref[...], preferred_element_type=F32)
        lg = both[:, :LANES] + both[:, LANES:] + jnp.dot(u_lo, rw_ref[:, :LANES], preferred_element_type=F32)
        lg_scr[...] = lg + rb_ref[...]
        route_previous_tile(lg_prev)

    @pl.when(i == last)
    def _():
        route_previous_tile(lg_scr[...])


def _mixout(pool_g, yn, proj, x2, wso, wo, nw, rw, rb, tm):
    T = x2.shape[0]
    n = T // tm
    const = lambda i: (0, 0)
    cur = lambda i: jnp.minimum(i, n - 1)
    prev = lambda i: jnp.maximum(i - 1, 0)
    return pl.pallas_call(
        functools.partial(_mixout_kernel, tm=tm),
        grid=(n + 1,),
        in_specs=[
            pl.BlockSpec((tm, D_MODEL), lambda i: (cur(i), 0)),
            pl.BlockSpec((tm, SSD_INNER), lambda i: (cur(i), 0)),
            pl.BlockSpec((tm, D_MODEL), lambda i: (cur(i), COL_GS // D_MODEL)),
            pl.BlockSpec((tm, D_MODEL), lambda i: (cur(i), 0)),
            pl.BlockSpec((SSD_INNER, D_MODEL), const),
            pl.BlockSpec((D_MODEL, D_MODEL), const),
            pl.BlockSpec((1, D_MODEL), const),
            pl.BlockSpec((D_MODEL, 2 * LANES), const),
            pl.BlockSpec((1, LANES), const),
        ],
        out_specs=[
            pl.BlockSpec((tm, D_MODEL), lambda i: (cur(i), 0)),
            pl.BlockSpec((tm, HALF), lambda i: (cur(i), 0)),
            pl.BlockSpec((tm, LANES), lambda i: (prev(i), 0)),
            pl.BlockSpec((tm, LANES), lambda i: (prev(i), 0)),
            pl.BlockSpec((1, LANES), const),
        ],
        out_shape=[
            jax.ShapeDtypeStruct((T, D_MODEL), F32),
            jax.ShapeDtypeStruct((T, HALF), jnp.uint32),
            jax.ShapeDtypeStruct((T, LANES), jnp.int32),
            jax.ShapeDtypeStruct((T, LANES), F32),
            jax.ShapeDtypeStruct((1, LANES), F32),
        ],
        scratch_shapes=[pltpu.VMEM((1, LANES), F32), pltpu.VMEM((tm, LANES), F32)],
        compiler_params=_cparams(("arbitrary",)),
        name="mixout",
    )(pool_g, yn, proj, x2, wso, wo, nw, rw, rb)


def _route(lg, run_scr, tm, valid):
    lane = lax.broadcasted_iota(jnp.int32, (tm, LANES), 1)
    lg = jnp.where(lane < N_EXPERTS, lg, -jnp.inf)
    idxs, vals, hots = [], [], []
    for _ in range(TOP_K):
        m = jnp.max(lg, axis=-1, keepdims=True)
        idx = jnp.min(jnp.where(lg == m, lane, LANES), axis=-1, keepdims=True)
        hot = lane == idx
        lg = jnp.where(hot, -jnp.inf, lg)
        idxs.append(idx)
        vals.append(m)
        hots.append(hot)
    exps = [jnp.exp(v - vals[0]) for v in vals]
    den = exps[0] + exps[1] + exps[2] + exps[3]
    cnt = sum(jnp.where(hot, 1.0, 0.0) for hot in hots)
    rr = lax.broadcasted_iota(jnp.int32, (tm, tm), 0)
    cc = lax.broadcasted_iota(jnp.int32, (tm, tm), 1)
    before = jnp.where(rr > cc, 1.0, 0.0).astype(BF16)
    base = jnp.dot(before, cnt.astype(BF16), preferred_element_type=F32) + run_scr[...]
    er = jnp.zeros((tm, LANES), jnp.int32)
    gt = jnp.zeros((tm, LANES), F32)
    for k in range(TOP_K):
        rank = jnp.sum(jnp.where(hots[k], base, 0.0), axis=-1, keepdims=True)
        er = jnp.where(lane == k, idxs[k], er)
        er = jnp.where(lane == TOP_K + k, rank.astype(jnp.int32), er)
        gt = jnp.where(lane == k, exps[k] / den, gt)
    run_scr[...] += jnp.where(valid, jnp.sum(cnt, axis=0, keepdims=True), 0.0)
    return er, gt


def _dispatch_kernel(ps_ref, e_ref, r_ref, u_ref, xb_ref, dest_ref, sem, *, tm):
    def body(g, carry):
        r0 = pl.multiple_of(g * ROW_GROUP, ROW_GROUP)
        for i in range(ROW_GROUP):
            for k in range(TOP_K):
                j = (r0 + i) * TOP_K + k
                d = ps_ref[e_ref[j]] + r_ref[j]
                dest_ref[j] = d
                pltpu.make_async_copy(u_ref.at[pl.ds(r0 + i, 1)], xb_ref.at[pl.ds(d, 1)], sem).start()
        return carry

    lax.fori_loop(0, tm // ROW_GROUP, body, 0)
    for _ in range(TOP_K):
        pltpu.make_async_copy(u_ref, xb_ref.at[pl.ds(0, tm)], sem).wait()


def _dispatch(pstart, e_flat, r_flat, u_pk, n_slots, tm):
    T = u_pk.shape[0]
    grid_spec = pltpu.PrefetchScalarGridSpec(
        num_scalar_prefetch=1,
        grid=(T // tm,),
        in_specs=[
            pl.BlockSpec((tm * TOP_K,), lambda i, ps: (i,), memory_space=pltpu.SMEM),
            pl.BlockSpec((tm * TOP_K,), lambda i, ps: (i,), memory_space=pltpu.SMEM),
            pl.BlockSpec((tm, HALF), lambda i, ps: (i, 0)),
        ],
        out_specs=[
            pl.BlockSpec(memory_space=pl.ANY),
            pl.BlockSpec((tm * TOP_K,), lambda i, ps: (i,), memory_space=pltpu.SMEM),
        ],
        scratch_shapes=[pltpu.SemaphoreType.DMA(())],
    )
    return pl.pallas_call(
        functools.partial(_dispatch_kernel, tm=tm),
        grid_spec=grid_spec,
        out_shape=[
            jax.ShapeDtypeStruct((n_slots, HALF), jnp.uint32),
            jax.ShapeDtypeStruct((T * TOP_K,), jnp.int32),
        ],
        compiler_params=_cparams(("arbitrary",)),
        name="dispatch",
    )(pstart, e_flat, r_flat, u_pk)


def _row_blocks(nrows, tm, sub, weights, block):
    nblk = (nrows + sub - 1) // sub
    per_big = min(FFN_BIG_BLOCK, tm) // sub
    nbig = tm // (per_big * sub)
    for b in range(nbig):
        @pl.when(nblk >= (b + 1) * per_big)
        def _():
            block(b * per_big * sub, per_big * sub, weights())

    def rest(sb, carry):
        block(pl.multiple_of(sb * sub, sub), sub, weights())
        return carry

    lax.fori_loop(jnp.minimum(nblk // per_big, nbig) * per_big, nblk, rest, 0)


def _ffn_up_kernel(te_ref, nu_ref, tr_ref, x_ref, wg_ref, wl_ref, bg_ref, bl_ref, o_ref, *, tm, sub):
    nrows = tr_ref[pl.program_id(0)]

    def weights():
        return wg_ref[0].astype(BF16), wl_ref[0].astype(BF16)

    def block(r0, nr, w):
        packed = x_ref[pl.ds(r0, nr), :]
        x = jnp.concatenate([_unpack_lo(packed).astype(BF16), _unpack_hi(packed).astype(BF16)], axis=1)
        glu = jnp.minimum(jnp.dot(x, w[0], preferred_element_type=F32) + bg_ref[0], SWIGLU_LIMIT)
        lin = jnp.clip(jnp.dot(x, w[1], preferred_element_type=F32) + bl_ref[0], -SWIGLU_LIMIT, SWIGLU_LIMIT)
        act = glu * jax.nn.sigmoid(SWIGLU_ALPHA * glu) * (lin + 1.0)
        o_ref[pl.ds(r0, nr), :] = act.astype(o_ref.dtype)

    _row_blocks(nrows, tm, sub, weights, block)


def _ffn_down_kernel(te_ref, nu_ref, tr_ref, a_ref, wa_ref, wb_ref, ba_ref, bb_ref, o_ref, *, tm, sub):
    nrows = tr_ref[pl.program_id(0)]

    def weights():
        return wa_ref[0].astype(BF16), wb_ref[0].astype(BF16)

    def block(r0, nr, w):
        a = a_ref[pl.ds(r0, nr), :]
        ya = jnp.dot(a, w[0], preferred_element_type=F32) + ba_ref[0]
        yb = jnp.dot(a, w[1], preferred_element_type=F32) + bb_ref[0]
        lo = pltpu.bitcast(ya.astype(BF16).astype(F32), jnp.uint32) >> 16
        hi = pltpu.bitcast(yb.astype(BF16).astype(F32), jnp.uint32) & jnp.uint32(0xFFFF0000)
        o_ref[pl.ds(r0, nr), :] = lo | hi

    _row_blocks(nrows, tm, sub, weights, block)


def _ffn(tile_expert, n_used, tile_rows, xb, w_up, b_up3, w_down, b_down3, tm, sub, fc, nc):
    n_slots = xb.shape[0]
    n_tiles = n_slots // tm
    nf = D_FF // fc
    nj = HALF // nc

    def live(i, nu):
        return jnp.minimum(i, nu[0] - 1)

    def frozen(i, j, nu, last):
        return jnp.where(i < nu[0], j, last)

    up_spec = pltpu.PrefetchScalarGridSpec(
        num_scalar_prefetch=3,
        grid=(n_tiles, nf),
        in_specs=[
            pl.BlockSpec((tm, HALF), lambda i, f, te, nu, tr: (live(i, nu), 0)),
            pl.BlockSpec((1, D_MODEL, fc), lambda i, f, te, nu, tr: (te[i], 0, frozen(i, f, nu, nf - 1))),
            pl.BlockSpec((1, D_MODEL, fc), lambda i, f, te, nu, tr: (te[i], 0, nf + frozen(i, f, nu, nf - 1))),
            pl.BlockSpec((1, 1, fc), lambda i, f, te, nu, tr: (te[i], 0, frozen(i, f, nu, nf - 1))),
            pl.BlockSpec((1, 1, fc), lambda i, f, te, nu, tr: (te[i], 0, nf + frozen(i, f, nu, nf - 1))),
        ],
        out_specs=pl.BlockSpec((tm, fc), lambda i, f, te, nu, tr: (live(i, nu), frozen(i, f, nu, nf - 1))),
    )
    act = pl.pallas_call(
        functools.partial(_ffn_up_kernel, tm=tm, sub=sub),
        grid_spec=up_spec,
        out_shape=jax.ShapeDtypeStruct((n_slots, D_FF), BF16),
        compiler_params=_cparams(("arbitrary", "arbitrary")),
        name="ffn_up",
    )(tile_expert, n_used, tile_rows, xb, w_up, w_up, b_up3, b_up3)

    down_spec = pltpu.PrefetchScalarGridSpec(
        num_scalar_prefetch=3,
        grid=(n_tiles, nj),
        in_specs=[
            pl.BlockSpec((tm, D_FF), lambda i, j, te, nu, tr: (live(i, nu), 0)),
            pl.BlockSpec((1, D_FF, nc), lambda i, j, te, nu, tr: (te[i], 0, frozen(i, j, nu, nj - 1))),
            pl.BlockSpec((1, D_FF, nc), lambda i, j, te, nu, tr: (te[i], 0, nj + frozen(i, j, nu, nj - 1))),
            pl.BlockSpec((1, 1, nc), lambda i, j, te, nu, tr: (te[i], 0, frozen(i, j, nu, nj - 1))),
            pl.BlockSpec((1, 1, nc), lambda i, j, te, nu, tr: (te[i], 0, nj + frozen(i, j, nu, nj - 1))),
        ],
        out_specs=pl.BlockSpec((tm, nc), lambda i, j, te, nu, tr: (live(i, nu), frozen(i, j, nu, nj - 1))),
    )
    return pl.pallas_call(
        functools.partial(_ffn_down_kernel, tm=tm, sub=sub),
        grid_spec=down_spec,
        out_shape=jax.ShapeDtypeStruct((n_slots, HALF), jnp.uint32),
        compiler_params=_cparams(("arbitrary", "arbitrary")),
        name="ffn_down",
    )(tile_expert, n_used, tile_rows, act, w_down, w_down, b_down3, b_down3)


def _combine_kernel(d_ref, dn_ref, h_ref, g_ref, nw_ref, yb_ref, o_ref, ybuf, sem, *, tm):
    step = pl.program_id(0)
    slot = step % 2

    def gather_rows(idx_ref, s):
        def body(g, carry):
            r0 = pl.multiple_of(g * ROW_GROUP, ROW_GROUP)
            for i in range(ROW_GROUP):
                for k in range(TOP_K):
                    d = idx_ref[(r0 + i) * TOP_K + k]
                    pltpu.make_async_copy(yb_ref.at[pl.ds(d, 1)], ybuf.at[s, k, pl.ds(r0 + i, 1)],
                                          sem.at[s]).start()
            return carry

        lax.fori_loop(0, tm // ROW_GROUP, body, 0)

    @pl.when(step == 0)
    def _():
        gather_rows(d_ref, 0)

    @pl.when(step + 1 < pl.num_programs(0))
    def _():
        gather_rows(dn_ref, 1 - slot)

    for k in range(TOP_K):
        pltpu.make_async_copy(yb_ref.at[pl.ds(0, tm)], ybuf.at[slot, k], sem.at[slot]).wait()
    lo = h_ref[:, :HALF]
    hi = h_ref[:, HALF:]
    for k in range(TOP_K):
        w = ybuf[slot, k]
        g = g_ref[:, k:k + 1]
        lo = lo + g * _unpack_lo(w)
        hi = hi + g * _unpack_hi(w)
    ms = (jnp.sum(lo * lo, axis=-1, keepdims=True) + jnp.sum(hi * hi, axis=-1, keepdims=True)) / D_MODEL
    scale = lax.rsqrt(ms + EPS)
    o_ref[:, :HALF] = lo * scale * nw_ref[:, :HALF]
    o_ref[:, HALF:] = hi * scale * nw_ref[:, HALF:]


def _combine(dest, h, gates, nw, yb, tm):
    T = h.shape[0]
    n = T // tm
    return pl.pallas_call(
        functools.partial(_combine_kernel, tm=tm),
        grid=(n,),
        in_specs=[
            pl.BlockSpec((tm * TOP_K,), lambda i: (i,), memory_space=pltpu.SMEM),
            pl.BlockSpec((tm * TOP_K,), lambda i: (jnp.minimum(i + 1, n - 1),), memory_space=pltpu.SMEM),
            pl.BlockSpec((tm, D_MODEL), lambda i: (i, 0)),
            pl.BlockSpec((tm, LANES), lambda i: (i, 0)),
            pl.BlockSpec((1, D_MODEL), lambda i: (0, 0)),
            pl.BlockSpec(memory_space=pl.ANY),
        ],
        out_specs=pl.BlockSpec((tm, D_MODEL), lambda i: (i, 0)),
        out_shape=jax.ShapeDtypeStruct((T, D_MODEL), F32),
        scratch_shapes=[pltpu.VMEM((2, TOP_K, tm, HALF), jnp.uint32), pltpu.SemaphoreType.DMA((2,))],
        compiler_params=_cparams(("arbitrary",)),
        name="combine",
    )(dest, dest, h, gates, nw, yb)


def _row(v, width=None):
    v = v.astype(F32).reshape(1, -1)
    if width is not None and v.shape[1] < width:
        v = jnp.pad(v, ((0, 0), (0, width - v.shape[1])))
    return v


def _tile(n, pref):
    t = pref
    while n % t:
        t //= 2
    return t


def kernel(x, norm_mix_w, w_in, w_pool, pool_scale, conv_w, conv_b, dt_bias, a_log, d_skip,
           ssd_norm_w, w_ssd_out, w_out, norm_ffn_w, router_w, router_b, w_up, b_up,
           w_down, b_down, norm_final_w):
    B, S, D = x.shape
    T = B * S
    x2 = x.reshape(T, D)

    o_p, o_z, o_xbc = 0, D_MODEL, D_MODEL + SSD_INNER
    o_dt = o_xbc + SSD_INNER + 2 * SSD_BC
    o_gp = o_dt + SSD_HEADS
    o_gs = o_gp + D_MODEL
    w_t = w_in.T.astype(BF16)
    w_dt_t = jnp.pad(w_t[o_dt:o_dt + SSD_HEADS], ((0, LANES - SSD_HEADS), (0, 0)))
    tn = INPROJ_TN
    row_of_block = ([o_z + c for c in range(0, SSD_INNER, tn)]
                    + [o_xbc + c for c in range(0, SSD_INNER + 2 * SSD_BC, tn)]
                    + [o_p, o_gp, o_gs])
    proj, dt_raw = _inproj(x2, _row(norm_mix_w), w_t, w_dt_t, row_of_block, _tile(T, 1024), tn)

    pool_g = _pool(proj, w_pool.astype(BF16), _row(pool_scale), _tile(S, 512), S)

    yn = _ssd(proj, dt_raw,
              conv_w[:, :SSD_INNER].astype(F32), _row(conv_b[:SSD_INNER]),
              conv_w[:, SSD_INNER:].astype(F32), _row(conv_b[SSD_INNER:]),
              _row(dt_bias, LANES), _row(a_log, LANES),
              _row(jnp.repeat(d_skip, SSD_HEADDIM)), _row(ssd_norm_w), B, S)

    rw = jnp.pad(router_w.astype(F32), ((0, 0), (0, LANES - N_EXPERTS)))
    rw_hi = rw.astype(BF16)
    rw_lo = (rw - rw_hi.astype(F32)).astype(BF16)
    h, u_pk, er, gates, cnt = _mixout(pool_g, yn, proj, x2, w_ssd_out.astype(BF16), w_out.astype(BF16),
                                      _row(norm_ffn_w), jnp.concatenate([rw_hi, rw_lo], axis=1),
                                      _row(router_b, LANES), _tile(T, 256))

    TK = T * TOP_K
    tm_ffn = 2304 if TK >= 32768 else 256
    sub = 256
    n_tiles = -(-TK // tm_ffn) + N_EXPERTS
    counts = cnt[0, :N_EXPERTS].astype(jnp.int32)
    tiles_e = (counts + tm_ffn - 1) // tm_ffn
    tend = jnp.cumsum(tiles_e)
    tstart = tend - tiles_e
    pstart = (tstart * tm_ffn).astype(jnp.int32)
    tile_id = jnp.arange(n_tiles, dtype=jnp.int32)
    tile_expert = jnp.minimum(jnp.searchsorted(tend, tile_id, side='right'), N_EXPERTS - 1).astype(jnp.int32)
    tile_rows = jnp.clip(counts[tile_expert] - (tile_id - tstart[tile_expert]) * tm_ffn, 0, tm_ffn).astype(jnp.int32)
    n_used = tend[-1].astype(jnp.int32).reshape(1)
    e_flat = er[:, :TOP_K].reshape(TK)
    r_flat = er[:, TOP_K:2 * TOP_K].reshape(TK)

    xb, dest = _dispatch(pstart, e_flat, r_flat, u_pk, n_tiles * tm_ffn, _tile(T, 2048))
    yb = _ffn(tile_expert, n_used, tile_rows, xb, w_up, b_up.reshape(N_EXPERTS, 1, 2 * D_FF),
              w_down, b_down.reshape(N_EXPERTS, 1, D_MODEL), tm_ffn, sub, 512, 512)
    out = _combine(dest, h, gates, _row(norm_final_w), yb, _tile(T, 256))
    return out.reshape(B, S, D)
```
